```python
import jax, jax.numpy as jnp
from jax import lax
import numpy as np

D_MODEL = 4096
BATCH = 2
SEQ = 8192
DEPTH = 2

CHUNK = 64
N_MEM = 256
EPS = 1e-6
D_FF = 256 * ((8 * D_MODEL // 3 + 255) // 256)
M_HEADS = 8
M_DV = D_MODEL // (2 * M_HEADS)
M_DQK = M_DV // 2
M_WIDTH = M_HEADS * M_DV
M_QK_WIDTH = M_HEADS * M_DQK
CONV_K = 4
H_DK = 128
H_HEADS = D_MODEL // (2 * H_DK)
H_DV = D_MODEL // (2 * H_HEADS)
H_F_WIDTH = H_HEADS * H_DK
H_WIDTH = H_HEADS * H_DV
X_HEADS = 4
X_DH = D_MODEL // 16
X_WIDTH = X_HEADS * X_DH
SPLIT_SIZES = (M_QK_WIDTH, M_QK_WIDTH, M_WIDTH, M_WIDTH, M_HEADS, M_HEADS, H_F_WIDTH, H_F_WIDTH, H_WIDTH, H_WIDTH)
IN_WIDTH = 2 * M_QK_WIDTH + 2 * M_WIDTH + 2 * M_HEADS + 2 * H_F_WIDTH + 2 * H_WIDTH

kernel_name = 'hybrid_mlstm_hgrn2_macaron_sandwich'


def rmsnorm(x, g):
    xf = x.astype(jnp.float32)
    y = xf * lax.rsqrt(jnp.mean(xf * xf, axis=-1, keepdims=True) + EPS)
    return y.astype(x.dtype) * g


def head_rmsnorm(x, g):
    B, S, H, d = x.shape
    return rmsnorm(x, jnp.ones((), x.dtype)).reshape(B, S, H * d) * g


def causal_conv(x, w):
    K = w.shape[0]
    S = x.shape[1]
    xp = jnp.pad(x, ((0, 0), (K - 1, 0), (0, 0)))
    return sum(xp[:, j:j + S] * w[j] for j in range(K))


def to_chunks(t):
    B, S = t.shape[:2]
    return t.reshape(B, S // CHUNK, CHUNK, *t.shape[2:]).swapaxes(0, 1)


def from_chunks(t):
    nc, B, L = t.shape[:3]
    return t.swapaxes(0, 1).reshape(B, nc * L, *t.shape[3:])


def mlstm_chunkwise(q, k, v, li, lf):
    B, S, H, dqk = q.shape
    dv = v.shape[-1]
    causal = jnp.tril(jnp.ones((CHUNK, CHUNK), dtype=bool))

    def step(carry, xs):
        C, n, m = carry
        qc, kc, vc, lic, lfc = xs
        b = jnp.cumsum(lfc, axis=1).transpose(0, 2, 1)
        ig = lic.transpose(0, 2, 1)
        g = b[..., -1]
        d_intra = jnp.where(causal, b[..., :, None] - b[..., None, :] + ig[..., None, :], -jnp.inf)
        m_inter = b + m[..., None]
        m_t = jnp.maximum(m_inter, jnp.max(d_intra, axis=-1))
        w_intra = jnp.exp(d_intra - m_t[..., None])
        w_inter = jnp.exp(m_inter - m_t)
        s = jnp.einsum('bthd,bshd->bhts', qc, kc) * w_intra
        num = (jnp.einsum('bhts,bshv->bthv', s, vc)
               + w_inter.transpose(0, 2, 1)[..., None] * jnp.einsum('bthd,bhdv->bthv', qc, C))
        den = jnp.sum(s, axis=-1) + w_inter * jnp.einsum('bthd,bhd->bht', qc, n)
        h = num / jnp.maximum(jnp.abs(den), jnp.exp(-m_t)).transpose(0, 2, 1)[..., None]
        a = g[..., None] - b + ig
        m_new = jnp.maximum(g + m, jnp.max(a, axis=-1))
        wk = jnp.exp(a - m_new[..., None])
        decay = jnp.exp(g + m - m_new)
        C = decay[..., None, None] * C + jnp.einsum('bhs,bshd,bshv->bhdv', wk, kc, vc)
        n = decay[..., None] * n + jnp.einsum('bhs,bshd->bhd', wk, kc)
        return (C, n, m_new), h

    init = (jnp.zeros((B, H, dqk, dv), q.dtype), jnp.zeros((B, H, dqk), q.dtype), jnp.zeros((B, H), q.dtype))
    _, h = lax.scan(step, init, (to_chunks(q), to_chunks(k), to_chunks(v), to_chunks(li), to_chunks(lf)))
    return from_chunks(h)


def hgrn2_chunkwise(q, k, i, lg):
    B, S, H, dk = q.shape
    dv = i.shape[-1]
    causal = jnp.tril(jnp.ones((CHUNK, CHUNK), dtype=bool))

    def step(state, xs):
        qc, kc, ic, lgc = xs
        b = jnp.cumsum(lgc, axis=1)
        g = b[:, -1]
        diff = jnp.where(causal[None, :, :, None, None], b[:, :, None] - b[:, None, :], -jnp.inf)
        a = jnp.einsum('bthc,bshc,btshc->bhts', qc, kc, jnp.exp(diff))
        o = (jnp.einsum('bhts,bshv->bthv', a, ic)
             + jnp.einsum('bthc,bhcv->bthv', qc * jnp.exp(b), state))
        state = jnp.exp(g)[..., None] * state + jnp.einsum('bshc,bshv->bhcv', kc * jnp.exp(g[:, None] - b), ic)
        return state, o

    init = jnp.zeros((B, H, dk, dv), q.dtype)
    _, o = lax.scan(step, init, (to_chunks(q), to_chunks(k), to_chunks(i), to_chunks(lg)))
    return from_chunks(o)


def swiglu_half(h, pre, w_gu, w_down, post):
    u = rmsnorm(h, pre)
    a, gte = jnp.split(u @ w_gu, 2, axis=-1)
    return 0.5 * rmsnorm((jax.nn.silu(a) * gte) @ w_down, post)


def token_mixers(h, layer, pre, w_in, gate_bias, conv_w, m_norm, lower_bounds, h_norm,
                 w_bm, w_bh, w_gate, b_gate, w_out, post):
    B, S, _ = h.shape
    f32 = jnp.float32
    u = rmsnorm(h, pre)
    idx = [int(c) for c in np.cumsum(SPLIT_SIZES)[:-1]]
    mq, mk, mv, mo, mi, mf, hq, hf, hi, hg = jnp.split(u @ w_in, idx, axis=-1)
    qk = jax.nn.silu(causal_conv(jnp.concatenate([mq, mk], axis=-1), conv_w))
    mq, mk = jnp.split(qk, 2, axis=-1)
    q = mq.reshape(B, S, M_HEADS, M_DQK).astype(f32) * (M_DQK ** -0.5)
    k = mk.reshape(B, S, M_HEADS, M_DQK).astype(f32)
    v = mv.reshape(B, S, M_HEADS, M_DV).astype(f32)
    li = (mi + gate_bias[:M_HEADS]).astype(f32)
    lf = jax.nn.log_sigmoid((mf + gate_bias[M_HEADS:]).astype(f32))
    hm = mlstm_chunkwise(q, k, v, li, lf).astype(h.dtype)
    y_m = (head_rmsnorm(hm, m_norm) * jax.nn.sigmoid(mo)) @ w_bm
    p = jax.nn.softmax(lower_bounds.astype(f32), axis=0)
    cs = jnp.cumsum(p, axis=0)
    lb = cs[layer] - cs[0]
    f = lb + (1.0 - lb) * jax.nn.sigmoid(hf.astype(f32))
    lg = jnp.log(f).reshape(B, S, H_HEADS, H_DK)
    kh = (1.0 - f).reshape(B, S, H_HEADS, H_DK)
    oh = hgrn2_chunkwise(hq.astype(f32).reshape(B, S, H_HEADS, H_DK), kh, hi.astype(f32).reshape(B, S, H_HEADS, H_DV), lg)
    y_h = (head_rmsnorm(oh.astype(h.dtype), h_norm) * jax.nn.silu(hg)) @ w_bh
    gm, gh = jnp.split(jax.nn.sigmoid(u @ w_gate + b_gate), 2, axis=-1)
    return rmsnorm((gm * y_m + gh * y_h) @ w_out, post)


def mem_cross_attention(h, mem, pre, mem_norm, w_q, w_kv, w_o, post):
    B, S, _ = h.shape
    M = mem.shape[1]
    q = (rmsnorm(h, pre) @ w_q).reshape(B, S, X_HEADS, X_DH)
    k, v = jnp.split(rmsnorm(mem, mem_norm) @ w_kv, 2, axis=-1)
    k = k.reshape(B, M, X_HEADS, X_DH)
    v = v.reshape(B, M, X_HEADS, X_DH)
    scores = jnp.einsum('bshd,bmhd->bhsm', q, k).astype(jnp.float32) * (X_DH ** -0.5)
    pr = jax.nn.softmax(scores, axis=-1).astype(v.dtype)
    o = jnp.einsum('bhsm,bmhd->bshd', pr, v).reshape(B, S, X_WIDTH)
    return rmsnorm(o @ w_o, post)


def setup_inputs(seed: int = 0) -> dict:
    key = jax.random.key(seed)
    ks = jax.random.split(key, 29)
    f32 = jnp.float32

    def nrm(k, shape, scale):
        return scale * jax.random.normal(k, shape, f32)

    def gain(k, shape):
        return 1.0 + 0.02 * jax.random.normal(k, shape, f32)

    L, D = DEPTH, D_MODEL
    ki, kf = jax.random.split(ks[8])
    gate_bias = jnp.concatenate([
        nrm(ki, (L, M_HEADS), 0.1),
        jnp.linspace(3.0, 6.0, M_HEADS, dtype=f32)[None, :] + nrm(kf, (L, M_HEADS), 0.1)], axis=-1)
    return {
        'x': nrm(ks[0], (BATCH, SEQ, D), 1.0),
        'mem': nrm(ks[1], (BATCH, N_MEM, D), 1.0),
        'ffn1_pre_norm': gain(ks[2], (L, D)),
        'ffn1_w_gate_up': nrm(ks[3], (L, D, 2 * D_FF), D ** -0.5),
        'ffn1_w_down': nrm(ks[4], (L, D_FF, D), D_FF ** -0.5),
        'ffn1_post_norm': gain(ks[5], (L, D)),
        'mix_pre_norm': gain(ks[6], (L, D)),
        'mix_w_in': nrm(ks[7], (L, D, IN_WIDTH), D ** -0.5),
        'mlstm_gate_bias': gate_bias,
        'mlstm_conv': nrm(ks[9], (L, CONV_K, 2 * M_QK_WIDTH), CONV_K ** -0.5),
        'mlstm_head_norm': gain(ks[10], (L, M_WIDTH)),
        'hgrn_lower_bounds': nrm(ks[11], (L, H_F_WIDTH), 0.1),
        'hgrn_head_norm': gain(ks[12], (L, H_WIDTH)),
        'branch_w_mlstm': nrm(ks[13], (L, M_WIDTH, D), M_WIDTH ** -0.5),
        'branch_w_hgrn': nrm(ks[14], (L, H_WIDTH, D), H_WIDTH ** -0.5),
        'merge_w_gate': nrm(ks[15], (L, D, 2 * D), D ** -0.5),
        'merge_b_gate': nrm(ks[16], (L, 2 * D), 0.02),
        'mix_w_out': nrm(ks[17], (L, D, D), D ** -0.5),
        'mix_post_norm': gain(ks[18], (L, D)),
        'xattn_pre_norm': gain(ks[19], (L, D)),
        'xattn_mem_norm': gain(ks[20], (L, D)),
        'xattn_w_q': nrm(ks[21], (L, D, X_WIDTH), D ** -0.5),
        'xattn_w_kv': nrm(ks[22], (L, D, 2 * X_WIDTH), D ** -0.5),
        'xattn_w_o': nrm(ks[23], (L, X_WIDTH, D), X_WIDTH ** -0.5),
        'xattn_post_norm': gain(ks[24], (L, D)),
        'ffn2_pre_norm': gain(ks[25], (L, D)),
        'ffn2_w_gate_up': nrm(ks[26], (L, D, 2 * D_FF), D ** -0.5),
        'ffn2_w_down': nrm(ks[27], (L, D_FF, D), D_FF ** -0.5),
        'ffn2_post_norm': gain(ks[28], (L, D)),
    }


def reference(x, mem,
              ffn1_pre_norm, ffn1_w_gate_up, ffn1_w_down, ffn1_post_norm,
              mix_pre_norm, mix_w_in, mlstm_gate_bias, mlstm_conv, mlstm_head_norm,
              hgrn_lower_bounds, hgrn_head_norm, branch_w_mlstm, branch_w_hgrn,
              merge_w_gate, merge_b_gate, mix_w_out, mix_post_norm,
              xattn_pre_norm, xattn_mem_norm, xattn_w_q, xattn_w_kv, xattn_w_o, xattn_post_norm,
              ffn2_pre_norm, ffn2_w_gate_up, ffn2_w_down, ffn2_post_norm):
    h = x
    for l in range(DEPTH):
        h = h + swiglu_half(h, ffn1_pre_norm[l], ffn1_w_gate_up[l], ffn1_w_down[l], ffn1_post_norm[l])
        h = h + token_mixers(h, l, mix_pre_norm[l], mix_w_in[l], mlstm_gate_bias[l], mlstm_conv[l],
                             mlstm_head_norm[l], hgrn_lower_bounds, hgrn_head_norm[l],
                             branch_w_mlstm[l], branch_w_hgrn[l], merge_w_gate[l], merge_b_gate[l],
                             mix_w_out[l], mix_post_norm[l])
        h = h + mem_cross_attention(h, mem, xattn_pre_norm[l], xattn_mem_norm[l], xattn_w_q[l],
                                    xattn_w_kv[l], xattn_w_o[l], xattn_post_norm[l])
        h = h + swiglu_half(h, ffn2_pre_norm[l], ffn2_w_gate_up[l], ffn2_w_down[l], ffn2_post_norm[l])
    return h
```

```python
import functools
import math

import jax
import jax.numpy as jnp
from jax import lax
from jax.experimental import pallas as pl
from jax.experimental.pallas import tpu as pltpu

EPS = 1e-6
F32 = jnp.float32
BF16 = jnp.bfloat16

V7X_VMEM_BYTES = 64 * 1024 * 1024
VMEM_LIMIT = V7X_VMEM_BYTES - 8 * 1024 * 1024
LANES = 128
SUBLANES = 8

CONV_K = 4
M_HEADS = 8
M_DQK = 128
M_DV = 256
H_HEADS = 16
H_DK = 128
H_DV = 128
X_HEADS = 4
X_DH = 256
MLSTM_CHUNK = 256
HGRN_CHUNK = 64


def _cparams(sem):
    return pltpu.CompilerParams(dimension_semantics=sem, vmem_limit_bytes=VMEM_LIMIT)


def _dot(a, b):
    return jnp.dot(a, b, preferred_element_type=F32)


def _dot_nt(a, b):
    return lax.dot_general(a, b, (((1,), (1,)), ((), ())), preferred_element_type=F32)


def _dot_tn(a, b):
    return lax.dot_general(a, b, (((0,), (0,)), ((), ())), preferred_element_type=F32)


def _sigmoid(x):
    return 1.0 / (1.0 + jnp.exp(-x))


def _silu(x):
    return x * _sigmoid(x)


def _log_sigmoid(x):
    return jnp.minimum(x, 0.0) - jnp.log(1.0 + jnp.exp(-jnp.abs(x)))


def _rms(x):
    return x * lax.rsqrt(jnp.mean(x * x, axis=-1, keepdims=True) + EPS)


def _rmsnorm_kernel(x_ref, g_ref, o_ref):
    o_ref[...] = (_rms(x_ref[...]) * g_ref[...]).astype(o_ref.dtype)


def rmsnorm_bf16(x, g, tm=256):
    m, d = x.shape
    tm = min(tm, m)
    return pl.pallas_call(
        _rmsnorm_kernel,
        grid=(m // tm,),
        in_specs=[pl.BlockSpec((tm, d), lambda i: (i, 0)), pl.BlockSpec((1, d), lambda i: (0, 0))],
        out_specs=pl.BlockSpec((tm, d), lambda i: (i, 0)),
        out_shape=jax.ShapeDtypeStruct((m, d), BF16),
        compiler_params=_cparams(("parallel",)),
        name="rmsnorm_bf16",
    )(x, g.reshape(1, d))


def _mm_kernel(*refs, n_w, epilogue):
    a_ref = refs[0]
    w_refs = refs[1:1 + n_w]
    extra_refs = refs[1 + n_w:-1]
    o_ref = refs[-1]
    a = a_ref[...]
    accs = [_dot(a, w[...]) for w in w_refs]
    o_ref[...] = epilogue(accs, [e[...] for e in extra_refs]).astype(o_ref.dtype)


def matmul(a, w, n_out, *, tm, tn, w_col_offsets=(0,), epilogue=None, extras=(), out_dtype=F32, name="matmul"):
    m, k = a.shape
    tm = min(tm, m)
    tn = min(tn, n_out)
    assert m % tm == 0 and n_out % tn == 0
    if epilogue is None:
        epilogue = lambda accs, ex: accs[0]
    in_specs = [pl.BlockSpec((tm, k), lambda i, j: (i, 0))]
    args = [a]
    for off in w_col_offsets:
        assert off % tn == 0
        in_specs.append(pl.BlockSpec((k, tn), functools.partial(lambda i, j, o: (0, j + o), o=off // tn)))
        args.append(w)
    for arr, off in extras:
        assert off % tn == 0
        rows = arr.shape[0]
        if rows == 1:
            in_specs.append(pl.BlockSpec((1, tn), functools.partial(lambda i, j, o: (0, j + o), o=off // tn)))
        else:
            in_specs.append(pl.BlockSpec((tm, tn), functools.partial(lambda i, j, o: (i, j + o), o=off // tn)))
        args.append(arr)
    return pl.pallas_call(
        functools.partial(_mm_kernel, n_w=len(w_col_offsets), epilogue=epilogue),
        grid=(m // tm, n_out // tn),
        in_specs=in_specs,
        out_specs=pl.BlockSpec((tm, tn), lambda i, j: (i, j)),
        out_shape=jax.ShapeDtypeStruct((m, n_out), out_dtype),
        compiler_params=_cparams(("parallel", "arbitrary")),
        name=name,
    )(*args)


def _epi_swiglu(accs, ex):
    return _silu(accs[0]) * accs[1]


def _epi_sigmoid_bias(accs, ex):
    return _sigmoid(accs[0] + ex[0])


def _epi_merge(accs, ex):
    gm, gh, ym = ex
    return gm.astype(F32) * ym + gh.astype(F32) * accs[0]


def _combine_kernel(y_ref, h_ref, post_ref, pre_ref, hn_ref, u_ref, *, scale):
    hn = h_ref[...] + scale * (_rms(y_ref[...]) * post_ref[...])
    hn_ref[...] = hn
    u_ref[...] = (_rms(hn) * pre_ref[...]).astype(u_ref.dtype)


def _combine_last_kernel(y_ref, h_ref, post_ref, hn_ref, *, scale):
    hn_ref[...] = h_ref[...] + scale * (_rms(y_ref[...]) * post_ref[...])


def combine(y, h, post, scale, next_pre=None, tm=256):
    m, d = y.shape
    tm = min(tm, m)
    row = pl.BlockSpec((tm, d), lambda i: (i, 0))
    vec = pl.BlockSpec((1, d), lambda i: (0, 0))
    if next_pre is None:
        return pl.pallas_call(
            functools.partial(_combine_last_kernel, scale=scale),
            grid=(m // tm,),
            in_specs=[row, row, vec],
            out_specs=row,
            out_shape=jax.ShapeDtypeStruct((m, d), F32),
            compiler_params=_cparams(("parallel",)),
            name="combine_last",
        )(y, h, post.reshape(1, d)), None
    return pl.pallas_call(
        functools.partial(_combine_kernel, scale=scale),
        grid=(m // tm,),
        in_specs=[row, row, vec, vec],
        out_specs=[row, row],
        out_shape=[jax.ShapeDtypeStruct((m, d), F32), jax.ShapeDtypeStruct((m, d), BF16)],
        compiler_params=_cparams(("parallel",)),
        name="combine",
    )(y, h, post.reshape(1, d), next_pre.reshape(1, d))


def _xattn_kernel(q_ref, k_ref, v_ref, o_ref, *, heads, dh):
    scale = dh ** -0.5
    for hd in range(heads):
        sl = slice(hd * dh, (hd + 1) * dh)
        s = _dot_nt(q_ref[:, sl], k_ref[:, sl]) * scale
        s = s - jnp.max(s, axis=-1, keepdims=True)
        p = jnp.exp(s)
        p = p / jnp.sum(p, axis=-1, keepdims=True)
        o_ref[:, sl] = _dot(p.astype(BF16), v_ref[:, sl]).astype(o_ref.dtype)


def xattn_core(q, kv, batch, heads, dh, tm=512):
    rows, width = q.shape
    s_len = rows // batch
    n_mem = kv.shape[0] // batch
    tm = min(tm, s_len)
    nt = s_len // tm
    return pl.pallas_call(
        functools.partial(_xattn_kernel, heads=heads, dh=dh),
        grid=(batch, nt),
        in_specs=[
            pl.BlockSpec((tm, width), lambda b, i: (b * nt + i, 0)),
            pl.BlockSpec((n_mem, width), lambda b, i: (b, 0)),
            pl.BlockSpec((n_mem, width), lambda b, i: (b, 1)),
        ],
        out_specs=pl.BlockSpec((tm, width), lambda b, i: (b * nt + i, 0)),
        out_shape=jax.ShapeDtypeStruct((rows, width), BF16),
        compiler_params=_cparams(("parallel", "arbitrary")),
        name="xattn_core",
    )(q, kv, kv)


def _causal_conv_silu(x, tail, w):
    rows8 = lax.broadcasted_iota(jnp.int32, (SUBLANES, 1), 0)
    y = x * w[CONV_K - 1:CONV_K]
    for d in range(1, CONV_K):
        xs = pltpu.roll(x, d, 0)
        top = jnp.where(rows8 < d, pltpu.roll(tail, d, 0), xs[0:SUBLANES])
        xs = jnp.concatenate([top, xs[SUBLANES:]], axis=0)
        y = y + xs * w[CONV_K - 1 - d:CONV_K - d]
    return _silu(y)


def _mlstm_kernel(q_ref, k_ref, v_ref, og_ref, gc_ref, gr_ref, bc_ref, br_ref, conv_ref, norm_ref, out_ref,
                  c_scr, n_scr, m_scr, tq_scr, tk_scr, *, heads, dqk, dv):
    L = q_ref.shape[0]
    qk_w = heads * dqk

    @pl.when(pl.program_id(1) == 0)
    def _():
        c_scr[...] = jnp.zeros_like(c_scr)
        n_scr[...] = jnp.zeros_like(n_scr)
        m_scr[...] = jnp.zeros_like(m_scr)
        tq_scr[...] = jnp.zeros_like(tq_scr)
        tk_scr[...] = jnp.zeros_like(tk_scr)

    q_raw = q_ref[...]
    k_raw = k_ref[...]
    q_all = _causal_conv_silu(q_raw, tq_scr[...], conv_ref[:, 0:qk_w]) * (dqk ** -0.5)
    k_all = _causal_conv_silu(k_raw, tk_scr[...], conv_ref[:, qk_w:2 * qk_w])
    tq_scr[...] = q_raw[L - SUBLANES:L]
    tk_scr[...] = k_raw[L - SUBLANES:L]

    gc = gc_ref[...] + bc_ref[...]
    gr = gr_ref[0] + br_ref[...]
    t_idx = lax.broadcasted_iota(jnp.int32, (L, L), 0)
    s_idx = lax.broadcasted_iota(jnp.int32, (L, L), 1)
    causal = s_idx <= t_idx
    tri = causal.astype(F32)
    bc_all = jnp.dot(tri, _log_sigmoid(gc), precision=lax.Precision.HIGHEST, preferred_element_type=F32)
    br_all = lax.dot_general(_log_sigmoid(gr), tri, (((1,), (1,)), ((), ())),
                             precision=lax.Precision.HIGHEST, preferred_element_type=F32)

    for hd in range(heads):
        qs = slice(hd * dqk, (hd + 1) * dqk)
        vs = slice(hd * dv, (hd + 1) * dv)
        q = q_all[:, qs]
        k = k_all[:, qs]
        qb = q.astype(BF16)
        kb = k.astype(BF16)
        vb = v_ref[:, vs].astype(BF16)
        b_c = bc_all[:, heads + hd:heads + hd + 1]
        li_c = gc[:, hd:hd + 1]
        b_r = br_all[heads + hd:heads + hd + 1, :]
        li_r = gr[hd:hd + 1, :]
        m_prev = m_scr[hd:hd + 1, 0:1]
        n_prev = n_scr[hd:hd + 1, :]

        d = jnp.where(causal, b_c - b_r + li_r, -jnp.inf)
        m_inter = b_c + m_prev
        m_t = jnp.maximum(m_inter, jnp.max(d, axis=1, keepdims=True))
        w_intra = jnp.exp(d - m_t)
        w_inter = jnp.exp(m_inter - m_t)
        s = _dot_nt(qb, kb) * w_intra
        num = _dot(s.astype(BF16), vb) + w_inter * _dot(qb, c_scr[hd].astype(BF16))
        den = jnp.sum(s, axis=1, keepdims=True) + w_inter * jnp.sum(q * n_prev, axis=1, keepdims=True)
        hh = num / jnp.maximum(jnp.abs(den), jnp.exp(-m_t))
        out = _rms(hh) * norm_ref[:, vs] * _sigmoid(og_ref[:, vs])
        out_ref[:, vs] = out.astype(out_ref.dtype)

        g = b_c[L - 1:L, :]
        a_c = g - b_c + li_c
        m_new = jnp.maximum(g + m_prev, jnp.max(a_c, axis=0, keepdims=True))
        decay = jnp.exp(g + m_prev - m_new)
        kw = k * jnp.exp(a_c - m_new)
        c_scr[hd] = decay * c_scr[hd] + _dot_tn(kw.astype(BF16), vb)
        n_scr[hd:hd + 1, :] = decay * n_prev + jnp.sum(kw, axis=0, keepdims=True)
        m_scr[hd:hd + 1, :] = jnp.broadcast_to(m_new, (1, LANES))


def mlstm_scan(proj, gates_col, gates_row, gate_bias, conv_w, head_norm, batch, *, chunk, heads=M_HEADS,
               dqk=M_DQK, dv=M_DV):
    rows = proj.shape[0]
    s_len = rows // batch
    L = min(chunk, s_len)
    nc = s_len // L
    qk_w = heads * dqk
    v_w = heads * dv
    assert qk_w % LANES == 0 and v_w % qk_w == 0
    bias_col = jnp.zeros((1, LANES), F32).at[0, :2 * heads].set(gate_bias)
    bias_row = gate_bias.reshape(2 * heads, 1)
    kern = functools.partial(_mlstm_kernel, heads=heads, dqk=dqk, dv=dv)
    return pl.pallas_call(
        kern,
        grid=(batch, nc),
        in_specs=[
            pl.BlockSpec((L, qk_w), lambda b, c: (b * nc + c, 0)),
            pl.BlockSpec((L, qk_w), lambda b, c: (b * nc + c, 1)),
            pl.BlockSpec((L, v_w), lambda b, c: (b * nc + c, 2 * qk_w // v_w)),
            pl.BlockSpec((L, v_w), lambda b, c: (b * nc + c, 2 * qk_w // v_w + 1)),
            pl.BlockSpec((L, LANES), lambda b, c: (b * nc + c, 0)),
            pl.BlockSpec((1, 2 * heads, L), lambda b, c: (b, 0, c)),
            pl.BlockSpec((1, LANES), lambda b, c: (0, 0)),
            pl.BlockSpec((2 * heads, 1), lambda b, c: (0, 0)),
            pl.BlockSpec((CONV_K, 2 * qk_w), lambda b, c: (0, 0)),
            pl.BlockSpec((1, v_w), lambda b, c: (0, 0)),
        ],
        out_specs=pl.BlockSpec((L, v_w), lambda b, c: (b * nc + c, 0)),
        out_shape=jax.ShapeDtypeStruct((rows, v_w), BF16),
        scratch_shapes=[
            pltpu.VMEM((heads, dqk, dv), F32),
            pltpu.VMEM((heads, dqk), F32),
            pltpu.VMEM((heads, LANES), F32),
            pltpu.VMEM((SUBLANES, qk_w), F32),
            pltpu.VMEM((SUBLANES, qk_w), F32),
        ],
        compiler_params=_cparams(("parallel", "arbitrary")),
        name="mlstm_scan",
    )(proj, proj, proj, proj, gates_col, gates_row, bias_col, bias_row, conv_w, head_norm.reshape(1, v_w))


def _hgrn_kernel(q_ref, f_ref, i_ref, g_ref, lb_ref, norm_ref, out_ref, st_scr, *, heads, dk, dv, layer):
    L = q_ref.shape[0]

    @pl.when(pl.program_id(1) == 0)
    def _():
        st_scr[...] = jnp.zeros_like(st_scr)

    lbp = lb_ref[...]
    e = jnp.exp(lbp - jnp.max(lbp, axis=0, keepdims=True))
    p = e / jnp.sum(e, axis=0, keepdims=True)
    lb = jnp.zeros((1, lbp.shape[1]), F32)
    for j in range(1, layer + 1):
        lb = lb + p[j:j + 1]

    f = lb + (1.0 - lb) * _sigmoid(f_ref[...])
    lg = jnp.log(f)
    k_all = 1.0 - f
    t_idx = lax.broadcasted_iota(jnp.int32, (L, L), 0)
    s_idx = lax.broadcasted_iota(jnp.int32, (L, L), 1)
    tri = (s_idx <= t_idx).astype(F32)
    b_all = jnp.dot(tri, lg, precision=lax.Precision.HIGHEST, preferred_element_type=F32)
    level = jnp.where(t_idx > s_idx, 31 - lax.clz(t_idx ^ s_idx), jnp.where(t_idx == s_idx, -1, -2))
    row = lax.broadcasted_iota(jnp.int32, (L, 1), 0)
    n_levels = int(math.log2(L))

    for hd in range(heads):
        ks = slice(hd * dk, (hd + 1) * dk)
        vs = slice(hd * dv, (hd + 1) * dv)
        q = q_ref[:, ks]
        k = k_all[:, ks]
        b = b_all[:, ks]
        ib = i_ref[:, vs].astype(BF16)

        a = jnp.where(level == -1, _dot_nt(q.astype(BF16), k.astype(BF16)), 0.0)
        fw = b
        for lw in range(n_levels):
            w = 1 << lw
            gw = pltpu.roll(fw, L - w, 0)
            qp = q * jnp.exp(b - fw)
            kp = k * jnp.exp(jnp.minimum(gw - b, 0.0))
            a = a + jnp.where(level == lw, _dot_nt(qp.astype(BF16), kp.astype(BF16)), 0.0)
            fw = jnp.where(((row >> lw) & 1) == 1, pltpu.roll(fw, w, 0), fw)

        st = st_scr[hd]
        o = _dot(a.astype(BF16), ib) + _dot_nt((q * jnp.exp(b)).astype(BF16), st.astype(BF16))
        g = b[L - 1:L, :]
        ke = k * jnp.exp(g - b)
        st_scr[hd] = st * jnp.exp(g) + _dot_tn(ib, ke.astype(BF16))
        out = _rms(o) * norm_ref[:, vs] * _silu(g_ref[:, vs])
        out_ref[:, vs] = out.astype(out_ref.dtype)


def hgrn_scan(proj, col_block0, lower_bounds, head_norm, batch, layer, *, chunk, heads=H_HEADS, dk=H_DK, dv=H_DV):
    rows = proj.shape[0]
    s_len = rows // batch
    L = min(chunk, s_len)
    nc = s_len // L
    assert L & (L - 1) == 0 and dk == dv
    w = heads * dk
    depth = lower_bounds.shape[0]
    kern = functools.partial(_hgrn_kernel, heads=heads, dk=dk, dv=dv, layer=layer)
    blk = lambda o: pl.BlockSpec((L, w), functools.partial(lambda b, c, o: (b * nc + c, o), o=col_block0 + o))
    return pl.pallas_call(
        kern,
        grid=(batch, nc),
        in_specs=[blk(0), blk(1), blk(2), blk(3),
                  pl.BlockSpec((depth, w), lambda b, c: (0, 0)),
                  pl.BlockSpec((1, w), lambda b, c: (0, 0))],
        out_specs=pl.BlockSpec((L, w), lambda b, c: (b * nc + c, 0)),
        out_shape=jax.ShapeDtypeStruct((rows, w), BF16),
        scratch_shapes=[pltpu.VMEM((heads, dv, dk), F32)],
        compiler_params=_cparams(("parallel", "arbitrary")),
        name="hgrn_scan",
    )(proj, proj, proj, proj, lower_bounds, head_norm.reshape(1, w))


def _ffn(u, w_gu, w_down, d_ff):
    act = matmul(u, w_gu, d_ff, tm=1024, tn=256, w_col_offsets=(0, d_ff), epilogue=_epi_swiglu,
                 out_dtype=BF16, name="ffn_gate_up")
    return matmul(act, w_down, w_down.shape[1], tm=512, tn=256, name="ffn_down")


def _token_mixers(u, layer, batch, w_in_main, w_in_gates, gate_bias, conv_w, m_norm, lower_bounds, h_norm,
                  w_bm, w_bh, w_gate, b_gate, w_out):
    rows, d = u.shape
    s_len = rows // batch
    proj = matmul(u, w_in_main, w_in_main.shape[1], tm=1024, tn=512, name="mix_in_proj")
    gates_col = matmul(u, w_in_gates, LANES, tm=1024, tn=LANES, name="mix_gate_proj")
    gates_row = gates_col[:, :2 * M_HEADS].reshape(batch, s_len, 2 * M_HEADS).transpose(0, 2, 1)
    ym = mlstm_scan(proj, gates_col, gates_row, gate_bias, conv_w, m_norm, batch, chunk=MLSTM_CHUNK)
    hgrn_w = H_HEADS * H_DK
    mlstm_cols = 2 * M_HEADS * M_DQK + 2 * M_HEADS * M_DV
    yh = hgrn_scan(proj, mlstm_cols // hgrn_w, lower_bounds, h_norm, batch, layer, chunk=HGRN_CHUNK)
    gate = matmul(u, w_gate, 2 * d, tm=1024, tn=512, epilogue=_epi_sigmoid_bias,
                  extras=((b_gate.reshape(1, 2 * d), 0),), out_dtype=BF16, name="merge_gate")
    y_m = matmul(ym, w_bm, d, tm=1024, tn=512, name="branch_mlstm")
    z = matmul(yh, w_bh, d, tm=1024, tn=512, epilogue=_epi_merge,
               extras=((gate, 0), (gate, d), (y_m, 0)), out_dtype=BF16, name="branch_hgrn_merge")
    return matmul(z, w_out, d, tm=1024, tn=512, name="mix_out")


def _xattn(u, kv, batch, w_q, w_o):
    q = matmul(u, w_q, w_q.shape[1], tm=1024, tn=512, out_dtype=BF16, name="xattn_q")
    o = xattn_core(q, kv, batch, X_HEADS, X_DH)
    return matmul(o, w_o, w_o.shape[1], tm=1024, tn=512, name="xattn_o")


def kernel(x, mem, ffn1_pre_norm, ffn1_w_gate_up, ffn1_w_down, ffn1_post_norm, mix_pre_norm, mix_w_in, mlstm_gate_bias, mlstm_conv, mlstm_head_norm, hgrn_lower_bounds, hgrn_head_norm, branch_w_mlstm, branch_w_hgrn, merge_w_gate, merge_b_gate, mix_w_out, mix_post_norm, xattn_pre_norm, xattn_mem_norm, xattn_w_q, xattn_w_kv, xattn_w_o, xattn_post_norm, ffn2_pre_norm, ffn2_w_gate_up, ffn2_w_down, ffn2_post_norm):
    batch, s_len, d = x.shape
    n_mem = mem.shape[1]
    depth = ffn1_pre_norm.shape[0]
    d_ff = ffn1_w_down.shape[1]
    gate_lo = 2 * M_HEADS * M_DQK + 2 * M_HEADS * M_DV
    gate_hi = gate_lo + 2 * M_HEADS

    h = x.reshape(batch * s_len, d)
    mem2 = mem.reshape(batch * n_mem, d)
    u = rmsnorm_bf16(h, ffn1_pre_norm[0])
    for l in range(depth):
        bf = lambda w: w[l].astype(BF16)
        y = _ffn(u, bf(ffn1_w_gate_up), bf(ffn1_w_down), d_ff)
        h, u = combine(y, h, ffn1_post_norm[l], 0.5, mix_pre_norm[l])

        w_in = mix_w_in[l]
        w_in_main = jnp.concatenate([w_in[:, :gate_lo], w_in[:, gate_hi:]], axis=1).astype(BF16)
        w_in_gates = jnp.pad(w_in[:, gate_lo:gate_hi], ((0, 0), (0, LANES - 2 * M_HEADS))).astype(BF16)
        y = _token_mixers(u, l, batch, w_in_main, w_in_gates, mlstm_gate_bias[l], mlstm_conv[l],
                          mlstm_head_norm[l], hgrn_lower_bounds, hgrn_head_norm[l], bf(branch_w_mlstm),
                          bf(branch_w_hgrn), bf(merge_w_gate), merge_b_gate[l], bf(mix_w_out))
        h, u = combine(y, h, mix_post_norm[l], 1.0, xattn_pre_norm[l])

        memn = rmsnorm_bf16(mem2, xattn_mem_norm[l])
        kv = matmul(memn, bf(xattn_w_kv), xattn_w_kv.shape[2], tm=512, tn=512, out_dtype=BF16, name="xattn_kv")
        y = _xattn(u, kv, batch, bf(xattn_w_q), bf(xattn_w_o))
        h, u = combine(y, h, xattn_post_norm[l], 1.0, ffn2_pre_norm[l])

        y = _ffn(u, bf(ffn2_w_gate_up), bf(ffn2_w_down), d_ff)
        nxt = ffn1_pre_norm[l + 1] if l + 1 < depth else None
        h, u = combine(y, h, ffn2_post_norm[l], 0.5, nxt)
    return h.reshape(batch, s_len, d)
```

```python
import functools
import math

import jax
import jax.numpy as jnp
from jax import lax
from jax.experimental import pallas as pl
from jax.experimental.pallas import tpu as pltpu

EPS = 1e-6
F32 = jnp.float32
BF16 = jnp.bfloat16

V7X_VMEM_BYTES = 64 * 1024 * 1024
VMEM_LIMIT = V7X_VMEM_BYTES - 8 * 1024 * 1024
LANES = 128
SUBLANES = 8

CONV_K = 4
M_HEADS = 8
M_DQK = 128
M_DV = 256
H_HEADS = 16
H_DK = 128
H_DV = 128
X_HEADS = 4
X_DH = 256
MLSTM_CHUNK = 256
HGRN_CHUNK = 64
CAST_BLOCK_ELEMS = 2 * 1024 * 1024
MM_TM = 1024
MM_TN = 512
FFN_UP_TN = 256
FFN_DOWN_TM = 512


def _cparams(sem):
    return pltpu.CompilerParams(dimension_semantics=sem, vmem_limit_bytes=VMEM_LIMIT)


def _dot(a, b):
    return jnp.dot(a, b, preferred_element_type=F32)


def _dot_nt(a, b):
    return lax.dot_general(a, b, (((1,), (1,)), ((), ())), preferred_element_type=F32)


def _dot_tn(a, b):
    return lax.dot_general(a, b, (((0,), (0,)), ((), ())), preferred_element_type=F32)


def _sigmoid(x):
    return 1.0 / (1.0 + jnp.exp(-x))


def _silu(x):
    return x * _sigmoid(x)


def _log_sigmoid(x):
    return jnp.minimum(x, 0.0) - jnp.log(1.0 + jnp.exp(-jnp.abs(x)))


def _rms(x):
    return x * lax.rsqrt(jnp.mean(x * x, axis=-1, keepdims=True) + EPS)


def _rmsnorm_kernel(x_ref, g_ref, o_ref):
    o_ref[...] = (_rms(x_ref[...]) * g_ref[...]).astype(o_ref.dtype)


def rmsnorm_bf16(x, g, tm=256):
    m, d = x.shape
    tm = min(tm, m)
    return pl.pallas_call(
        _rmsnorm_kernel,
        grid=(m // tm,),
        in_specs=[pl.BlockSpec((tm, d), lambda i: (i, 0)), pl.BlockSpec((1, d), lambda i: (0, 0))],
        out_specs=pl.BlockSpec((tm, d), lambda i: (i, 0)),
        out_shape=jax.ShapeDtypeStruct((m, d), BF16),
        compiler_params=_cparams(("parallel",)),
        name="rmsnorm_bf16",
    )(x, g.reshape(1, d))


def _cast_kernel(x_ref, o_ref):
    o_ref[...] = x_ref[...].astype(o_ref.dtype)


def _largest_divisor(n, cap, multiple):
    best = multiple
    for c in range(multiple, min(n, cap) + 1, multiple):
        if n % c == 0:
            best = c
    return best


def cast_tiled_bf16(w3, layer, tn):
    _, k, n = w3.shape
    tn = min(tn, n)
    assert n % tn == 0
    tk = _largest_divisor(k, max(SUBLANES, CAST_BLOCK_ELEMS // tn), 2 * SUBLANES)
    return pl.pallas_call(
        _cast_kernel,
        grid=(n // tn, k // tk),
        in_specs=[pl.BlockSpec((None, tk, tn), lambda j, i: (layer, i, j))],
        out_specs=pl.BlockSpec((None, tk, tn), lambda j, i: (j, i, 0)),
        out_shape=jax.ShapeDtypeStruct((n // tn, k, tn), BF16),
        compiler_params=_cparams(("parallel", "parallel")),
        name="cast_tiled_bf16",
    )(w3)


def _mm_kernel(*refs, n_w, epilogue):
    a_ref = refs[0]
    w_refs = refs[1:1 + n_w]
    extra_refs = refs[1 + n_w:-1]
    o_ref = refs[-1]
    a = a_ref[...]
    accs = [_dot(a, w[...]) for w in w_refs]
    o_ref[...] = epilogue(accs, [e[...] for e in extra_refs]).astype(o_ref.dtype)


def matmul(a, wt, n_out, *, tm, w_col_offsets=(0,), epilogue=None, extras=(), out_dtype=F32, name="matmul"):
    m, k = a.shape
    tn = wt.shape[2]
    tm = min(tm, m)
    assert m % tm == 0 and n_out % tn == 0 and wt.shape[1] == k
    if epilogue is None:
        epilogue = lambda accs, ex: accs[0]
    in_specs = [pl.BlockSpec((tm, k), lambda i, j: (i, 0))]
    args = [a]
    for off in w_col_offsets:
        assert off % tn == 0
        in_specs.append(pl.BlockSpec((None, k, tn), functools.partial(lambda i, j, o: (j + o, 0, 0), o=off // tn)))
        args.append(wt)
    for arr, off in extras:
        assert off % tn == 0
        rows = arr.shape[0]
        if rows == 1:
            in_specs.append(pl.BlockSpec((1, tn), functools.partial(lambda i, j, o: (0, j + o), o=off // tn)))
        else:
            in_specs.append(pl.BlockSpec((tm, tn), functools.partial(lambda i, j, o: (i, j + o), o=off // tn)))
        args.append(arr)
    return pl.pallas_call(
        functools.partial(_mm_kernel, n_w=len(w_col_offsets), epilogue=epilogue),
        grid=(m // tm, n_out // tn),
        in_specs=in_specs,
        out_specs=pl.BlockSpec((tm, tn), lambda i, j: (i, j)),
        out_shape=jax.ShapeDtypeStruct((m, n_out), out_dtype),
        compiler_params=_cparams(("parallel", "arbitrary")),
        name=name,
    )(*args)


def _epi_swiglu(accs, ex):
    return _silu(accs[0]) * accs[1]


def _epi_sigmoid_bias(accs, ex):
    return _sigmoid(accs[0] + ex[0])


def _epi_merge(accs, ex):
    gm, gh, ym = ex
    return gm.astype(F32) * ym + gh.astype(F32) * accs[0]


def _combine_kernel(y_ref, h_ref, post_ref, pre_ref, hn_ref, u_ref, *, scale):
    hn = h_ref[...] + scale * (_rms(y_ref[...]) * post_ref[...])
    hn_ref[...] = hn
    u_ref[...] = (_rms(hn) * pre_ref[...]).astype(u_ref.dtype)


def _combine_last_kernel(y_ref, h_ref, post_ref, hn_ref, *, scale):
    hn_ref[...] = h_ref[...] + scale * (_rms(y_ref[...]) * post_ref[...])


def combine(y, h, post, scale, next_pre=None, tm=256):
    m, d = y.shape
    tm = min(tm, m)
    row = pl.BlockSpec((tm, d), lambda i: (i, 0))
    vec = pl.BlockSpec((1, d), lambda i: (0, 0))
    if next_pre is None:
        return pl.pallas_call(
            functools.partial(_combine_last_kernel, scale=scale),
            grid=(m // tm,),
            in_specs=[row, row, vec],
            out_specs=row,
            out_shape=jax.ShapeDtypeStruct((m, d), F32),
            compiler_params=_cparams(("parallel",)),
            name="combine_last",
        )(y, h, post.reshape(1, d)), None
    return pl.pallas_call(
        functools.partial(_combine_kernel, scale=scale),
        grid=(m // tm,),
        in_specs=[row, row, vec, vec],
        out_specs=[row, row],
        out_shape=[jax.ShapeDtypeStruct((m, d), F32), jax.ShapeDtypeStruct((m, d), BF16)],
        compiler_params=_cparams(("parallel",)),
        name="combine",
    )(y, h, post.reshape(1, d), next_pre.reshape(1, d))


def _xattn_kernel(q_ref, k_ref, v_ref, o_ref, *, heads, dh):
    scale = dh ** -0.5
    for hd in range(heads):
        sl = slice(hd * dh, (hd + 1) * dh)
        s = _dot_nt(q_ref[:, sl], k_ref[:, sl]) * scale
        s = s - jnp.max(s, axis=-1, keepdims=True)
        p = jnp.exp(s)
        p = p / jnp.sum(p, axis=-1, keepdims=True)
        o_ref[:, sl] = _dot(p.astype(BF16), v_ref[:, sl]).astype(o_ref.dtype)


def xattn_core(q, kv, batch, heads, dh, tm=512):
    rows, width = q.shape
    s_len = rows // batch
    n_mem = kv.shape[0] // batch
    tm = min(tm, s_len)
    nt = s_len // tm
    return pl.pallas_call(
        functools.partial(_xattn_kernel, heads=heads, dh=dh),
        grid=(batch, nt),
        in_specs=[
            pl.BlockSpec((tm, width), lambda b, i: (b * nt + i, 0)),
            pl.BlockSpec((n_mem, width), lambda b, i: (b, 0)),
            pl.BlockSpec((n_mem, width), lambda b, i: (b, 1)),
        ],
        out_specs=pl.BlockSpec((tm, width), lambda b, i: (b * nt + i, 0)),
        out_shape=jax.ShapeDtypeStruct((rows, width), BF16),
        compiler_params=_cparams(("parallel", "arbitrary")),
        name="xattn_core",
    )(q, kv, kv)


def _causal_conv_silu(x, tail, w):
    rows8 = lax.broadcasted_iota(jnp.int32, (SUBLANES, 1), 0)
    y = x * w[CONV_K - 1:CONV_K]
    for d in range(1, CONV_K):
        xs = pltpu.roll(x, d, 0)
        top = jnp.where(rows8 < d, pltpu.roll(tail, d, 0), xs[0:SUBLANES])
        xs = jnp.concatenate([top, xs[SUBLANES:]], axis=0)
        y = y + xs * w[CONV_K - 1 - d:CONV_K - d]
    return _silu(y)


def _mlstm_kernel(q_ref, k_ref, v_ref, og_ref, gc_ref, gr_ref, bc_ref, br_ref, conv_ref, norm_ref, out_ref,
                  c_scr, n_scr, m_scr, tq_scr, tk_scr, *, heads, dqk, dv):
    L = q_ref.shape[0]
    qk_w = heads * dqk

    @pl.when(pl.program_id(1) == 0)
    def _():
        c_scr[...] = jnp.zeros_like(c_scr)
        n_scr[...] = jnp.zeros_like(n_scr)
        m_scr[...] = jnp.zeros_like(m_scr)
        tq_scr[...] = jnp.zeros_like(tq_scr)
        tk_scr[...] = jnp.zeros_like(tk_scr)

    q_raw = q_ref[...]
    k_raw = k_ref[...]
    q_all = _causal_conv_silu(q_raw, tq_scr[...], conv_ref[:, 0:qk_w]) * (dqk ** -0.5)
    k_all = _causal_conv_silu(k_raw, tk_scr[...], conv_ref[:, qk_w:2 * qk_w])
    tq_scr[...] = q_raw[L - SUBLANES:L]
    tk_scr[...] = k_raw[L - SUBLANES:L]

    gc = gc_ref[...] + bc_ref[...]
    gr = gr_ref[0] + br_ref[...]
    t_idx = lax.broadcasted_iota(jnp.int32, (L, L), 0)
    s_idx = lax.broadcasted_iota(jnp.int32, (L, L), 1)
    causal = s_idx <= t_idx
    tri = causal.astype(F32)
    bc_all = jnp.dot(tri, _log_sigmoid(gc), precision=lax.Precision.HIGHEST, preferred_element_type=F32)
    br_all = lax.dot_general(_log_sigmoid(gr), tri, (((1,), (1,)), ((), ())),
                             precision=lax.Precision.HIGHEST, preferred_element_type=F32)

    for hd in range(heads):
        qs = slice(hd * dqk, (hd + 1) * dqk)
        vs = slice(hd * dv, (hd + 1) * dv)
        q = q_all[:, qs]
        k = k_all[:, qs]
        qb = q.astype(BF16)
        kb = k.astype(BF16)
        vb = v_ref[:, vs].astype(BF16)
        b_c = bc_all[:, heads + hd:heads + hd + 1]
        li_c = gc[:, hd:hd + 1]
        b_r = br_all[heads + hd:heads + hd + 1, :]
        li_r = gr[hd:hd + 1, :]
        m_prev = m_scr[hd:hd + 1, 0:1]
        n_prev = n_scr[hd:hd + 1, :]

        d = jnp.where(causal, b_c - b_r + li_r, -jnp.inf)
        m_inter = b_c + m_prev
        m_t = jnp.maximum(m_inter, jnp.max(d, axis=1, keepdims=True))
        w_intra = jnp.exp(d - m_t)
        w_inter = jnp.exp(m_inter - m_t)
        s = _dot_nt(qb, kb) * w_intra
        num = _dot(s.astype(BF16), vb) + w_inter * _dot(qb, c_scr[hd].astype(BF16))
        den = jnp.sum(s, axis=1, keepdims=True) + w_inter * jnp.sum(q * n_prev, axis=1, keepdims=True)
        hh = num / jnp.maximum(jnp.abs(den), jnp.exp(-m_t))
        out = _rms(hh) * norm_ref[:, vs] * _sigmoid(og_ref[:, vs])
        out_ref[:, vs] = out.astype(out_ref.dtype)

        g = b_c[L - 1:L, :]
        a_c = g - b_c + li_c
        m_new = jnp.maximum(g + m_prev, jnp.max(a_c, axis=0, keepdims=True))
        decay = jnp.exp(g + m_prev - m_new)
        kw = k * jnp.exp(a_c - m_new)
        c_scr[hd] = decay * c_scr[hd] + _dot_tn(kw.astype(BF16), vb)
        n_scr[hd:hd + 1, :] = decay * n_prev + jnp.sum(kw, axis=0, keepdims=True)
        m_scr[hd:hd + 1, :] = jnp.broadcast_to(m_new, (1, LANES))


def mlstm_scan(proj, gates_col, gates_row, gate_bias, conv_w, head_norm, batch, *, chunk, heads=M_HEADS,
               dqk=M_DQK, dv=M_DV):
    rows = proj.shape[0]
    s_len = rows // batch
    L = min(chunk, s_len)
    nc = s_len // L
    qk_w = heads * dqk
    v_w = heads * dv
    assert qk_w % LANES == 0 and v_w % qk_w == 0
    bias_col = jnp.zeros((1, LANES), F32).at[0, :2 * heads].set(gate_bias)
    bias_row = gate_bias.reshape(2 * heads, 1)
    kern = functools.partial(_mlstm_kernel, heads=heads, dqk=dqk, dv=dv)
    return pl.pallas_call(
        kern,
        grid=(batch, nc),
        in_specs=[
            pl.BlockSpec((L, qk_w), lambda b, c: (b * nc + c, 0)),
            pl.BlockSpec((L, qk_w), lambda b, c: (b * nc + c, 1)),
            pl.BlockSpec((L, v_w), lambda b, c: (b * nc + c, 2 * qk_w // v_w)),
            pl.BlockSpec((L, v_w), lambda b, c: (b * nc + c, 2 * qk_w // v_w + 1)),
            pl.BlockSpec((L, LANES), lambda b, c: (b * nc + c, 0)),
            pl.BlockSpec((1, 2 * heads, L), lambda b, c: (b, 0, c)),
            pl.BlockSpec((1, LANES), lambda b, c: (0, 0)),
            pl.BlockSpec((2 * heads, 1), lambda b, c: (0, 0)),
            pl.BlockSpec((CONV_K, 2 * qk_w), lambda b, c: (0, 0)),
            pl.BlockSpec((1, v_w), lambda b, c: (0, 0)),
        ],
        out_specs=pl.BlockSpec((L, v_w), lambda b, c: (b * nc + c, 0)),
        out_shape=jax.ShapeDtypeStruct((rows, v_w), BF16),
        scratch_shapes=[
            pltpu.VMEM((heads, dqk, dv), F32),
            pltpu.VMEM((heads, dqk), F32),
            pltpu.VMEM((heads, LANES), F32),
            pltpu.VMEM((SUBLANES, qk_w), F32),
            pltpu.VMEM((SUBLANES, qk_w), F32),
        ],
        compiler_params=_cparams(("parallel", "arbitrary")),
        name="mlstm_scan",
    )(proj, proj, proj, proj, gates_col, gates_row, bias_col, bias_row, conv_w, head_norm.reshape(1, v_w))


def _hgrn_kernel(q_ref, f_ref, i_ref, g_ref, lb_ref, norm_ref, out_ref, st_scr, *, heads, dk, dv, layer):
    L = q_ref.shape[0]

    @pl.when(pl.program_id(1) == 0)
    def _():
        st_scr[...] = jnp.zeros_like(st_scr)

    lbp = lb_ref[...]
    e = jnp.exp(lbp - jnp.max(lbp, axis=0, keepdims=True))
    p = e / jnp.sum(e, axis=0, keepdims=True)
    lb = jnp.zeros((1, lbp.shape[1]), F32)
    for j in range(1, layer + 1):
        lb = lb + p[j:j + 1]

    f = lb + (1.0 - lb) * _sigmoid(f_ref[...])
    lg = jnp.log(f)
    k_all = 1.0 - f
    t_idx = lax.broadcasted_iota(jnp.int32, (L, L), 0)
    s_idx = lax.broadcasted_iota(jnp.int32, (L, L), 1)
    tri = (s_idx <= t_idx).astype(F32)
    b_all = jnp.dot(tri, lg, precision=lax.Precision.HIGHEST, preferred_element_type=F32)
    level = jnp.where(t_idx > s_idx, 31 - lax.clz(t_idx ^ s_idx), jnp.where(t_idx == s_idx, -1, -2))
    row = lax.broadcasted_iota(jnp.int32, (L, 1), 0)
    n_levels = int(math.log2(L))

    for hd in range(heads):
        ks = slice(hd * dk, (hd + 1) * dk)
        vs = slice(hd * dv, (hd + 1) * dv)
        q = q_ref[:, ks]
        k = k_all[:, ks]
        b = b_all[:, ks]
        ib = i_ref[:, vs].astype(BF16)

        qb = q.astype(BF16)
        a = jnp.where(level == -1, _dot_nt(qb, k.astype(BF16)), 0.0)
        kp = k * pltpu.roll(f[:, ks], L - 1, 0)
        a = jnp.where(level == 0, _dot_nt(qb, kp.astype(BF16)), a)
        fw = jnp.where((row & 1) == 1, pltpu.roll(b, 1, 0), b)
        for lw in range(1, n_levels):
            w = 1 << lw
            gw = pltpu.roll(fw, L - w, 0)
            qp = q * jnp.exp(b - fw)
            kp = k * jnp.exp(jnp.minimum(gw - b, 0.0))
            a = jnp.where(level == lw, _dot_nt(qp.astype(BF16), kp.astype(BF16)), a)
            fw = jnp.where(((row >> lw) & 1) == 1, pltpu.roll(fw, w, 0), fw)

        st = st_scr[hd]
        o = _dot(a.astype(BF16), ib) + _dot_nt((q * jnp.exp(b)).astype(BF16), st.astype(BF16))
        g = b[L - 1:L, :]
        ke = k * jnp.exp(g - b)
        st_scr[hd] = st * jnp.exp(g) + _dot_tn(ib, ke.astype(BF16))
        out = _rms(o) * norm_ref[:, vs] * _silu(g_ref[:, vs])
        out_ref[:, vs] = out.astype(out_ref.dtype)


def hgrn_scan(proj, col_block0, lower_bounds, head_norm, batch, layer, *, chunk, heads=H_HEADS, dk=H_DK, dv=H_DV):
    rows = proj.shape[0]
    s_len = rows // batch
    L = min(chunk, s_len)
    nc = s_len // L
    assert L & (L - 1) == 0 and dk == dv
    w = heads * dk
    depth = lower_bounds.shape[0]
    kern = functools.partial(_hgrn_kernel, heads=heads, dk=dk, dv=dv, layer=layer)
    blk = lambda o: pl.BlockSpec((L, w), functools.partial(lambda b, c, o: (b * nc + c, o), o=col_block0 + o))
    return pl.pallas_call(
        kern,
        grid=(batch, nc),
        in_specs=[blk(0), blk(1), blk(2), blk(3),
                  pl.BlockSpec((depth, w), lambda b, c: (0, 0)),
                  pl.BlockSpec((1, w), lambda b, c: (0, 0))],
        out_specs=pl.BlockSpec((L, w), lambda b, c: (b * nc + c, 0)),
        out_shape=jax.ShapeDtypeStruct((rows, w), BF16),
        scratch_shapes=[pltpu.VMEM((heads, dv, dk), F32)],
        compiler_params=_cparams(("parallel", "arbitrary")),
        name="hgrn_scan",
    )(proj, proj, proj, proj, lower_bounds, head_norm.reshape(1, w))


def _ffn(u, w_gate_up, w_down, layer):
    d_ff, d = w_down.shape[1:]
    w_gu = cast_tiled_bf16(w_gate_up, layer, FFN_UP_TN)
    act = matmul(u, w_gu, d_ff, tm=MM_TM, w_col_offsets=(0, d_ff), epilogue=_epi_swiglu,
                 out_dtype=BF16, name="ffn_gate_up")
    return matmul(act, cast_tiled_bf16(w_down, layer, MM_TN), d, tm=FFN_DOWN_TM, name="ffn_down")


def _token_mixers(u, layer, batch, w_in, gate_bias, conv_w, m_norm, lower_bounds, h_norm,
                  w_bm, w_bh, w_gate, b_gate, w_out):
    rows, d = u.shape
    s_len = rows // batch
    gate_lo = 2 * M_HEADS * M_DQK + 2 * M_HEADS * M_DV
    gate_hi = gate_lo + 2 * M_HEADS
    w_in_l = w_in[layer]
    w_main = jnp.concatenate([w_in_l[:, :gate_lo], w_in_l[:, gate_hi:]], axis=1)[None]
    w_gates = jnp.pad(w_in_l[:, gate_lo:gate_hi], ((0, 0), (0, LANES - 2 * M_HEADS)))[None]
    proj = matmul(u, cast_tiled_bf16(w_main, 0, MM_TN), w_main.shape[2], tm=MM_TM, name="mix_in_proj")
    gates_col = matmul(u, cast_tiled_bf16(w_gates, 0, LANES), LANES, tm=MM_TM, name="mix_gate_proj")
    gates_row = gates_col[:, :2 * M_HEADS].reshape(batch, s_len, 2 * M_HEADS).transpose(0, 2, 1)
    ym = mlstm_scan(proj, gates_col, gates_row, gate_bias, conv_w, m_norm, batch, chunk=MLSTM_CHUNK)
    hgrn_w = H_HEADS * H_DK
    yh = hgrn_scan(proj, gate_lo // hgrn_w, lower_bounds, h_norm, batch, layer, chunk=HGRN_CHUNK)
    gate = matmul(u, cast_tiled_bf16(w_gate, layer, MM_TN), 2 * d, tm=MM_TM, epilogue=_epi_sigmoid_bias,
                  extras=((b_gate.reshape(1, 2 * d), 0),), out_dtype=BF16, name="merge_gate")
    y_m = matmul(ym, cast_tiled_bf16(w_bm, layer, MM_TN), d, tm=MM_TM, name="branch_mlstm")
    z = matmul(yh, cast_tiled_bf16(w_bh, layer, MM_TN), d, tm=MM_TM, epilogue=_epi_merge,
               extras=((gate, 0), (gate, d), (y_m, 0)), out_dtype=BF16, name="branch_hgrn_merge")
    return matmul(z, cast_tiled_bf16(w_out, layer, MM_TN), d, tm=MM_TM, name="mix_out")


def _xattn(u, mem2, mem_norm, batch, layer, w_q, w_kv, w_o):
    memn = rmsnorm_bf16(mem2, mem_norm)
    kv = matmul(memn, cast_tiled_bf16(w_kv, layer, MM_TN), w_kv.shape[2], tm=MM_TM, out_dtype=BF16, name="xattn_kv")
    q = matmul(u, cast_tiled_bf16(w_q, layer, MM_TN), w_q.shape[2], tm=MM_TM, out_dtype=BF16, name="xattn_q")
    o = xattn_core(q, kv, batch, X_HEADS, X_DH)
    return matmul(o, cast_tiled_bf16(w_o, layer, MM_TN), w_o.shape[2], tm=MM_TM, name="xattn_o")


def kernel(x, mem, ffn1_pre_norm, ffn1_w_gate_up, ffn1_w_down, ffn1_post_norm, mix_pre_norm, mix_w_in, mlstm_gate_bias, mlstm_conv, mlstm_head_norm, hgrn_lower_bounds, hgrn_head_norm, branch_w_mlstm, branch_w_hgrn, merge_w_gate, merge_b_gate, mix_w_out, mix_post_norm, xattn_pre_norm, xattn_mem_norm, xattn_w_q, xattn_w_kv, xattn_w_o, xattn_post_norm, ffn2_pre_norm, ffn2_w_gate_up, ffn2_w_down, ffn2_post_norm):
    batch, s_len, d = x.shape
    n_mem = mem.shape[1]
    depth = ffn1_pre_norm.shape[0]

    h = x.reshape(batch * s_len, d)
    mem2 = mem.reshape(batch * n_mem, d)
    u = rmsnorm_bf16(h, ffn1_pre_norm[0])
    for l in range(depth):
        y = _ffn(u, ffn1_w_gate_up, ffn1_w_down, l)
        h, u = combine(y, h, ffn1_post_norm[l], 0.5, mix_pre_norm[l])

        y = _token_mixers(u, l, batch, mix_w_in, mlstm_gate_bias[l], mlstm_conv[l], mlstm_head_norm[l],
                          hgrn_lower_bounds, hgrn_head_norm[l], branch_w_mlstm, branch_w_hgrn, merge_w_gate,
                          merge_b_gate[l], mix_w_out)
        h, u = combine(y, h, mix_post_norm[l], 1.0, xattn_pre_norm[l])

        y = _xattn(u, mem2, xattn_mem_norm[l], batch, l, xattn_w_q, xattn_w_kv, xattn_w_o)
        h, u = combine(y, h, xattn_post_norm[l], 1.0, ffn2_pre_norm[l])

        y = _ffn(u, ffn2_w_gate_up, ffn2_w_down, l)
        nxt = ffn1_pre_norm[l + 1] if l + 1 < depth else None
        h, u = combine(y, h, ffn2_post_norm[l], 0.5, nxt)
    return h.reshape(batch, s_len, d)
```

```python
import functools
import math
from typing import NamedTuple

import jax
import jax.numpy as jnp
from jax import lax
from jax.experimental import pallas as pl
from jax.experimental.pallas import tpu as pltpu

EPS = 1e-6
F32 = jnp.float32
BF16 = jnp.bfloat16

V7X_VMEM_BYTES = 64 * 1024 * 1024
VMEM_LIMIT = V7X_VMEM_BYTES - 8 * 1024 * 1024
LANES = 128
SUBLANES = 8

CONV_K = 4
M_HEADS = 8
M_DQK = 128
M_DV = 256
H_HEADS = 16
H_DK = 128
H_DV = 128
X_HEADS = 4
X_DH = 256
MLSTM_CHUNK = 256
HGRN_CHUNK = 64
CAST_BLOCK_ELEMS = 2 * 1024 * 1024
MM_TM = 1024
MM_TN = 512
MM_WIDE_TN = 1024
FFN_UP_TN = 256
FFN_DOWN_TM = 512


def _cparams(sem):
    return pltpu.CompilerParams(dimension_semantics=sem, vmem_limit_bytes=VMEM_LIMIT)


def _dot(a, b):
    return jnp.dot(a, b, preferred_element_type=F32)


def _dot_nt(a, b):
    return lax.dot_general(a, b, (((1,), (1,)), ((), ())), preferred_element_type=F32)


def _dot_tn(a, b):
    return lax.dot_general(a, b, (((0,), (0,)), ((), ())), preferred_element_type=F32)


def _sigmoid(x):
    return 1.0 / (1.0 + jnp.exp(-x))


def _silu(x):
    return x * _sigmoid(x)


def _log_sigmoid(x):
    return jnp.minimum(x, 0.0) - jnp.log(1.0 + jnp.exp(-jnp.abs(x)))


def _rms(x):
    return x * lax.rsqrt(jnp.mean(x * x, axis=-1, keepdims=True) + EPS)


def _rmsnorm_kernel(x_ref, g_ref, o_ref):
    o_ref[...] = (_rms(x_ref[...]) * g_ref[...]).astype(o_ref.dtype)


def rmsnorm_bf16(x, g, tm=256):
    m, d = x.shape
    tm = min(tm, m)
    return pl.pallas_call(
        _rmsnorm_kernel,
        grid=(m // tm,),
        in_specs=[pl.BlockSpec((tm, d), lambda i: (i, 0)), pl.BlockSpec((1, d), lambda i: (0, 0))],
        out_specs=pl.BlockSpec((tm, d), lambda i: (i, 0)),
        out_shape=jax.ShapeDtypeStruct((m, d), BF16),
        compiler_params=_cparams(("parallel",)),
        name="rmsnorm_bf16",
    )(x, g.reshape(1, d))


def _cast_kernel(x_ref, o_ref):
    o_ref[...] = x_ref[...].astype(o_ref.dtype)


def _largest_divisor(n, cap, multiple):
    best = multiple
    for c in range(multiple, min(n, cap) + 1, multiple):
        if n % c == 0:
            best = c
    return best


def cast_tiled_bf16(w3, layer, tn):
    _, k, n = w3.shape
    tn = min(tn, n)
    assert n % tn == 0
    tk = _largest_divisor(k, max(SUBLANES, CAST_BLOCK_ELEMS // tn), 2 * SUBLANES)
    return pl.pallas_call(
        _cast_kernel,
        grid=(n // tn, k // tk),
        in_specs=[pl.BlockSpec((None, tk, tn), lambda j, i: (layer, i, j))],
        out_specs=pl.BlockSpec((None, tk, tn), lambda j, i: (j, i, 0)),
        out_shape=jax.ShapeDtypeStruct((n // tn, k, tn), BF16),
        compiler_params=_cparams(("parallel", "parallel")),
        name="cast_tiled_bf16",
    )(w3)


def _mm_kernel(*refs, n_w, epilogue):
    a_ref = refs[0]
    w_refs = refs[1:1 + n_w]
    extra_refs = refs[1 + n_w:-1]
    o_ref = refs[-1]
    a = a_ref[...]
    accs = [_dot(a, w[...].astype(BF16)) for w in w_refs]
    o_ref[...] = epilogue(accs, [e[...] for e in extra_refs]).astype(o_ref.dtype)


class RawWeight(NamedTuple):
    w3: jax.Array
    layer: int
    tn: int


def matmul(a, w, n_out, *, tm, w_col_offsets=(0,), epilogue=None, extras=(), out_dtype=F32, name="matmul"):
    m, k = a.shape
    raw = isinstance(w, RawWeight)
    tn = w.tn if raw else w.shape[2]
    w_arr = w.w3 if raw else w
    tm = min(tm, m)
    assert m % tm == 0 and n_out % tn == 0 and w_arr.shape[1] == k
    if epilogue is None:
        epilogue = lambda accs, ex: accs[0]
    in_specs = [pl.BlockSpec((tm, k), lambda i, j: (i, 0))]
    args = [a]
    for off in w_col_offsets:
        assert off % tn == 0
        if raw:
            in_specs.append(pl.BlockSpec((None, k, tn),
                                         functools.partial(lambda i, j, o: (w.layer, 0, j + o), o=off // tn)))
        else:
            in_specs.append(pl.BlockSpec((None, k, tn),
                                         functools.partial(lambda i, j, o: (j + o, 0, 0), o=off // tn)))
        args.append(w_arr)
    for arr, off in extras:
        assert off % tn == 0
        rows = arr.shape[0]
        if rows == 1:
            in_specs.append(pl.BlockSpec((1, tn), functools.partial(lambda i, j, o: (0, j + o), o=off // tn)))
        else:
            in_specs.append(pl.BlockSpec((tm, tn), functools.partial(lambda i, j, o: (i, j + o), o=off // tn)))
        args.append(arr)
    return pl.pallas_call(
        functools.partial(_mm_kernel, n_w=len(w_col_offsets), epilogue=epilogue),
        grid=(m // tm, n_out // tn),
        in_specs=in_specs,
        out_specs=pl.BlockSpec((tm, tn), lambda i, j: (i, j)),
        out_shape=jax.ShapeDtypeStruct((m, n_out), out_dtype),
        compiler_params=_cparams(("parallel", "arbitrary")),
        name=name,
    )(*args)


def _epi_swiglu(accs, ex):
    return _silu(accs[0]) * accs[1]


def _epi_sigmoid_bias(accs, ex):
    return _sigmoid(accs[0] + ex[0])


def _epi_merge(accs, ex):
    gm, gh, ym = ex
    return gm.astype(F32) * ym + gh.astype(F32) * accs[0]


def _combine_kernel(y_ref, h_ref, post_ref, pre_ref, hn_ref, u_ref, *, scale):
    hn = h_ref[...] + scale * (_rms(y_ref[...].astype(F32)) * post_ref[...])
    hn_ref[...] = hn
    u_ref[...] = (_rms(hn) * pre_ref[...]).astype(u_ref.dtype)


def _combine_last_kernel(y_ref, h_ref, post_ref, hn_ref, *, scale):
    hn_ref[...] = h_ref[...] + scale * (_rms(y_ref[...].astype(F32)) * post_ref[...])


def combine(y, h, post, scale, next_pre=None, tm=256):
    m, d = y.shape
    tm = min(tm, m)
    row = pl.BlockSpec((tm, d), lambda i: (i, 0))
    vec = pl.BlockSpec((1, d), lambda i: (0, 0))
    if next_pre is None:
        return pl.pallas_call(
            functools.partial(_combine_last_kernel, scale=scale),
            grid=(m // tm,),
            in_specs=[row, row, vec],
            out_specs=row,
            out_shape=jax.ShapeDtypeStruct((m, d), F32),
            compiler_params=_cparams(("parallel",)),
            name="combine_last",
        )(y, h, post.reshape(1, d)), None
    return pl.pallas_call(
        functools.partial(_combine_kernel, scale=scale),
        grid=(m // tm,),
        in_specs=[row, row, vec, vec],
        out_specs=[row, row],
        out_shape=[jax.ShapeDtypeStruct((m, d), F32), jax.ShapeDtypeStruct((m, d), BF16)],
        compiler_params=_cparams(("parallel",)),
        name="combine",
    )(y, h, post.reshape(1, d), next_pre.reshape(1, d))


def _xattn_kernel(q_ref, k_ref, v_ref, o_ref, *, heads, dh):
    scale = dh ** -0.5
    for hd in range(heads):
        sl = slice(hd * dh, (hd + 1) * dh)
        s = _dot_nt(q_ref[:, sl], k_ref[:, sl]) * scale
        s = s - jnp.max(s, axis=-1, keepdims=True)
        p = jnp.exp(s)
        p = p / jnp.sum(p, axis=-1, keepdims=True)
        o_ref[:, sl] = _dot(p.astype(BF16), v_ref[:, sl]).astype(o_ref.dtype)


def xattn_core(q, kv, batch, heads, dh, tm=512):
    rows, width = q.shape
    s_len = rows // batch
    n_mem = kv.shape[0] // batch
    tm = min(tm, s_len)
    nt = s_len // tm
    return pl.pallas_call(
        functools.partial(_xattn_kernel, heads=heads, dh=dh),
        grid=(batch, nt),
        in_specs=[
            pl.BlockSpec((tm, width), lambda b, i: (b * nt + i, 0)),
            pl.BlockSpec((n_mem, width), lambda b, i: (b, 0)),
            pl.BlockSpec((n_mem, width), lambda b, i: (b, 1)),
        ],
        out_specs=pl.BlockSpec((tm, width), lambda b, i: (b * nt + i, 0)),
        out_shape=jax.ShapeDtypeStruct((rows, width), BF16),
        compiler_params=_cparams(("parallel", "arbitrary")),
        name="xattn_core",
    )(q, kv, kv)


def _causal_conv_silu(x, tail, w):
    rows8 = lax.broadcasted_iota(jnp.int32, (SUBLANES, 1), 0)
    y = x * w[CONV_K - 1:CONV_K]
    for d in range(1, CONV_K):
        xs = pltpu.roll(x, d, 0)
        top = jnp.where(rows8 < d, pltpu.roll(tail, d, 0), xs[0:SUBLANES])
        xs = jnp.concatenate([top, xs[SUBLANES:]], axis=0)
        y = y + xs * w[CONV_K - 1 - d:CONV_K - d]
    return _silu(y)


def _mlstm_kernel(q_ref, k_ref, v_ref, og_ref, gc_ref, gr_ref, bc_ref, br_ref, conv_ref, norm_ref, out_ref,
                  c_scr, n_scr, m_scr, tq_scr, tk_scr, *, heads, dqk, dv):
    L = q_ref.shape[0]
    qk_w = heads * dqk

    @pl.when(pl.program_id(1) == 0)
    def _():
        c_scr[...] = jnp.zeros_like(c_scr)
        n_scr[...] = jnp.zeros_like(n_scr)
        m_scr[...] = jnp.zeros_like(m_scr)
        tq_scr[...] = jnp.zeros_like(tq_scr)
        tk_scr[...] = jnp.zeros_like(tk_scr)

    q_raw = q_ref[...]
    k_raw = k_ref[...]
    q_all = _causal_conv_silu(q_raw, tq_scr[...], conv_ref[:, 0:qk_w]) * (dqk ** -0.5)
    k_all = _causal_conv_silu(k_raw, tk_scr[...], conv_ref[:, qk_w:2 * qk_w])
    tq_scr[...] = q_raw[L - SUBLANES:L]
    tk_scr[...] = k_raw[L - SUBLANES:L]

    gc = gc_ref[...] + bc_ref[...]
    gr = gr_ref[0] + br_ref[...]
    t_idx = lax.broadcasted_iota(jnp.int32, (L, L), 0)
    s_idx = lax.broadcasted_iota(jnp.int32, (L, L), 1)
    causal = s_idx <= t_idx
    tri = causal.astype(F32)
    bc_all = jnp.dot(tri, _log_sigmoid(gc), precision=lax.Precision.HIGHEST, preferred_element_type=F32)
    br_all = lax.dot_general(_log_sigmoid(gr), tri, (((1,), (1,)), ((), ())),
                             precision=lax.Precision.HIGHEST, preferred_element_type=F32)

    for hd in range(heads):
        qs = slice(hd * dqk, (hd + 1) * dqk)
        vs = slice(hd * dv, (hd + 1) * dv)
        q = q_all[:, qs]
        k = k_all[:, qs]
        qb = q.astype(BF16)
        kb = k.astype(BF16)
        vb = v_ref[:, vs].astype(BF16)
        b_c = bc_all[:, heads + hd:heads + hd + 1]
        li_c = gc[:, hd:hd + 1]
        b_r = br_all[heads + hd:heads + hd + 1, :]
        li_r = gr[hd:hd + 1, :]
        m_prev = m_scr[hd:hd + 1, 0:1]
        n_prev = n_scr[hd:hd + 1, :]

        d = jnp.where(causal, b_c - b_r + li_r, -jnp.inf)
        m_inter = b_c + m_prev
        m_t = jnp.maximum(m_inter, jnp.max(d, axis=1, keepdims=True))
        w_intra = jnp.exp(d - m_t)
        w_inter = jnp.exp(m_inter - m_t)
        s = _dot_nt(qb, kb) * w_intra
        num = _dot(s.astype(BF16), vb) + w_inter * _dot(qb, c_scr[hd].astype(BF16))
        den = jnp.sum(s, axis=1, keepdims=True) + w_inter * jnp.sum(q * n_prev, axis=1, keepdims=True)
        hh = num / jnp.maximum(jnp.abs(den), jnp.exp(-m_t))
        out = _rms(hh) * norm_ref[:, vs] * _sigmoid(og_ref[:, vs])
        out_ref[:, vs] = out.astype(out_ref.dtype)

        g = b_c[L - 1:L, :]
        a_c = g - b_c + li_c
        m_new = jnp.maximum(g + m_prev, jnp.max(a_c, axis=0, keepdims=True))
        decay = jnp.exp(g + m_prev - m_new)
        kw = k * jnp.exp(a_c - m_new)
        c_scr[hd] = decay * c_scr[hd] + _dot_tn(kw.astype(BF16), vb)
        n_scr[hd:hd + 1, :] = decay * n_prev + jnp.sum(kw, axis=0, keepdims=True)
        m_scr[hd:hd + 1, :] = jnp.broadcast_to(m_new, (1, LANES))


def mlstm_scan(proj, gates_col, gates_row, gate_bias, conv_w, head_norm, batch, *, chunk, heads=M_HEADS,
               dqk=M_DQK, dv=M_DV):
    rows = proj.shape[0]
    s_len = rows // batch
    L = min(chunk, s_len)
    nc = s_len // L
    qk_w = heads * dqk
    v_w = heads * dv
    assert qk_w % LANES == 0 and v_w % qk_w == 0
    bias_col = jnp.zeros((1, LANES), F32).at[0, :2 * heads].set(gate_bias)
    bias_row = gate_bias.reshape(2 * heads, 1)
    kern = functools.partial(_mlstm_kernel, heads=heads, dqk=dqk, dv=dv)
    return pl.pallas_call(
        kern,
        grid=(batch, nc),
        in_specs=[
            pl.BlockSpec((L, qk_w), lambda b, c: (b * nc + c, 0)),
            pl.BlockSpec((L, qk_w), lambda b, c: (b * nc + c, 1)),
            pl.BlockSpec((L, v_w), lambda b, c: (b * nc + c, 2 * qk_w // v_w)),
            pl.BlockSpec((L, v_w), lambda b, c: (b * nc + c, 2 * qk_w // v_w + 1)),
            pl.BlockSpec((L, LANES), lambda b, c: (b * nc + c, 0)),
            pl.BlockSpec((1, 2 * heads, L), lambda b, c: (b, 0, c)),
            pl.BlockSpec((1, LANES), lambda b, c: (0, 0)),
            pl.BlockSpec((2 * heads, 1), lambda b, c: (0, 0)),
            pl.BlockSpec((CONV_K, 2 * qk_w), lambda b, c: (0, 0)),
            pl.BlockSpec((1, v_w), lambda b, c: (0, 0)),
        ],
        out_specs=pl.BlockSpec((L, v_w), lambda b, c: (b * nc + c, 0)),
        out_shape=jax.ShapeDtypeStruct((rows, v_w), BF16),
        scratch_shapes=[
            pltpu.VMEM((heads, dqk, dv), F32),
            pltpu.VMEM((heads, dqk), F32),
            pltpu.VMEM((heads, LANES), F32),
            pltpu.VMEM((SUBLANES, qk_w), F32),
            pltpu.VMEM((SUBLANES, qk_w), F32),
        ],
        compiler_params=_cparams(("parallel", "arbitrary")),
        name="mlstm_scan",
    )(proj, proj, proj, proj, gates_col, gates_row, bias_col, bias_row, conv_w, head_norm.reshape(1, v_w))


def _hgrn_kernel(q_ref, f_ref, i_ref, g_ref, lb_ref, norm_ref, out_ref, st_scr, *, heads, dk, dv, layer):
    L = q_ref.shape[0]

    @pl.when(pl.program_id(1) == 0)
    def _():
        st_scr[...] = jnp.zeros_like(st_scr)

    lbp = lb_ref[...]
    e = jnp.exp(lbp - jnp.max(lbp, axis=0, keepdims=True))
    p = e / jnp.sum(e, axis=0, keepdims=True)
    lb = jnp.zeros((1, lbp.shape[1]), F32)
    for j in range(1, layer + 1):
        lb = lb + p[j:j + 1]

    f = lb + (1.0 - lb) * _sigmoid(f_ref[...])
    lg = jnp.log(f)
    k_all = 1.0 - f
    t_idx = lax.broadcasted_iota(jnp.int32, (L, L), 0)
    s_idx = lax.broadcasted_iota(jnp.int32, (L, L), 1)
    tri = (s_idx <= t_idx).astype(F32)
    b_all = jnp.dot(tri, lg, precision=lax.Precision.HIGHEST, preferred_element_type=F32)
    level = jnp.where(t_idx > s_idx, 31 - lax.clz(t_idx ^ s_idx), jnp.where(t_idx == s_idx, -1, -2))
    row = lax.broadcasted_iota(jnp.int32, (L, 1), 0)
    n_levels = int(math.log2(L))
    k_slices = [slice(hd * dk, (hd + 1) * dk) for hd in range(heads)]
    q_all = q_ref[...]
    qb_all = q_all.astype(BF16)

    def scores(qx, kx, lw, acc):
        keep = level == lw
        return [jnp.where(keep, _dot_nt(qx[:, sl], kx[:, sl]), acc[i]) for i, sl in enumerate(k_slices)]

    a = scores(qb_all, k_all.astype(BF16), -1, [0.0] * heads)
    a = scores(qb_all, (k_all * pltpu.roll(f, L - 1, 0)).astype(BF16), 0, a)
    fw = jnp.where((row & 1) == 1, pltpu.roll(b_all, 1, 0), b_all)
    for lw in range(1, n_levels):
        w = 1 << lw
        odd = ((row >> lw) & 1) == 1
        gw = pltpu.roll(fw, L - w, 0)
        e = jnp.exp(jnp.where(odd, b_all, gw) - jnp.where(odd, fw, b_all))
        a = scores((q_all * e).astype(BF16), (k_all * e).astype(BF16), lw, a)
        fw = jnp.where(odd, pltpu.roll(fw, w, 0), fw)

    g = b_all[L - 1:L, :]
    qe_all = (q_all * jnp.exp(b_all)).astype(BF16)
    ke_all = (k_all * jnp.exp(g - b_all)).astype(BF16)
    eg = jnp.exp(g)
    ib_all = i_ref[...].astype(BF16)
    for hd in range(heads):
        ks = k_slices[hd]
        vs = slice(hd * dv, (hd + 1) * dv)
        st = st_scr[hd]
        o = _dot(a[hd].astype(BF16), ib_all[:, vs]) + _dot_nt(qe_all[:, ks], st.astype(BF16))
        st_scr[hd] = st * eg[:, ks] + _dot_tn(ib_all[:, vs], ke_all[:, ks])
        out = _rms(o) * norm_ref[:, vs] * _silu(g_ref[:, vs])
        out_ref[:, vs] = out.astype(out_ref.dtype)


def hgrn_scan(proj, col_block0, lower_bounds, head_norm, batch, layer, *, chunk, heads=H_HEADS, dk=H_DK, dv=H_DV):
    rows = proj.shape[0]
    s_len = rows // batch
    L = min(chunk, s_len)
    nc = s_len // L
    assert L & (L - 1) == 0 and dk == dv
    w = heads * dk
    depth = lower_bounds.shape[0]
    kern = functools.partial(_hgrn_kernel, heads=heads, dk=dk, dv=dv, layer=layer)
    blk = lambda o: pl.BlockSpec((L, w), functools.partial(lambda b, c, o: (b * nc + c, o), o=col_block0 + o))
    return pl.pallas_call(
        kern,
        grid=(batch, nc),
        in_specs=[blk(0), blk(1), blk(2), blk(3),
                  pl.BlockSpec((depth, w), lambda b, c: (0, 0)),
                  pl.BlockSpec((1, w), lambda b, c: (0, 0))],
        out_specs=pl.BlockSpec((L, w), lambda b, c: (b * nc + c, 0)),
        out_shape=jax.ShapeDtypeStruct((rows, w), BF16),
        scratch_shapes=[pltpu.VMEM((heads, dv, dk), F32)],
        compiler_params=_cparams(("parallel", "arbitrary")),
        name="hgrn_scan",
    )(proj, proj, proj, proj, lower_bounds, head_norm.reshape(1, w))


def _ffn(u, w_gate_up, w_down, layer, raw_up):
    d_ff, d = w_down.shape[1:]
    w_gu = RawWeight(w_gate_up, layer, FFN_UP_TN) if raw_up else cast_tiled_bf16(w_gate_up, layer, FFN_UP_TN)
    act = matmul(u, w_gu, d_ff, tm=MM_TM, w_col_offsets=(0, d_ff), epilogue=_epi_swiglu,
                 out_dtype=BF16, name="ffn_gate_up")
    return matmul(act, cast_tiled_bf16(w_down, layer, MM_TN), d, tm=FFN_DOWN_TM, out_dtype=BF16, name="ffn_down")


def _token_mixers(u, layer, batch, w_in, gate_bias, conv_w, m_norm, lower_bounds, h_norm,
                  w_bm, w_bh, w_gate, b_gate, w_out):
    rows, d = u.shape
    s_len = rows // batch
    gate_lo = 2 * M_HEADS * M_DQK + 2 * M_HEADS * M_DV
    gate_hi = gate_lo + 2 * M_HEADS
    w_hgrn = w_in[layer][:, gate_hi:][None]
    w_gates = jnp.pad(w_in[layer][:, gate_lo:gate_hi], ((0, 0), (0, LANES - 2 * M_HEADS)))[None]
    proj_m = matmul(u, RawWeight(w_in, layer, MM_TN), gate_lo, tm=MM_TM, name="mix_in_proj_mlstm")
    proj_h = matmul(u, RawWeight(w_hgrn, 0, MM_TN), w_hgrn.shape[2], tm=MM_TM, name="mix_in_proj_hgrn")
    gates_col = matmul(u, cast_tiled_bf16(w_gates, 0, LANES), LANES, tm=MM_TM, name="mix_gate_proj")
    gates_row = gates_col[:, :2 * M_HEADS].reshape(batch, s_len, 2 * M_HEADS).transpose(0, 2, 1)
    ym = mlstm_scan(proj_m, gates_col, gates_row, gate_bias, conv_w, m_norm, batch, chunk=MLSTM_CHUNK)
    yh = hgrn_scan(proj_h, 0, lower_bounds, h_norm, batch, layer, chunk=HGRN_CHUNK)
    gate = matmul(u, RawWeight(w_gate, layer, MM_TN), 2 * d, tm=MM_TM, epilogue=_epi_sigmoid_bias,
                  extras=((b_gate.reshape(1, 2 * d), 0),), out_dtype=BF16, name="merge_gate")
    y_m = matmul(ym, cast_tiled_bf16(w_bm, layer, MM_WIDE_TN), d, tm=MM_TM, name="branch_mlstm")
    z = matmul(yh, cast_tiled_bf16(w_bh, layer, MM_WIDE_TN), d, tm=MM_TM, epilogue=_epi_merge,
               extras=((gate, 0), (gate, d), (y_m, 0)), out_dtype=BF16, name="branch_hgrn_merge")
    return matmul(z, cast_tiled_bf16(w_out, layer, MM_TN), d, tm=MM_TM, out_dtype=BF16, name="mix_out")


def _xattn(u, mem2, mem_norm, batch, layer, w_q, w_kv, w_o):
    memn = rmsnorm_bf16(mem2, mem_norm)
    kv = matmul(memn, cast_tiled_bf16(w_kv, layer, MM_TN), w_kv.shape[2], tm=MM_TM, out_dtype=BF16, name="xattn_kv")
    q = matmul(u, cast_tiled_bf16(w_q, layer, MM_WIDE_TN), w_q.shape[2], tm=MM_TM, out_dtype=BF16, name="xattn_q")
    o = xattn_core(q, kv, batch, X_HEADS, X_DH)
    return matmul(o, cast_tiled_bf16(w_o, layer, MM_WIDE_TN), w_o.shape[2], tm=MM_TM, out_dtype=BF16,
                  name="xattn_o")


def kernel(x, mem, ffn1_pre_norm, ffn1_w_gate_up, ffn1_w_down, ffn1_post_norm, mix_pre_norm, mix_w_in, mlstm_gate_bias, mlstm_conv, mlstm_head_norm, hgrn_lower_bounds, hgrn_head_norm, branch_w_mlstm, branch_w_hgrn, merge_w_gate, merge_b_gate, mix_w_out, mix_post_norm, xattn_pre_norm, xattn_mem_norm, xattn_w_q, xattn_w_kv, xattn_w_o, xattn_post_norm, ffn2_pre_norm, ffn2_w_gate_up, ffn2_w_down, ffn2_post_norm):
    batch, s_len, d = x.shape
    n_mem = mem.shape[1]
    depth = ffn1_pre_norm.shape[0]

    h = x.reshape(batch * s_len, d)
    mem2 = mem.reshape(batch * n_mem, d)
    u = rmsnorm_bf16(h, ffn1_pre_norm[0])
    for l in range(depth):
        y = _ffn(u, ffn1_w_gate_up, ffn1_w_down, l, raw_up=True)
        h, u = combine(y, h, ffn1_post_norm[l], 0.5, mix_pre_norm[l])

        y = _token_mixers(u, l, batch, mix_w_in, mlstm_gate_bias[l], mlstm_conv[l], mlstm_head_norm[l],
                          hgrn_lower_bounds, hgrn_head_norm[l], branch_w_mlstm, branch_w_hgrn, merge_w_gate,
                          merge_b_gate[l], mix_w_out)
        h, u = combine(y, h, mix_post_norm[l], 1.0, xattn_pre_norm[l])

        y = _xattn(u, mem2, xattn_mem_norm[l], batch, l, xattn_w_q, xattn_w_kv, xattn_w_o)
        h, u = combine(y, h, xattn_post_norm[l], 1.0, ffn2_pre_norm[l])

        y = _ffn(u, ffn2_w_gate_up, ffn2_w_down, l, raw_up=False)
        nxt = ffn1_pre_norm[l + 1] if l + 1 < depth else None
        h, u = combine(y, h, ffn2_post_norm[l], 0.5, nxt)
    return h.reshape(batch, s_len, d)
```

```python
import functools
import math
from typing import NamedTuple

import jax
import jax.numpy as jnp
from jax import lax
from jax.experimental import pallas as pl
from jax.experimental.pallas import tpu as pltpu

EPS = 1e-6
F32 = jnp.float32
BF16 = jnp.bfloat16

V7X_VMEM_BYTES = 64 * 1024 * 1024
VMEM_LIMIT = V7X_VMEM_BYTES - 8 * 1024 * 1024
LANES = 128
SUBLANES = 8

CONV_K = 4
M_HEADS = 8
M_DQK = 128
M_DV = 256
H_HEADS = 16
H_DK = 128
H_DV = 128
X_HEADS = 4
X_DH = 256
MLSTM_CHUNK = 256
HGRN_CHUNK = 64
CAST_BLOCK_ELEMS = 2 * 1024 * 1024
MM_TM = 1024
MM_TN = 512
MM_WIDE_TN = 1024
FFN_UP_TN = 256
FFN_DOWN_TM = 512


def _cparams(sem):
    return pltpu.CompilerParams(dimension_semantics=sem, vmem_limit_bytes=VMEM_LIMIT)


def _dot(a, b):
    return jnp.dot(a, b, preferred_element_type=F32)


def _dot_nt(a, b):
    return lax.dot_general(a, b, (((1,), (1,)), ((), ())), preferred_element_type=F32)


def _dot_tn(a, b):
    return lax.dot_general(a, b, (((0,), (0,)), ((), ())), preferred_element_type=F32)


def _sigmoid(x):
    return 1.0 / (1.0 + jnp.exp(-x))


def _silu(x):
    return x * _sigmoid(x)


def _log_sigmoid(x):
    return jnp.minimum(x, 0.0) - jnp.log(1.0 + jnp.exp(-jnp.abs(x)))


def _rms(x):
    return x * lax.rsqrt(jnp.mean(x * x, axis=-1, keepdims=True) + EPS)


def _rmsnorm_kernel(x_ref, g_ref, o_ref):
    o_ref[...] = (_rms(x_ref[...]) * g_ref[...]).astype(o_ref.dtype)


def rmsnorm_bf16(x, g, tm=256):
    m, d = x.shape
    tm = min(tm, m)
    return pl.pallas_call(
        _rmsnorm_kernel,
        grid=(m // tm,),
        in_specs=[pl.BlockSpec((tm, d), lambda i: (i, 0)), pl.BlockSpec((1, d), lambda i: (0, 0))],
        out_specs=pl.BlockSpec((tm, d), lambda i: (i, 0)),
        out_shape=jax.ShapeDtypeStruct((m, d), BF16),
        compiler_params=_cparams(("parallel",)),
        name="rmsnorm_bf16",
    )(x, g.reshape(1, d))


def _cast_kernel(x_ref, o_ref):
    o_ref[...] = x_ref[...].astype(o_ref.dtype)


def _largest_divisor(n, cap, multiple):
    best = multiple
    for c in range(multiple, min(n, cap) + 1, multiple):
        if n % c == 0:
            best = c
    return best


def cast_tiled_bf16(w3, layer, tn):
    _, k, n = w3.shape
    tn = min(tn, n)
    assert n % tn == 0
    tk = _largest_divisor(k, max(SUBLANES, CAST_BLOCK_ELEMS // tn), 2 * SUBLANES)
    return pl.pallas_call(
        _cast_kernel,
        grid=(n // tn, k // tk),
        in_specs=[pl.BlockSpec((None, tk, tn), lambda j, i: (layer, i, j))],
        out_specs=pl.BlockSpec((None, tk, tn), lambda j, i: (j, i, 0)),
        out_shape=jax.ShapeDtypeStruct((n // tn, k, tn), BF16),
        compiler_params=_cparams(("parallel", "parallel")),
        name="cast_tiled_bf16",
    )(w3)


def _mm_kernel(*refs, n_w, epilogue, transposed):
    a_ref = refs[0]
    w_refs = refs[1:1 + n_w]
    extra_refs = refs[1 + n_w:-1]
    o_ref = refs[-1]
    a = a_ref[...]
    if transposed:
        accs = [_dot_nt(a, w[0].astype(BF16)) for w in w_refs]
    else:
        accs = [_dot(a, w[...].astype(BF16)) for w in w_refs]
    o_ref[...] = epilogue(accs, [e[...] for e in extra_refs]).astype(o_ref.dtype)


class RawWeight(NamedTuple):
    w3: jax.Array
    layer: int
    tn: int


class RawWeightT(NamedTuple):
    w3t: jax.Array
    layer: int
    tn: int
    row0: int


def matmul(a, w, n_out, *, tm, w_col_offsets=(0,), epilogue=None, extras=(), out_dtype=F32, name="matmul"):
    m, k = a.shape
    tm = min(tm, m)
    if epilogue is None:
        epilogue = lambda accs, ex: accs[0]
    in_specs = [pl.BlockSpec((tm, k), lambda i, j: (i, 0))]
    args = [a]
    transposed = isinstance(w, RawWeightT)
    tn = w.shape[2] if isinstance(w, jax.Array) else w.tn
    assert m % tm == 0 and n_out % tn == 0
    for off in w_col_offsets:
        assert off % tn == 0
        if transposed:
            assert w.w3t.shape[2] == k and w.row0 % SUBLANES == 0
            spec = pl.BlockSpec((pl.Element(1), pl.Element(tn), pl.Element(k)),
                                functools.partial(lambda i, j, o: (w.layer, pl.multiple_of(o + j * tn, SUBLANES), 0),
                                                  o=w.row0 + off))
        elif isinstance(w, RawWeight):
            assert w.w3.shape[1] == k
            spec = pl.BlockSpec((None, k, tn), functools.partial(lambda i, j, o: (w.layer, 0, j + o), o=off // tn))
        else:
            assert w.shape[1] == k
            spec = pl.BlockSpec((None, k, tn), functools.partial(lambda i, j, o: (j + o, 0, 0), o=off // tn))
        in_specs.append(spec)
        args.append(w if isinstance(w, jax.Array) else w[0])
    for arr, off in extras:
        assert off % tn == 0
        rows = arr.shape[0]
        if rows == 1:
            in_specs.append(pl.BlockSpec((1, tn), functools.partial(lambda i, j, o: (0, j + o), o=off // tn)))
        else:
            in_specs.append(pl.BlockSpec((tm, tn), functools.partial(lambda i, j, o: (i, j + o), o=off // tn)))
        args.append(arr)
    return pl.pallas_call(
        functools.partial(_mm_kernel, n_w=len(w_col_offsets), epilogue=epilogue, transposed=transposed),
        grid=(m // tm, n_out // tn),
        in_specs=in_specs,
        out_specs=pl.BlockSpec((tm, tn), lambda i, j: (i, j)),
        out_shape=jax.ShapeDtypeStruct((m, n_out), out_dtype),
        compiler_params=_cparams(("parallel", "arbitrary")),
        name=name,
    )(*args)


def _epi_swiglu(accs, ex):
    return _silu(accs[0]) * accs[1]


def _epi_sigmoid_bias(accs, ex):
    return _sigmoid(accs[0] + ex[0])


def _epi_merge(accs, ex):
    gm, gh, ym = ex
    return gm.astype(F32) * ym + gh.astype(F32) * accs[0]


def _combine_kernel(y_ref, h_ref, post_ref, pre_ref, hn_ref, u_ref, *, scale):
    hn = h_ref[...] + scale * (_rms(y_ref[...].astype(F32)) * post_ref[...])
    hn_ref[...] = hn
    u_ref[...] = (_rms(hn) * pre_ref[...]).astype(u_ref.dtype)


def _combine_last_kernel(y_ref, h_ref, post_ref, hn_ref, *, scale):
    hn_ref[...] = h_ref[...] + scale * (_rms(y_ref[...].astype(F32)) * post_ref[...])


def combine(y, h, post, scale, next_pre=None, tm=256):
    m, d = y.shape
    tm = min(tm, m)
    row = pl.BlockSpec((tm, d), lambda i: (i, 0))
    vec = pl.BlockSpec((1, d), lambda i: (0, 0))
    if next_pre is None:
        return pl.pallas_call(
            functools.partial(_combine_last_kernel, scale=scale),
            grid=(m // tm,),
            in_specs=[row, row, vec],
            out_specs=row,
            out_shape=jax.ShapeDtypeStruct((m, d), F32),
            compiler_params=_cparams(("parallel",)),
            name="combine_last",
        )(y, h, post.reshape(1, d)), None
    return pl.pallas_call(
        functools.partial(_combine_kernel, scale=scale),
        grid=(m // tm,),
        in_specs=[row, row, vec, vec],
        out_specs=[row, row],
        out_shape=[jax.ShapeDtypeStruct((m, d), F32), jax.ShapeDtypeStruct((m, d), BF16)],
        compiler_params=_cparams(("parallel",)),
        name="combine",
    )(y, h, post.reshape(1, d), next_pre.reshape(1, d))


def _xattn_kernel(q_ref, k_ref, v_ref, o_ref, *, heads, dh):
    scale = dh ** -0.5
    for hd in range(heads):
        sl = slice(hd * dh, (hd + 1) * dh)
        s = _dot_nt(q_ref[:, sl], k_ref[:, sl]) * scale
        s = s - jnp.max(s, axis=-1, keepdims=True)
        p = jnp.exp(s)
        p = p / jnp.sum(p, axis=-1, keepdims=True)
        o_ref[:, sl] = _dot(p.astype(BF16), v_ref[:, sl]).astype(o_ref.dtype)


def xattn_core(q, kv, batch, heads, dh, tm=512):
    rows, width = q.shape
    s_len = rows // batch
    n_mem = kv.shape[0] // batch
    tm = min(tm, s_len)
    nt = s_len // tm
    return pl.pallas_call(
        functools.partial(_xattn_kernel, heads=heads, dh=dh),
        grid=(batch, nt),
        in_specs=[
            pl.BlockSpec((tm, width), lambda b, i: (b * nt + i, 0)),
            pl.BlockSpec((n_mem, width), lambda b, i: (b, 0)),
            pl.BlockSpec((n_mem, width), lambda b, i: (b, 1)),
        ],
        out_specs=pl.BlockSpec((tm, width), lambda b, i: (b * nt + i, 0)),
        out_shape=jax.ShapeDtypeStruct((rows, width), BF16),
        compiler_params=_cparams(("parallel", "arbitrary")),
        name="xattn_core",
    )(q, kv, kv)


def _causal_conv_silu(x, tail, w):
    rows8 = lax.broadcasted_iota(jnp.int32, (SUBLANES, 1), 0)
    y = x * w[CONV_K - 1:CONV_K]
    for d in range(1, CONV_K):
        xs = pltpu.roll(x, d, 0)
        top = jnp.where(rows8 < d, pltpu.roll(tail, d, 0), xs[0:SUBLANES])
        xs = jnp.concatenate([top, xs[SUBLANES:]], axis=0)
        y = y + xs * w[CONV_K - 1 - d:CONV_K - d]
    return _silu(y)


def _mlstm_kernel(q_ref, k_ref, v_ref, og_ref, gc_ref, gr_ref, bc_ref, br_ref, conv_ref, norm_ref, out_ref,
                  c_scr, n_scr, m_scr, tq_scr, tk_scr, *, heads, dqk, dv):
    L = q_ref.shape[0]
    qk_w = heads * dqk

    @pl.when(pl.program_id(1) == 0)
    def _():
        c_scr[...] = jnp.zeros_like(c_scr)
        n_scr[...] = jnp.zeros_like(n_scr)
        m_scr[...] = jnp.zeros_like(m_scr)
        tq_scr[...] = jnp.zeros_like(tq_scr)
        tk_scr[...] = jnp.zeros_like(tk_scr)

    q_raw = q_ref[...]
    k_raw = k_ref[...]
    q_all = _causal_conv_silu(q_raw, tq_scr[...], conv_ref[:, 0:qk_w]) * (dqk ** -0.5)
    k_all = _causal_conv_silu(k_raw, tk_scr[...], conv_ref[:, qk_w:2 * qk_w])
    tq_scr[...] = q_raw[L - SUBLANES:L]
    tk_scr[...] = k_raw[L - SUBLANES:L]

    gc = gc_ref[...] + bc_ref[...]
    gr = gr_ref[0] + br_ref[...]
    t_idx = lax.broadcasted_iota(jnp.int32, (L, L), 0)
    s_idx = lax.broadcasted_iota(jnp.int32, (L, L), 1)
    causal = s_idx <= t_idx
    tri = causal.astype(F32)
    bc_all = jnp.dot(tri, _log_sigmoid(gc), precision=lax.Precision.HIGHEST, preferred_element_type=F32)
    br_all = lax.dot_general(_log_sigmoid(gr), tri, (((1,), (1,)), ((), ())),
                             precision=lax.Precision.HIGHEST, preferred_element_type=F32)

    for hd in range(heads):
        qs = slice(hd * dqk, (hd + 1) * dqk)
        vs = slice(hd * dv, (hd + 1) * dv)
        q = q_all[:, qs]
        k = k_all[:, qs]
        qb = q.astype(BF16)
        kb = k.astype(BF16)
        vb = v_ref[:, vs].astype(BF16)
        b_c = bc_all[:, heads + hd:heads + hd + 1]
        li_c = gc[:, hd:hd + 1]
        b_r = br_all[heads + hd:heads + hd + 1, :]
        li_r = gr[hd:hd + 1, :]
        m_prev = m_scr[hd:hd + 1, 0:1]
        n_prev = n_scr[hd:hd + 1, :]

        d = jnp.where(causal, b_c - b_r + li_r, -jnp.inf)
        m_inter = b_c + m_prev
        m_t = jnp.maximum(m_inter, jnp.max(d, axis=1, keepdims=True))
        w_intra = jnp.exp(d - m_t)
        w_inter = jnp.exp(m_inter - m_t)
        s = _dot_nt(qb, kb) * w_intra
        num = _dot(s.astype(BF16), vb) + w_inter * _dot(qb, c_scr[hd].astype(BF16))
        den = jnp.sum(s, axis=1, keepdims=True) + w_inter * jnp.sum(q * n_prev, axis=1, keepdims=True)
        hh = num * (1.0 / jnp.maximum(jnp.abs(den), jnp.exp(-m_t)))
        out = _rms(hh) * norm_ref[:, vs] * _sigmoid(og_ref[:, vs])
        out_ref[:, vs] = out.astype(out_ref.dtype)

        g = b_c[L - 1:L, :]
        a_c = g - b_c + li_c
        m_new = jnp.maximum(g + m_prev, jnp.max(a_c, axis=0, keepdims=True))
        decay = jnp.exp(g + m_prev - m_new)
        kw = k * jnp.exp(a_c - m_new)
        c_scr[hd] = decay * c_scr[hd] + _dot_tn(kw.astype(BF16), vb)
        n_scr[hd:hd + 1, :] = decay * n_prev + jnp.sum(kw, axis=0, keepdims=True)
        m_scr[hd:hd + 1, :] = jnp.broadcast_to(m_new, (1, LANES))


def mlstm_scan(proj, gates_col, gates_row, gate_bias, conv_w, head_norm, batch, *, chunk, heads=M_HEADS,
               dqk=M_DQK, dv=M_DV):
    rows = proj.shape[0]
    s_len = rows // batch
    L = min(chunk, s_len)
    nc = s_len // L
    qk_w = heads * dqk
    v_w = heads * dv
    assert qk_w % LANES == 0 and v_w % qk_w == 0
    bias_col = gate_bias.reshape(1, 2 * heads)
    bias_row = gate_bias.reshape(2 * heads, 1)
    kern = functools.partial(_mlstm_kernel, heads=heads, dqk=dqk, dv=dv)
    return pl.pallas_call(
        kern,
        grid=(batch, nc),
        in_specs=[
            pl.BlockSpec((L, qk_w), lambda b, c: (b * nc + c, 0)),
            pl.BlockSpec((L, qk_w), lambda b, c: (b * nc + c, 1)),
            pl.BlockSpec((L, v_w), lambda b, c: (b * nc + c, 2 * qk_w // v_w)),
            pl.BlockSpec((L, v_w), lambda b, c: (b * nc + c, 2 * qk_w // v_w + 1)),
            pl.BlockSpec((L, 2 * heads), lambda b, c: (b * nc + c, 0)),
            pl.BlockSpec((1, 2 * heads, L), lambda b, c: (b, 0, c)),
            pl.BlockSpec((1, 2 * heads), lambda b, c: (0, 0)),
            pl.BlockSpec((2 * heads, 1), lambda b, c: (0, 0)),
            pl.BlockSpec((CONV_K, 2 * qk_w), lambda b, c: (0, 0)),
            pl.BlockSpec((1, v_w), lambda b, c: (0, 0)),
        ],
        out_specs=pl.BlockSpec((L, v_w), lambda b, c: (b * nc + c, 0)),
        out_shape=jax.ShapeDtypeStruct((rows, v_w), BF16),
        scratch_shapes=[
            pltpu.VMEM((heads, dqk, dv), F32),
            pltpu.VMEM((heads, dqk), F32),
            pltpu.VMEM((heads, LANES), F32),
            pltpu.VMEM((SUBLANES, qk_w), F32),
            pltpu.VMEM((SUBLANES, qk_w), F32),
        ],
        compiler_params=_cparams(("parallel", "arbitrary")),
        name="mlstm_scan",
    )(proj, proj, proj, proj, gates_col, gates_row, bias_col, bias_row, conv_w, head_norm.reshape(1, v_w))


def _hgrn_kernel(q_ref, f_ref, i_ref, g_ref, lb_ref, norm_ref, out_ref, st_scr, *, heads, dk, dv, layer):
    L = q_ref.shape[0]

    @pl.when(pl.program_id(1) == 0)
    def _():
        st_scr[...] = jnp.zeros_like(st_scr)

    lbp = lb_ref[...]
    e = jnp.exp(lbp - jnp.max(lbp, axis=0, keepdims=True))
    p = e / jnp.sum(e, axis=0, keepdims=True)
    lb = jnp.zeros((1, lbp.shape[1]), F32)
    for j in range(1, layer + 1):
        lb = lb + p[j:j + 1]

    f = lb + (1.0 - lb) * _sigmoid(f_ref[...])
    lg = jnp.log(f)
    k_all = 1.0 - f
    t_idx = lax.broadcasted_iota(jnp.int32, (L, L), 0)
    s_idx = lax.broadcasted_iota(jnp.int32, (L, L), 1)
    tri = (s_idx <= t_idx).astype(F32)
    b_all = jnp.dot(tri, lg, precision=lax.Precision.HIGHEST, preferred_element_type=F32)
    level = jnp.where(t_idx > s_idx, 31 - lax.clz(t_idx ^ s_idx), jnp.where(t_idx == s_idx, -1, -2))
    row = lax.broadcasted_iota(jnp.int32, (L, 1), 0)
    n_levels = int(math.log2(L))
    k_slices = [slice(hd * dk, (hd + 1) * dk) for hd in range(heads)]
    q_all = q_ref[...]
    qb_all = q_all.astype(BF16)

    def scores(qx, kx, lw, acc):
        keep = level == lw
        return [jnp.where(keep, _dot_nt(qx[:, sl], kx[:, sl]), acc[i]) for i, sl in enumerate(k_slices)]

    a = scores(qb_all, k_all.astype(BF16), -1, [0.0] * heads)
    a = scores(qb_all, (k_all * pltpu.roll(f, L - 1, 0)).astype(BF16), 0, a)
    fw = jnp.where((row & 1) == 1, pltpu.roll(b_all, 1, 0), b_all)
    for lw in range(1, n_levels):
        w = 1 << lw
        odd = ((row >> lw) & 1) == 1
        gw = pltpu.roll(fw, L - w, 0)
        e = jnp.exp(jnp.where(odd, b_all, gw) - jnp.where(odd, fw, b_all))
        a = scores((q_all * e).astype(BF16), (k_all * e).astype(BF16), lw, a)
        fw = jnp.where(odd, pltpu.roll(fw, w, 0), fw)

    g = b_all[L - 1:L, :]
    qe_all = (q_all * jnp.exp(b_all)).astype(BF16)
    ke_all = (k_all * jnp.exp(g - b_all)).astype(BF16)
    eg = jnp.exp(g)
    ib_all = i_ref[...].astype(BF16)
    for hd in range(heads):
        ks = k_slices[hd]
        vs = slice(hd * dv, (hd + 1) * dv)
        st = st_scr[hd]
        o = _dot(a[hd].astype(BF16), ib_all[:, vs]) + _dot_nt(qe_all[:, ks], st.astype(BF16))
        st_scr[hd] = st * eg[:, ks] + _dot_tn(ib_all[:, vs], ke_all[:, ks])
        out = _rms(o) * norm_ref[:, vs] * _silu(g_ref[:, vs])
        out_ref[:, vs] = out.astype(out_ref.dtype)


def hgrn_scan(proj, col_block0, lower_bounds, head_norm, batch, layer, *, chunk, heads=H_HEADS, dk=H_DK, dv=H_DV):
    rows = proj.shape[0]
    s_len = rows // batch
    L = min(chunk, s_len)
    nc = s_len // L
    assert L & (L - 1) == 0 and dk == dv
    w = heads * dk
    depth = lower_bounds.shape[0]
    kern = functools.partial(_hgrn_kernel, heads=heads, dk=dk, dv=dv, layer=layer)
    blk = lambda o: pl.BlockSpec((L, w), functools.partial(lambda b, c, o: (b * nc + c, o), o=col_block0 + o))
    return pl.pallas_call(
        kern,
        grid=(batch, nc),
        in_specs=[blk(0), blk(1), blk(2), blk(3),
                  pl.BlockSpec((depth, w), lambda b, c: (0, 0)),
                  pl.BlockSpec((1, w), lambda b, c: (0, 0))],
        out_specs=pl.BlockSpec((L, w), lambda b, c: (b * nc + c, 0)),
        out_shape=jax.ShapeDtypeStruct((rows, w), BF16),
        scratch_shapes=[pltpu.VMEM((heads, dv, dk), F32)],
        compiler_params=_cparams(("parallel", "arbitrary")),
        name="hgrn_scan",
    )(proj, proj, proj, proj, lower_bounds, head_norm.reshape(1, w))


def _ffn(u, w_gate_up, w_down, layer):
    d_ff, d = w_down.shape[1:]
    act = matmul(u, RawWeight(w_gate_up, layer, FFN_UP_TN), d_ff, tm=MM_TM, w_col_offsets=(0, d_ff),
                 epilogue=_epi_swiglu, out_dtype=BF16, name="ffn_gate_up")
    return matmul(act, cast_tiled_bf16(w_down, layer, MM_TN), d, tm=FFN_DOWN_TM, out_dtype=BF16, name="ffn_down")


def _token_mixers(u, layer, batch, w_in, gate_bias, conv_w, m_norm, lower_bounds, h_norm,
                  w_bm, w_bh, w_gate, b_gate, w_out):
    rows, d = u.shape
    s_len = rows // batch
    gate_lo = 2 * M_HEADS * M_DQK + 2 * M_HEADS * M_DV
    gate_hi = gate_lo + 2 * M_HEADS
    w_in_t = jnp.swapaxes(w_in, 1, 2)
    proj_m = matmul(u, RawWeightT(w_in_t, layer, MM_TN, 0), gate_lo, tm=MM_TM, name="mix_in_proj_mlstm")
    proj_h = matmul(u, RawWeightT(w_in_t, layer, MM_TN, gate_hi), w_in.shape[2] - gate_hi, tm=MM_TM,
                    name="mix_in_proj_hgrn")
    gates_col = matmul(u, RawWeightT(w_in_t, layer, 2 * M_HEADS, gate_lo), 2 * M_HEADS, tm=MM_TM,
                       name="mix_gate_proj")
    gates_row = gates_col.reshape(batch, s_len, 2 * M_HEADS).transpose(0, 2, 1)
    ym = mlstm_scan(proj_m, gates_col, gates_row, gate_bias, conv_w, m_norm, batch, chunk=MLSTM_CHUNK)
    yh = hgrn_scan(proj_h, 0, lower_bounds, h_norm, batch, layer, chunk=HGRN_CHUNK)
    gate = matmul(u, RawWeight(w_gate, layer, MM_TN), 2 * d, tm=MM_TM, epilogue=_epi_sigmoid_bias,
                  extras=((b_gate.reshape(1, 2 * d), 0),), out_dtype=BF16, name="merge_gate")
    y_m = matmul(ym, RawWeight(w_bm, layer, MM_WIDE_TN), d, tm=MM_TM, name="branch_mlstm")
    z = matmul(yh, RawWeight(w_bh, layer, MM_WIDE_TN), d, tm=MM_TM, epilogue=_epi_merge,
               extras=((gate, 0), (gate, d), (y_m, 0)), out_dtype=BF16, name="branch_hgrn_merge")
    return matmul(z, RawWeight(w_out, layer, MM_TN), d, tm=MM_TM, out_dtype=BF16, name="mix_out")


def _xattn(u, mem2, mem_norm, batch, layer, w_q, w_kv, w_o):
    memn = rmsnorm_bf16(mem2, mem_norm)
    kv = matmul(memn, RawWeight(w_kv, layer, MM_TN), w_kv.shape[2], tm=MM_TM, out_dtype=BF16, name="xattn_kv")
    q = matmul(u, RawWeight(w_q, layer, MM_TN), w_q.shape[2], tm=MM_TM, out_dtype=BF16, name="xattn_q")
    o = xattn_core(q, kv, batch, X_HEADS, X_DH)
    return matmul(o, RawWeight(w_o, layer, MM_WIDE_TN), w_o.shape[2], tm=MM_TM, out_dtype=BF16, name="xattn_o")


def kernel(x, mem, ffn1_pre_norm, ffn1_w_gate_up, ffn1_w_down, ffn1_post_norm, mix_pre_norm, mix_w_in, mlstm_gate_bias, mlstm_conv, mlstm_head_norm, hgrn_lower_bounds, hgrn_head_norm, branch_w_mlstm, branch_w_hgrn, merge_w_gate, merge_b_gate, mix_w_out, mix_post_norm, xattn_pre_norm, xattn_mem_norm, xattn_w_q, xattn_w_kv, xattn_w_o, xattn_post_norm, ffn2_pre_norm, ffn2_w_gate_up, ffn2_w_down, ffn2_post_norm):
    batch, s_len, d = x.shape
    n_mem = mem.shape[1]
    depth = ffn1_pre_norm.shape[0]

    h = x.reshape(batch * s_len, d)
    mem2 = mem.reshape(batch * n_mem, d)
    u = rmsnorm_bf16(h, ffn1_pre_norm[0])
    for l in range(depth):
        y = _ffn(u, ffn1_w_gate_up, ffn1_w_down, l)
        h, u = combine(y, h, ffn1_post_norm[l], 0.5, mix_pre_norm[l])

        y = _token_mixers(u, l, batch, mix_w_in, mlstm_gate_bias[l], mlstm_conv[l], mlstm_head_norm[l],
                          hgrn_lower_bounds, hgrn_head_norm[l], branch_w_mlstm, branch_w_hgrn, merge_w_gate,
                          merge_b_gate[l], mix_w_out)
        h, u = combine(y, h, mix_post_norm[l], 1.0, xattn_pre_norm[l])

        y = _xattn(u, mem2, xattn_mem_norm[l], batch, l, xattn_w_q, xattn_w_kv, xattn_w_o)
        h, u = combine(y, h, xattn_post_norm[l], 1.0, ffn2_pre_norm[l])

        y = _ffn(u, ffn2_w_gate_up, ffn2_w_down, l)
        nxt = ffn1_pre_norm[l + 1] if l + 1 < depth else None
        h, u = combine(y, h, ffn2_post_norm[l], 0.5, nxt)
    return h.reshape(batch, s_len, d)
```

```python
import functools
import math
from typing import NamedTuple

import jax
import jax.numpy as jnp
from jax import lax
from jax.experimental import pallas as pl
from jax.experimental.pallas import tpu as pltpu

EPS = 1e-6
F32 = jnp.float32
BF16 = jnp.bfloat16

V7X_VMEM_BYTES = 64 * 1024 * 1024
VMEM_LIMIT = V7X_VMEM_BYTES - 8 * 1024 * 1024
LANES = 128
SUBLANES = 8

CONV_K = 4
M_HEADS = 8
M_DQK = 128
M_DV = 256
H_HEADS = 16
H_DK = 128
H_DV = 128
X_HEADS = 4
X_DH = 256
MLSTM_CHUNK = 256
HGRN_CHUNK = 64
CAST_BLOCK_ELEMS = 2 * 1024 * 1024
MM_TM = 1024
MM_TALL_TM = 2048
MAX_DOUBLE_BUFFERED_PANEL_BYTES = 24 * 1024 * 1024
MM_TN = 512
MM_WIDE_TN = 1024
FFN_UP_TN = 256
FFN_DOWN_TM = 512


def _cparams(sem):
    return pltpu.CompilerParams(dimension_semantics=sem, vmem_limit_bytes=VMEM_LIMIT)


def _dot(a, b):
    return jnp.dot(a, b, preferred_element_type=F32)


def _dot_nt(a, b):
    return lax.dot_general(a, b, (((1,), (1,)), ((), ())), preferred_element_type=F32)


def _dot_tn(a, b):
    return lax.dot_general(a, b, (((0,), (0,)), ((), ())), preferred_element_type=F32)


def _sigmoid(x):
    return 1.0 / (1.0 + jnp.exp(-x))


def _silu(x):
    return x * _sigmoid(x)


def _log_sigmoid(x):
    return jnp.minimum(x, 0.0) - jnp.log(1.0 + jnp.exp(-jnp.abs(x)))


def _rms(x):
    return x * lax.rsqrt(jnp.mean(x * x, axis=-1, keepdims=True) + EPS)


def _rmsnorm_kernel(x_ref, g_ref, o_ref):
    o_ref[...] = (_rms(x_ref[...]) * g_ref[...]).astype(o_ref.dtype)


def rmsnorm_bf16(x, g, tm=256):
    m, d = x.shape
    tm = min(tm, m)
    return pl.pallas_call(
        _rmsnorm_kernel,
        grid=(m // tm,),
        in_specs=[pl.BlockSpec((tm, d), lambda i: (i, 0)), pl.BlockSpec((1, d), lambda i: (0, 0))],
        out_specs=pl.BlockSpec((tm, d), lambda i: (i, 0)),
        out_shape=jax.ShapeDtypeStruct((m, d), BF16),
        compiler_params=_cparams(("parallel",)),
        name="rmsnorm_bf16",
    )(x, g.reshape(1, d))


def _cast_kernel(x_ref, o_ref):
    o_ref[...] = x_ref[...].astype(o_ref.dtype)


def _largest_divisor(n, cap, multiple):
    best = multiple
    for c in range(multiple, min(n, cap) + 1, multiple):
        if n % c == 0:
            best = c
    return best


def cast_tiled_bf16(w3, layer, tn):
    _, k, n = w3.shape
    tn = min(tn, n)
    assert n % tn == 0
    tk = _largest_divisor(k, max(SUBLANES, CAST_BLOCK_ELEMS // tn), 2 * SUBLANES)
    return pl.pallas_call(
        _cast_kernel,
        grid=(n // tn, k // tk),
        in_specs=[pl.BlockSpec((None, tk, tn), lambda j, i: (layer, i, j))],
        out_specs=pl.BlockSpec((None, tk, tn), lambda j, i: (j, i, 0)),
        out_shape=jax.ShapeDtypeStruct((n // tn, k, tn), BF16),
        compiler_params=_cparams(("parallel", "parallel")),
        name="cast_tiled_bf16",
    )(w3)


def _mm_kernel(*refs, n_w, epilogue, transposed):
    a_ref = refs[0]
    w_refs = refs[1:1 + n_w]
    extra_refs = refs[1 + n_w:-1]
    o_ref = refs[-1]
    a = a_ref[...]
    if transposed:
        accs = [_dot_nt(a, w[0].astype(BF16)) for w in w_refs]
    else:
        accs = [_dot(a, w[...].astype(BF16)) for w in w_refs]
    o_ref[...] = epilogue(accs, [e[...] for e in extra_refs]).astype(o_ref.dtype)


class RawWeight(NamedTuple):
    w3: jax.Array
    layer: int
    tn: int


class RawWeightT(NamedTuple):
    w3t: jax.Array
    layer: int
    tn: int
    row0: int


def matmul(a, w, n_out, *, tm, w_col_offsets=(0,), epilogue=None, extras=(), out_dtype=F32, name="matmul"):
    m, k = a.shape
    tm = min(tm, m)
    if epilogue is None:
        epilogue = lambda accs, ex: accs[0]
    a_mode = pl.Buffered(1) if 2 * tm * k * a.dtype.itemsize > MAX_DOUBLE_BUFFERED_PANEL_BYTES else None
    in_specs = [pl.BlockSpec((tm, k), lambda i, j: (i, 0), pipeline_mode=a_mode)]
    args = [a]
    transposed = isinstance(w, RawWeightT)
    tn = w.shape[2] if isinstance(w, jax.Array) else w.tn
    assert m % tm == 0 and n_out % tn == 0
    for off in w_col_offsets:
        assert off % tn == 0
        if transposed:
            assert w.w3t.shape[2] == k and w.row0 % SUBLANES == 0
            spec = pl.BlockSpec((pl.Element(1), pl.Element(tn), pl.Element(k)),
                                functools.partial(lambda i, j, o: (w.layer, pl.multiple_of(o + j * tn, SUBLANES), 0),
                                                  o=w.row0 + off))
        elif isinstance(w, RawWeight):
            assert w.w3.shape[1] == k
            spec = pl.BlockSpec((None, k, tn), functools.partial(lambda i, j, o: (w.layer, 0, j + o), o=off // tn))
        else:
            assert w.shape[1] == k
            spec = pl.BlockSpec((None, k, tn), functools.partial(lambda i, j, o: (j + o, 0, 0), o=off // tn))
        in_specs.append(spec)
        args.append(w if isinstance(w, jax.Array) else w[0])
    for arr, off in extras:
        assert off % tn == 0
        rows = arr.shape[0]
        if rows == 1:
            in_specs.append(pl.BlockSpec((1, tn), functools.partial(lambda i, j, o: (0, j + o), o=off // tn)))
        else:
            in_specs.append(pl.BlockSpec((tm, tn), functools.partial(lambda i, j, o: (i, j + o), o=off // tn)))
        args.append(arr)
    return pl.pallas_call(
        functools.partial(_mm_kernel, n_w=len(w_col_offsets), epilogue=epilogue, transposed=transposed),
        grid=(m // tm, n_out // tn),
        in_specs=in_specs,
        out_specs=pl.BlockSpec((tm, tn), lambda i, j: (i, j)),
        out_shape=jax.ShapeDtypeStruct((m, n_out), out_dtype),
        compiler_params=_cparams(("parallel", "arbitrary")),
        name=name,
    )(*args)


def _epi_swiglu(accs, ex):
    return _silu(accs[0]) * accs[1]


def _epi_sigmoid_bias(accs, ex):
    return _sigmoid(accs[0] + ex[0])


def _epi_merge(accs, ex):
    gm, gh, ym = ex
    return gm.astype(F32) * ym + gh.astype(F32) * accs[0]


def _combine_kernel(y_ref, h_ref, post_ref, pre_ref, hn_ref, u_ref, *, scale):
    hn = h_ref[...] + scale * (_rms(y_ref[...].astype(F32)) * post_ref[...])
    hn_ref[...] = hn
    u_ref[...] = (_rms(hn) * pre_ref[...]).astype(u_ref.dtype)


def _combine_last_kernel(y_ref, h_ref, post_ref, hn_ref, *, scale):
    hn_ref[...] = h_ref[...] + scale * (_rms(y_ref[...].astype(F32)) * post_ref[...])


def combine(y, h, post, scale, next_pre=None, tm=256):
    m, d = y.shape
    tm = min(tm, m)
    row = pl.BlockSpec((tm, d), lambda i: (i, 0))
    vec = pl.BlockSpec((1, d), lambda i: (0, 0))
    if next_pre is None:
        return pl.pallas_call(
            functools.partial(_combine_last_kernel, scale=scale),
            grid=(m // tm,),
            in_specs=[row, row, vec],
            out_specs=row,
            out_shape=jax.ShapeDtypeStruct((m, d), F32),
            compiler_params=_cparams(("parallel",)),
            name="combine_last",
        )(y, h, post.reshape(1, d)), None
    return pl.pallas_call(
        functools.partial(_combine_kernel, scale=scale),
        grid=(m // tm,),
        in_specs=[row, row, vec, vec],
        out_specs=[row, row],
        out_shape=[jax.ShapeDtypeStruct((m, d), F32), jax.ShapeDtypeStruct((m, d), BF16)],
        compiler_params=_cparams(("parallel",)),
        name="combine",
    )(y, h, post.reshape(1, d), next_pre.reshape(1, d))


def _xattn_kernel(q_ref, k_ref, v_ref, o_ref, *, heads, dh):
    scale = dh ** -0.5
    for hd in range(heads):
        sl = slice(hd * dh, (hd + 1) * dh)
        s = _dot_nt(q_ref[:, sl], k_ref[:, sl]) * scale
        s = s - jnp.max(s, axis=-1, keepdims=True)
        p = jnp.exp(s)
        p = p / jnp.sum(p, axis=-1, keepdims=True)
        o_ref[:, sl] = _dot(p.astype(BF16), v_ref[:, sl]).astype(o_ref.dtype)


def _xattn_out_kernel(q_ref, k_ref, v_ref, wo_ref, h_ref, post_ref, pre_ref, hn_ref, u_ref, o_scr, *, heads, dh):
    _xattn_kernel(q_ref, k_ref, v_ref, o_scr, heads=heads, dh=dh)
    y = _dot(o_scr[...], wo_ref[...])
    hn = h_ref[...] + _rms(y) * post_ref[...]
    hn_ref[...] = hn
    u_ref[...] = (_rms(hn) * pre_ref[...]).astype(u_ref.dtype)


def xattn_out(q, kv, w_o, h, post, next_pre, batch, heads, dh, tm=256):
    rows, width = q.shape
    d = h.shape[1]
    s_len = rows // batch
    n_mem = kv.shape[0] // batch
    tm = min(tm, s_len)
    nt = s_len // tm
    row = pl.BlockSpec((tm, d), lambda b, i: (b * nt + i, 0))
    vec = pl.BlockSpec((1, d), lambda b, i: (0, 0))
    return pl.pallas_call(
        functools.partial(_xattn_out_kernel, heads=heads, dh=dh),
        grid=(batch, nt),
        in_specs=[
            pl.BlockSpec((tm, width), lambda b, i: (b * nt + i, 0)),
            pl.BlockSpec((n_mem, width), lambda b, i: (b, 0)),
            pl.BlockSpec((n_mem, width), lambda b, i: (b, 1)),
            pl.BlockSpec((None, width, d), lambda b, i: (0, 0, 0), pipeline_mode=pl.Buffered(1)),
            row, vec, vec,
        ],
        out_specs=[row, row],
        out_shape=[jax.ShapeDtypeStruct((rows, d), F32), jax.ShapeDtypeStruct((rows, d), BF16)],
        scratch_shapes=[pltpu.VMEM((tm, width), BF16)],
        compiler_params=_cparams(("parallel", "arbitrary")),
        name="xattn_out",
    )(q, kv, kv, w_o, h, post.reshape(1, d), next_pre.reshape(1, d))


def _causal_conv_silu(x, tail, w):
    rows8 = lax.broadcasted_iota(jnp.int32, (SUBLANES, 1), 0)
    y = x * w[CONV_K - 1:CONV_K]
    for d in range(1, CONV_K):
        xs = pltpu.roll(x, d, 0)
        top = jnp.where(rows8 < d, pltpu.roll(tail, d, 0), xs[0:SUBLANES])
        xs = jnp.concatenate([top, xs[SUBLANES:]], axis=0)
        y = y + xs * w[CONV_K - 1 - d:CONV_K - d]
    return _silu(y)


def _mlstm_kernel(q_ref, k_ref, v_ref, og_ref, gc_ref, gr_ref, bc_ref, br_ref, conv_ref, norm_ref, out_ref,
                  c_scr, n_scr, m_scr, tq_scr, tk_scr, *, heads, dqk, dv):
    L = q_ref.shape[0]
    qk_w = heads * dqk

    @pl.when(pl.program_id(1) == 0)
    def _():
        c_scr[...] = jnp.zeros_like(c_scr)
        n_scr[...] = jnp.zeros_like(n_scr)
        m_scr[...] = jnp.zeros_like(m_scr)
        tq_scr[...] = jnp.zeros_like(tq_scr)
        tk_scr[...] = jnp.zeros_like(tk_scr)

    q_raw = q_ref[...]
    k_raw = k_ref[...]
    q_all = _causal_conv_silu(q_raw, tq_scr[...], conv_ref[:, 0:qk_w]) * (dqk ** -0.5)
    k_all = _causal_conv_silu(k_raw, tk_scr[...], conv_ref[:, qk_w:2 * qk_w])
    tq_scr[...] = q_raw[L - SUBLANES:L]
    tk_scr[...] = k_raw[L - SUBLANES:L]

    gc = gc_ref[...] + bc_ref[...]
    gr = gr_ref[0] + br_ref[...]
    t_idx = lax.broadcasted_iota(jnp.int32, (L, L), 0)
    s_idx = lax.broadcasted_iota(jnp.int32, (L, L), 1)
    causal = s_idx <= t_idx
    tri = causal.astype(F32)
    bc_all = jnp.dot(tri, _log_sigmoid(gc), precision=lax.Precision.HIGHEST, preferred_element_type=F32)
    br_all = lax.dot_general(_log_sigmoid(gr), tri, (((1,), (1,)), ((), ())),
                             precision=lax.Precision.HIGHEST, preferred_element_type=F32)

    for hd in range(heads):
        qs = slice(hd * dqk, (hd + 1) * dqk)
        vs = slice(hd * dv, (hd + 1) * dv)
        q = q_all[:, qs]
        k = k_all[:, qs]
        qb = q.astype(BF16)
        kb = k.astype(BF16)
        vb = v_ref[:, vs].astype(BF16)
        b_c = bc_all[:, heads + hd:heads + hd + 1]
        li_c = gc[:, hd:hd + 1]
        b_r = br_all[heads + hd:heads + hd + 1, :]
        li_r = gr[hd:hd + 1, :]
        m_prev = m_scr[hd:hd + 1, 0:1]
        n_prev = n_scr[hd:hd + 1, :]

        d = jnp.where(causal, b_c - b_r + li_r, -jnp.inf)
        m_inter = b_c + m_prev
        m_t = jnp.maximum(m_inter, jnp.max(d, axis=1, keepdims=True))
        w_intra = jnp.exp(d - m_t)
        w_inter = jnp.exp(m_inter - m_t)
        s = _dot_nt(qb, kb) * w_intra
        num = _dot(s.astype(BF16), vb) + w_inter * _dot(qb, c_scr[hd].astype(BF16))
        den = jnp.sum(s, axis=1, keepdims=True) + w_inter * jnp.sum(q * n_prev, axis=1, keepdims=True)
        hh = num * (1.0 / jnp.maximum(jnp.abs(den), jnp.exp(-m_t)))
        out = _rms(hh) * norm_ref[:, vs] * _sigmoid(og_ref[:, vs])
        out_ref[:, vs] = out.astype(out_ref.dtype)

        g = b_c[L - 1:L, :]
        a_c = g - b_c + li_c
        m_new = jnp.maximum(g + m_prev, jnp.max(a_c, axis=0, keepdims=True))
        decay = jnp.exp(g + m_prev - m_new)
        kw = k * jnp.exp(a_c - m_new)
        c_scr[hd] = decay * c_scr[hd] + _dot_tn(kw.astype(BF16), vb)
        n_scr[hd:hd + 1, :] = decay * n_prev + jnp.sum(kw, axis=0, keepdims=True)
        m_scr[hd:hd + 1, :] = jnp.broadcast_to(m_new, (1, LANES))


def mlstm_scan(proj, gates_col, gates_row, gate_bias, conv_w, head_norm, batch, *, chunk, heads=M_HEADS,
               dqk=M_DQK, dv=M_DV):
    rows = proj.shape[0]
    s_len = rows // batch
    L = min(chunk, s_len)
    nc = s_len // L
    qk_w = heads * dqk
    v_w = heads * dv
    assert qk_w % LANES == 0 and v_w % qk_w == 0
    bias_col = gate_bias.reshape(1, 2 * heads)
    bias_row = gate_bias.reshape(2 * heads, 1)
    kern = functools.partial(_mlstm_kernel, heads=heads, dqk=dqk, dv=dv)
    return pl.pallas_call(
        kern,
        grid=(batch, nc),
        in_specs=[
            pl.BlockSpec((L, qk_w), lambda b, c: (b * nc + c, 0)),
            pl.BlockSpec((L, qk_w), lambda b, c: (b * nc + c, 1)),
            pl.BlockSpec((L, v_w), lambda b, c: (b * nc + c, 2 * qk_w // v_w)),
            pl.BlockSpec((L, v_w), lambda b, c: (b * nc + c, 2 * qk_w // v_w + 1)),
            pl.BlockSpec((L, 2 * heads), lambda b, c: (b * nc + c, 0)),
            pl.BlockSpec((1, 2 * heads, L), lambda b, c: (b, 0, c)),
            pl.BlockSpec((1, 2 * heads), lambda b, c: (0, 0)),
            pl.BlockSpec((2 * heads, 1), lambda b, c: (0, 0)),
            pl.BlockSpec((CONV_K, 2 * qk_w), lambda b, c: (0, 0)),
            pl.BlockSpec((1, v_w), lambda b, c: (0, 0)),
        ],
        out_specs=pl.BlockSpec((L, v_w), lambda b, c: (b * nc + c, 0)),
        out_shape=jax.ShapeDtypeStruct((rows, v_w), BF16),
        scratch_shapes=[
            pltpu.VMEM((heads, dqk, dv), F32),
            pltpu.VMEM((heads, dqk), F32),
            pltpu.VMEM((heads, LANES), F32),
            pltpu.VMEM((SUBLANES, qk_w), F32),
            pltpu.VMEM((SUBLANES, qk_w), F32),
        ],
        compiler_params=_cparams(("parallel", "arbitrary")),
        name="mlstm_scan",
    )(proj, proj, proj, proj, gates_col, gates_row, bias_col, bias_row, conv_w, head_norm.reshape(1, v_w))


def _hgrn_kernel(q_ref, f_ref, i_ref, g_ref, lb_ref, norm_ref, out_ref, st_scr, *, heads, dk, dv, layer):
    L = q_ref.shape[0]

    @pl.when(pl.program_id(1) == 0)
    def _():
        st_scr[...] = jnp.zeros_like(st_scr)

    lbp = lb_ref[...]
    e = jnp.exp(lbp - jnp.max(lbp, axis=0, keepdims=True))
    p = e / jnp.sum(e, axis=0, keepdims=True)
    lb = jnp.zeros((1, lbp.shape[1]), F32)
    for j in range(1, layer + 1):
        lb = lb + p[j:j + 1]

    f = lb + (1.0 - lb) * _sigmoid(f_ref[...])
    lg = jnp.log(f)
    k_all = 1.0 - f
    t_idx = lax.broadcasted_iota(jnp.int32, (L, L), 0)
    s_idx = lax.broadcasted_iota(jnp.int32, (L, L), 1)
    tri = (s_idx <= t_idx).astype(F32)
    b_all = jnp.dot(tri, lg, precision=lax.Precision.HIGHEST, preferred_element_type=F32)
    level = jnp.where(t_idx > s_idx, 31 - lax.clz(t_idx ^ s_idx), jnp.where(t_idx == s_idx, -1, -2))
    row = lax.broadcasted_iota(jnp.int32, (L, 1), 0)
    n_levels = int(math.log2(L))
    k_slices = [slice(hd * dk, (hd + 1) * dk) for hd in range(heads)]
    q_all = q_ref[...]
    qb_all = q_all.astype(BF16)

    def scores(qx, kx, lw, acc):
        keep = level == lw
        return [jnp.where(keep, _dot_nt(qx[:, sl], kx[:, sl]), acc[i]) for i, sl in enumerate(k_slices)]

    a = scores(qb_all, k_all.astype(BF16), -1, [0.0] * heads)
    a = scores(qb_all, (k_all * pltpu.roll(f, L - 1, 0)).astype(BF16), 0, a)
    fw = jnp.where((row & 1) == 1, pltpu.roll(b_all, 1, 0), b_all)
    for lw in range(1, n_levels):
        w = 1 << lw
        odd = ((row >> lw) & 1) == 1
        gw = pltpu.roll(fw, L - w, 0)
        e = jnp.exp(jnp.where(odd, b_all, gw) - jnp.where(odd, fw, b_all))
        a = scores((q_all * e).astype(BF16), (k_all * e).astype(BF16), lw, a)
        fw = jnp.where(odd, pltpu.roll(fw, w, 0), fw)

    g = b_all[L - 1:L, :]
    qe_all = (q_all * jnp.exp(b_all)).astype(BF16)
    ke_all = (k_all * jnp.exp(g - b_all)).astype(BF16)
    eg = jnp.exp(g)
    ib_all = i_ref[...].astype(BF16)
    for hd in range(heads):
        ks = k_slices[hd]
        vs = slice(hd * dv, (hd + 1) * dv)
        st = st_scr[hd]
        o = _dot(a[hd].astype(BF16), ib_all[:, vs]) + _dot_nt(qe_all[:, ks], st.astype(BF16))
        st_scr[hd] = st * eg[:, ks] + _dot_tn(ib_all[:, vs], ke_all[:, ks])
        out = _rms(o) * norm_ref[:, vs] * _silu(g_ref[:, vs])
        out_ref[:, vs] = out.astype(out_ref.dtype)


def hgrn_scan(proj, col_block0, lower_bounds, head_norm, batch, layer, *, chunk, heads=H_HEADS, dk=H_DK, dv=H_DV):
    rows = proj.shape[0]
    s_len = rows // batch
    L = min(chunk, s_len)
    nc = s_len // L
    assert L & (L - 1) == 0 and dk == dv
    w = heads * dk
    depth = lower_bounds.shape[0]
    kern = functools.partial(_hgrn_kernel, heads=heads, dk=dk, dv=dv, layer=layer)
    blk = lambda o: pl.BlockSpec((L, w), functools.partial(lambda b, c, o: (b * nc + c, o), o=col_block0 + o))
    return pl.pallas_call(
        kern,
        grid=(batch, nc),
        in_specs=[blk(0), blk(1), blk(2), blk(3),
                  pl.BlockSpec((depth, w), lambda b, c: (0, 0)),
                  pl.BlockSpec((1, w), lambda b, c: (0, 0))],
        out_specs=pl.BlockSpec((L, w), lambda b, c: (b * nc + c, 0)),
        out_shape=jax.ShapeDtypeStruct((rows, w), BF16),
        scratch_shapes=[pltpu.VMEM((heads, dv, dk), F32)],
        compiler_params=_cparams(("parallel", "arbitrary")),
        name="hgrn_scan",
    )(proj, proj, proj, proj, lower_bounds, head_norm.reshape(1, w))


def _ffn(u, w_gate_up, w_down, layer):
    d_ff, d = w_down.shape[1:]
    act = matmul(u, RawWeight(w_gate_up, layer, FFN_UP_TN), d_ff, tm=MM_TALL_TM, w_col_offsets=(0, d_ff),
                 epilogue=_epi_swiglu, out_dtype=BF16, name="ffn_gate_up")
    return matmul(act, cast_tiled_bf16(w_down, layer, MM_TN), d, tm=FFN_DOWN_TM, out_dtype=BF16, name="ffn_down")


def _token_mixers(u, layer, batch, w_in, gate_bias, conv_w, m_norm, lower_bounds, h_norm,
                  w_bm, w_bh, w_gate, b_gate, w_out):
    rows, d = u.shape
    s_len = rows // batch
    gate_lo = 2 * M_HEADS * M_DQK + 2 * M_HEADS * M_DV
    gate_hi = gate_lo + 2 * M_HEADS
    w_in_t = jnp.swapaxes(w_in, 1, 2)
    proj_m = matmul(u, RawWeightT(w_in_t, layer, MM_TN, 0), gate_lo, tm=MM_TALL_TM, name="mix_in_proj_mlstm")
    proj_h = matmul(u, RawWeightT(w_in_t, layer, MM_TN, gate_hi), w_in.shape[2] - gate_hi, tm=MM_TALL_TM,
                    name="mix_in_proj_hgrn")
    gates_col = matmul(u, RawWeightT(w_in_t, layer, 2 * M_HEADS, gate_lo), 2 * M_HEADS, tm=MM_TM,
                       name="mix_gate_proj")
    gates_row = gates_col.reshape(batch, s_len, 2 * M_HEADS).transpose(0, 2, 1)
    ym = mlstm_scan(proj_m, gates_col, gates_row, gate_bias, conv_w, m_norm, batch, chunk=MLSTM_CHUNK)
    yh = hgrn_scan(proj_h, 0, lower_bounds, h_norm, batch, layer, chunk=HGRN_CHUNK)
    gate = matmul(u, RawWeight(w_gate, layer, MM_TN), 2 * d, tm=MM_TALL_TM, epilogue=_epi_sigmoid_bias,
                  extras=((b_gate.reshape(1, 2 * d), 0),), out_dtype=BF16, name="merge_gate")
    y_m = matmul(ym, cast_tiled_bf16(w_bm, layer, MM_WIDE_TN), d, tm=MM_TM, name="branch_mlstm")
    z = matmul(yh, cast_tiled_bf16(w_bh, layer, MM_WIDE_TN), d, tm=MM_TM, epilogue=_epi_merge,
               extras=((gate, 0), (gate, d), (y_m, 0)), out_dtype=BF16, name="branch_hgrn_merge")
    return matmul(z, RawWeight(w_out, layer, MM_TN), d, tm=MM_TM, out_dtype=BF16, name="mix_out")


def _xattn(u, h, mem2, mem_norm, batch, layer, w_q, w_kv, w_o, post, next_pre):
    memn = rmsnorm_bf16(mem2, mem_norm)
    kv = matmul(memn, RawWeight(w_kv, layer, MM_TN), w_kv.shape[2], tm=MM_TM, out_dtype=BF16, name="xattn_kv")
    q = matmul(u, cast_tiled_bf16(w_q, layer, MM_WIDE_TN), w_q.shape[2], tm=MM_TM, out_dtype=BF16, name="xattn_q")
    return xattn_out(q, kv, cast_tiled_bf16(w_o, layer, w_o.shape[2]), h, post, next_pre, batch, X_HEADS, X_DH)


def kernel(x, mem, ffn1_pre_norm, ffn1_w_gate_up, ffn1_w_down, ffn1_post_norm, mix_pre_norm, mix_w_in, mlstm_gate_bias, mlstm_conv, mlstm_head_norm, hgrn_lower_bounds, hgrn_head_norm, branch_w_mlstm, branch_w_hgrn, merge_w_gate, merge_b_gate, mix_w_out, mix_post_norm, xattn_pre_norm, xattn_mem_norm, xattn_w_q, xattn_w_kv, xattn_w_o, xattn_post_norm, ffn2_pre_norm, ffn2_w_gate_up, ffn2_w_down, ffn2_post_norm):
    batch, s_len, d = x.shape
    n_mem = mem.shape[1]
    depth = ffn1_pre_norm.shape[0]

    h = x.reshape(batch * s_len, d)
    mem2 = mem.reshape(batch * n_mem, d)
    u = rmsnorm_bf16(h, ffn1_pre_norm[0])
    for l in range(depth):
        y = _ffn(u, ffn1_w_gate_up, ffn1_w_down, l)
        h, u = combine(y, h, ffn1_post_norm[l], 0.5, mix_pre_norm[l])

        y = _token_mixers(u, l, batch, mix_w_in, mlstm_gate_bias[l], mlstm_conv[l], mlstm_head_norm[l],
                          hgrn_lower_bounds, hgrn_head_norm[l], branch_w_mlstm, branch_w_hgrn, merge_w_gate,
                          merge_b_gate[l], mix_w_out)
        h, u = combine(y, h, mix_post_norm[l], 1.0, xattn_pre_norm[l])

        h, u = _xattn(u, h, mem2, xattn_mem_norm[l], batch, l, xattn_w_q, xattn_w_kv, xattn_w_o,
                      xattn_post_norm[l], ffn2_pre_norm[l])

        y = _ffn(u, ffn2_w_gate_up, ffn2_w_down, l)
        nxt = ffn1_pre_norm[l + 1] if l + 1 < depth else None
        h, u = combine(y, h, ffn2_post_norm[l], 0.5, nxt)
    return h.reshape(batch, s_len, d)
```

```python
import functools
import math
from typing import NamedTuple

import jax
import jax.numpy as jnp
from jax import lax
from jax.experimental import pallas as pl
from jax.experimental.pallas import tpu as pltpu

EPS = 1e-6
F32 = jnp.float32
BF16 = jnp.bfloat16

V7X_VMEM_BYTES = 64 * 1024 * 1024
VMEM_LIMIT = V7X_VMEM_BYTES - 8 * 1024 * 1024
LANES = 128
SUBLANES = 8

CONV_K = 4
M_HEADS = 8
M_DQK = 128
M_DV = 256
H_HEADS = 16
H_DK = 128
H_DV = 128
X_HEADS = 4
X_DH = 256
MLSTM_CHUNK = 256
HGRN_CHUNK = 64
CAST_BLOCK_ELEMS = 2 * 1024 * 1024
MM_TM = 1024
MM_TALL_TM = 2048
MAX_DOUBLE_BUFFERED_PANEL_BYTES = 24 * 1024 * 1024
MM_ROW_CHUNK = 256
MM_TN = 512
MM_WIDE_TN = 1024
FFN_UP_TN = 256
FFN_DOWN_TM = 512


def _cparams(sem):
    return pltpu.CompilerParams(dimension_semantics=sem, vmem_limit_bytes=VMEM_LIMIT)


def _dot(a, b):
    return jnp.dot(a, b, preferred_element_type=F32)


def _dot_nt(a, b):
    return lax.dot_general(a, b, (((1,), (1,)), ((), ())), preferred_element_type=F32)


def _dot_tn(a, b):
    return lax.dot_general(a, b, (((0,), (0,)), ((), ())), preferred_element_type=F32)


def _sigmoid(x):
    return 1.0 / (1.0 + jnp.exp(-x))


def _silu(x):
    return x * _sigmoid(x)


def _log_sigmoid(x):
    return jnp.minimum(x, 0.0) - jnp.log(1.0 + jnp.exp(-jnp.abs(x)))


def _rms(x):
    return x * lax.rsqrt(jnp.mean(x * x, axis=-1, keepdims=True) + EPS)


def _rmsnorm_kernel(x_ref, g_ref, o_ref):
    o_ref[...] = (_rms(x_ref[...]) * g_ref[...]).astype(o_ref.dtype)


def rmsnorm_bf16(x, g, tm=256):
    m, d = x.shape
    tm = min(tm, m)
    return pl.pallas_call(
        _rmsnorm_kernel,
        grid=(m // tm,),
        in_specs=[pl.BlockSpec((tm, d), lambda i: (i, 0)), pl.BlockSpec((1, d), lambda i: (0, 0))],
        out_specs=pl.BlockSpec((tm, d), lambda i: (i, 0)),
        out_shape=jax.ShapeDtypeStruct((m, d), BF16),
        compiler_params=_cparams(("parallel",)),
        name="rmsnorm_bf16",
    )(x, g.reshape(1, d))


def _cast_kernel(x_ref, o_ref):
    o_ref[...] = x_ref[...].astype(o_ref.dtype)


def _largest_divisor(n, cap, multiple):
    best = multiple
    for c in range(multiple, min(n, cap) + 1, multiple):
        if n % c == 0:
            best = c
    return best


def cast_tiled_bf16(w3, layer, tn):
    _, k, n = w3.shape
    tn = min(tn, n)
    assert n % tn == 0
    tk = _largest_divisor(k, max(SUBLANES, CAST_BLOCK_ELEMS // tn), 2 * SUBLANES)
    return pl.pallas_call(
        _cast_kernel,
        grid=(n // tn, k // tk),
        in_specs=[pl.BlockSpec((None, tk, tn), lambda j, i: (layer, i, j))],
        out_specs=pl.BlockSpec((None, tk, tn), lambda j, i: (j, i, 0)),
        out_shape=jax.ShapeDtypeStruct((n // tn, k, tn), BF16),
        compiler_params=_cparams(("parallel", "parallel")),
        name="cast_tiled_bf16",
    )(w3)


def _mm_kernel(*refs, n_w, epilogue, transposed):
    a_ref = refs[0]
    w_refs = refs[1:1 + n_w]
    extra_refs = refs[1 + n_w:-1]
    o_ref = refs[-1]
    ws = [(w[0] if transposed else w[...]).astype(BF16) for w in w_refs]
    mm = _dot_nt if transposed else _dot
    tm = a_ref.shape[0]
    rc = min(tm, MM_ROW_CHUNK)
    for r0 in range(0, tm, rc):
        r = slice(r0, r0 + rc)
        accs = [mm(a_ref[r, :], w) for w in ws]
        ex = [e[...] if e.shape[0] == 1 else e[r, :] for e in extra_refs]
        o_ref[r, :] = epilogue(accs, ex).astype(o_ref.dtype)


class RawWeight(NamedTuple):
    w3: jax.Array
    layer: int
    tn: int


class RawWeightT(NamedTuple):
    w3t: jax.Array
    layer: int
    tn: int
    row0: int


def matmul(a, w, n_out, *, tm, w_col_offsets=(0,), epilogue=None, extras=(), out_dtype=F32, name="matmul"):
    m, k = a.shape
    tm = min(tm, m)
    if epilogue is None:
        epilogue = lambda accs, ex: accs[0]
    a_mode = pl.Buffered(1) if 2 * tm * k * a.dtype.itemsize > MAX_DOUBLE_BUFFERED_PANEL_BYTES else None
    in_specs = [pl.BlockSpec((tm, k), lambda i, j: (i, 0), pipeline_mode=a_mode)]
    args = [a]
    transposed = isinstance(w, RawWeightT)
    tn = w.shape[2] if isinstance(w, jax.Array) else w.tn
    assert m % tm == 0 and n_out % tn == 0
    for off in w_col_offsets:
        assert off % tn == 0
        if transposed:
            assert w.w3t.shape[2] == k and w.row0 % SUBLANES == 0
            spec = pl.BlockSpec((pl.Element(1), pl.Element(tn), pl.Element(k)),
                                functools.partial(lambda i, j, o: (w.layer, pl.multiple_of(o + j * tn, SUBLANES), 0),
                                                  o=w.row0 + off))
        elif isinstance(w, RawWeight):
            assert w.w3.shape[1] == k
            spec = pl.BlockSpec((None, k, tn), functools.partial(lambda i, j, o: (w.layer, 0, j + o), o=off // tn))
        else:
            assert w.shape[1] == k
            spec = pl.BlockSpec((None, k, tn), functools.partial(lambda i, j, o: (j + o, 0, 0), o=off // tn))
        in_specs.append(spec)
        args.append(w if isinstance(w, jax.Array) else w[0])
    for arr, off in extras:
        assert off % tn == 0
        rows = arr.shape[0]
        if rows == 1:
            in_specs.append(pl.BlockSpec((1, tn), functools.partial(lambda i, j, o: (0, j + o), o=off // tn)))
        else:
            in_specs.append(pl.BlockSpec((tm, tn), functools.partial(lambda i, j, o: (i, j + o), o=off // tn)))
        args.append(arr)
    return pl.pallas_call(
        functools.partial(_mm_kernel, n_w=len(w_col_offsets), epilogue=epilogue, transposed=transposed),
        grid=(m // tm, n_out // tn),
        in_specs=in_specs,
        out_specs=pl.BlockSpec((tm, tn), lambda i, j: (i, j)),
        out_shape=jax.ShapeDtypeStruct((m, n_out), out_dtype),
        compiler_params=_cparams(("parallel", "arbitrary")),
        name=name,
    )(*args)


def _epi_swiglu(accs, ex):
    return _silu(accs[0]) * accs[1]


def _epi_sigmoid_bias(accs, ex):
    return _sigmoid(accs[0] + ex[0])


def _epi_merge(accs, ex):
    gm, gh, ym = ex
    return gm.astype(F32) * ym + gh.astype(F32) * accs[0]


def _combine_kernel(y_ref, h_ref, post_ref, pre_ref, hn_ref, u_ref, *, scale):
    hn = h_ref[...] + scale * (_rms(y_ref[...].astype(F32)) * post_ref[...])
    hn_ref[...] = hn
    u_ref[...] = (_rms(hn) * pre_ref[...]).astype(u_ref.dtype)


def _combine_last_kernel(y_ref, h_ref, post_ref, hn_ref, *, scale):
    hn_ref[...] = h_ref[...] + scale * (_rms(y_ref[...].astype(F32)) * post_ref[...])


def combine(y, h, post, scale, next_pre=None, tm=256):
    m, d = y.shape
    tm = min(tm, m)
    row = pl.BlockSpec((tm, d), lambda i: (i, 0))
    vec = pl.BlockSpec((1, d), lambda i: (0, 0))
    if next_pre is None:
        return pl.pallas_call(
            functools.partial(_combine_last_kernel, scale=scale),
            grid=(m // tm,),
            in_specs=[row, row, vec],
            out_specs=row,
            out_shape=jax.ShapeDtypeStruct((m, d), F32),
            compiler_params=_cparams(("parallel",)),
            name="combine_last",
        )(y, h, post.reshape(1, d)), None
    return pl.pallas_call(
        functools.partial(_combine_kernel, scale=scale),
        grid=(m // tm,),
        in_specs=[row, row, vec, vec],
        out_specs=[row, row],
        out_shape=[jax.ShapeDtypeStruct((m, d), F32), jax.ShapeDtypeStruct((m, d), BF16)],
        compiler_params=_cparams(("parallel",)),
        name="combine",
    )(y, h, post.reshape(1, d), next_pre.reshape(1, d))


def _xattn_kernel(q_ref, k_ref, v_ref, o_ref, *, heads, dh):
    scale = dh ** -0.5
    for hd in range(heads):
        sl = slice(hd * dh, (hd + 1) * dh)
        s = _dot_nt(q_ref[:, sl], k_ref[:, sl]) * scale
        s = s - jnp.max(s, axis=-1, keepdims=True)
        p = jnp.exp(s)
        p = p / jnp.sum(p, axis=-1, keepdims=True)
        o_ref[:, sl] = _dot(p.astype(BF16), v_ref[:, sl]).astype(o_ref.dtype)


def _xattn_out_kernel(q_ref, k_ref, v_ref, wo_ref, h_ref, post_ref, pre_ref, hn_ref, u_ref, o_scr, *, heads, dh):
    _xattn_kernel(q_ref, k_ref, v_ref, o_scr, heads=heads, dh=dh)
    tm = o_scr.shape[0]
    for r0 in range(0, tm, tm // 2):
        r = slice(r0, r0 + tm // 2)
        y = _dot(o_scr[r, :], wo_ref[...])
        hn = h_ref[r, :] + _rms(y) * post_ref[...]
        hn_ref[r, :] = hn
        u_ref[r, :] = (_rms(hn) * pre_ref[...]).astype(u_ref.dtype)


def xattn_out(q, kv, w_o, h, post, next_pre, batch, heads, dh, tm=256):
    rows, width = q.shape
    d = h.shape[1]
    s_len = rows // batch
    n_mem = kv.shape[0] // batch
    tm = min(tm, s_len)
    nt = s_len // tm
    row = pl.BlockSpec((tm, d), lambda b, i: (b * nt + i, 0))
    vec = pl.BlockSpec((1, d), lambda b, i: (0, 0))
    return pl.pallas_call(
        functools.partial(_xattn_out_kernel, heads=heads, dh=dh),
        grid=(batch, nt),
        in_specs=[
            pl.BlockSpec((tm, width), lambda b, i: (b * nt + i, 0)),
            pl.BlockSpec((n_mem, width), lambda b, i: (b, 0)),
            pl.BlockSpec((n_mem, width), lambda b, i: (b, 1)),
            pl.BlockSpec((None, width, d), lambda b, i: (0, 0, 0), pipeline_mode=pl.Buffered(1)),
            row, vec, vec,
        ],
        out_specs=[row, row],
        out_shape=[jax.ShapeDtypeStruct((rows, d), F32), jax.ShapeDtypeStruct((rows, d), BF16)],
        scratch_shapes=[pltpu.VMEM((tm, width), BF16)],
        compiler_params=_cparams(("parallel", "arbitrary")),
        name="xattn_out",
    )(q, kv, kv, w_o, h, post.reshape(1, d), next_pre.reshape(1, d))


def _causal_conv_silu(x, tail, w):
    rows8 = lax.broadcasted_iota(jnp.int32, (SUBLANES, 1), 0)
    y = x * w[CONV_K - 1:CONV_K]
    for d in range(1, CONV_K):
        xs = pltpu.roll(x, d, 0)
        top = jnp.where(rows8 < d, pltpu.roll(tail, d, 0), xs[0:SUBLANES])
        xs = jnp.concatenate([top, xs[SUBLANES:]], axis=0)
        y = y + xs * w[CONV_K - 1 - d:CONV_K - d]
    return _silu(y)


def _mlstm_kernel(q_ref, k_ref, v_ref, og_ref, gc_ref, gr_ref, bc_ref, br_ref, conv_ref, norm_ref, out_ref,
                  c_scr, n_scr, m_scr, tq_scr, tk_scr, *, heads, dqk, dv):
    L = q_ref.shape[0]
    qk_w = heads * dqk

    @pl.when(pl.program_id(1) == 0)
    def _():
        c_scr[...] = jnp.zeros_like(c_scr)
        n_scr[...] = jnp.zeros_like(n_scr)
        m_scr[...] = jnp.zeros_like(m_scr)
        tq_scr[...] = jnp.zeros_like(tq_scr)
        tk_scr[...] = jnp.zeros_like(tk_scr)

    q_raw = q_ref[...]
    k_raw = k_ref[...]
    q_all = _causal_conv_silu(q_raw, tq_scr[...], conv_ref[:, 0:qk_w]) * (dqk ** -0.5)
    k_all = _causal_conv_silu(k_raw, tk_scr[...], conv_ref[:, qk_w:2 * qk_w])
    tq_scr[...] = q_raw[L - SUBLANES:L]
    tk_scr[...] = k_raw[L - SUBLANES:L]

    gc = gc_ref[...] + bc_ref[...]
    gr = gr_ref[0] + br_ref[...]
    t_idx = lax.broadcasted_iota(jnp.int32, (L, L), 0)
    s_idx = lax.broadcasted_iota(jnp.int32, (L, L), 1)
    causal = s_idx <= t_idx
    tri = causal.astype(F32)
    bc_all = jnp.dot(tri, _log_sigmoid(gc), precision=lax.Precision.HIGHEST, preferred_element_type=F32)
    br_all = lax.dot_general(_log_sigmoid(gr), tri, (((1,), (1,)), ((), ())),
                             precision=lax.Precision.HIGHEST, preferred_element_type=F32)

    for hd in range(heads):
        qs = slice(hd * dqk, (hd + 1) * dqk)
        vs = slice(hd * dv, (hd + 1) * dv)
        q = q_all[:, qs]
        k = k_all[:, qs]
        qb = q.astype(BF16)
        kb = k.astype(BF16)
        vb = v_ref[:, vs].astype(BF16)
        b_c = bc_all[:, heads + hd:heads + hd + 1]
        li_c = gc[:, hd:hd + 1]
        b_r = br_all[heads + hd:heads + hd + 1, :]
        li_r = gr[hd:hd + 1, :]
        m_prev = m_scr[hd:hd + 1, 0:1]
        n_prev = n_scr[hd:hd + 1, :]

        d = jnp.where(causal, b_c - b_r + li_r, -jnp.inf)
        m_inter = b_c + m_prev
        m_t = jnp.maximum(m_inter, jnp.max(d, axis=1, keepdims=True))
        w_intra = jnp.exp(d - m_t)
        w_inter = jnp.exp(m_inter - m_t)
        s = _dot_nt(qb, kb) * w_intra
        num = _dot(s.astype(BF16), vb) + w_inter * _dot(qb, c_scr[hd].astype(BF16))
        den = jnp.sum(s, axis=1, keepdims=True) + w_inter * jnp.sum(q * n_prev, axis=1, keepdims=True)
        hh = num * (1.0 / jnp.maximum(jnp.abs(den), jnp.exp(-m_t)))
        out = _rms(hh) * norm_ref[:, vs] * _sigmoid(og_ref[:, vs])
        out_ref[:, vs] = out.astype(out_ref.dtype)

        g = b_c[L - 1:L, :]
        a_c = g - b_c + li_c
        m_new = jnp.maximum(g + m_prev, jnp.max(a_c, axis=0, keepdims=True))
        decay = jnp.exp(g + m_prev - m_new)
        kw = k * jnp.exp(a_c - m_new)
        c_scr[hd] = decay * c_scr[hd] + _dot_tn(kw.astype(BF16), vb)
        n_scr[hd:hd + 1, :] = decay * n_prev + jnp.sum(kw, axis=0, keepdims=True)
        m_scr[hd:hd + 1, :] = jnp.broadcast_to(m_new, (1, LANES))


def mlstm_scan(proj, gates_col, gates_row, gate_bias, conv_w, head_norm, batch, *, chunk, heads=M_HEADS,
               dqk=M_DQK, dv=M_DV):
    rows = proj.shape[0]
    s_len = rows // batch
    L = min(chunk, s_len)
    nc = s_len // L
    qk_w = heads * dqk
    v_w = heads * dv
    assert qk_w % LANES == 0 and v_w % qk_w == 0
    bias_col = gate_bias.reshape(1, 2 * heads)
    bias_row = gate_bias.reshape(2 * heads, 1)
    kern = functools.partial(_mlstm_kernel, heads=heads, dqk=dqk, dv=dv)
    return pl.pallas_call(
        kern,
        grid=(batch, nc),
        in_specs=[
            pl.BlockSpec((L, qk_w), lambda b, c: (b * nc + c, 0)),
            pl.BlockSpec((L, qk_w), lambda b, c: (b * nc + c, 1)),
            pl.BlockSpec((L, v_w), lambda b, c: (b * nc + c, 2 * qk_w // v_w)),
            pl.BlockSpec((L, v_w), lambda b, c: (b * nc + c, 2 * qk_w // v_w + 1)),
            pl.BlockSpec((L, 2 * heads), lambda b, c: (b * nc + c, 0)),
            pl.BlockSpec((1, 2 * heads, L), lambda b, c: (b, 0, c)),
            pl.BlockSpec((1, 2 * heads), lambda b, c: (0, 0)),
            pl.BlockSpec((2 * heads, 1), lambda b, c: (0, 0)),
            pl.BlockSpec((CONV_K, 2 * qk_w), lambda b, c: (0, 0)),
            pl.BlockSpec((1, v_w), lambda b, c: (0, 0)),
        ],
        out_specs=pl.BlockSpec((L, v_w), lambda b, c: (b * nc + c, 0)),
        out_shape=jax.ShapeDtypeStruct((rows, v_w), BF16),
        scratch_shapes=[
            pltpu.VMEM((heads, dqk, dv), F32),
            pltpu.VMEM((heads, dqk), F32),
            pltpu.VMEM((heads, LANES), F32),
            pltpu.VMEM((SUBLANES, qk_w), F32),
            pltpu.VMEM((SUBLANES, qk_w), F32),
        ],
        compiler_params=_cparams(("parallel", "arbitrary")),
        name="mlstm_scan",
    )(proj, proj, proj, proj, gates_col, gates_row, bias_col, bias_row, conv_w, head_norm.reshape(1, v_w))


def _hgrn_kernel(q_ref, f_ref, i_ref, g_ref, lb_ref, norm_ref, out_ref, st_scr, *, heads, dk, dv, layer):
    L = q_ref.shape[0]

    @pl.when(pl.program_id(1) == 0)
    def _():
        st_scr[...] = jnp.zeros_like(st_scr)

    lbp = lb_ref[...]
    e = jnp.exp(lbp - jnp.max(lbp, axis=0, keepdims=True))
    p = e / jnp.sum(e, axis=0, keepdims=True)
    lb = jnp.zeros((1, lbp.shape[1]), F32)
    for j in range(1, layer + 1):
        lb = lb + p[j:j + 1]

    f = lb + (1.0 - lb) * _sigmoid(f_ref[...])
    lg = jnp.log(f)
    k_all = 1.0 - f
    t_idx = lax.broadcasted_iota(jnp.int32, (L, L), 0)
    s_idx = lax.broadcasted_iota(jnp.int32, (L, L), 1)
    tri = (s_idx <= t_idx).astype(F32)
    b_all = jnp.dot(tri, lg, precision=lax.Precision.HIGHEST, preferred_element_type=F32)
    level = jnp.where(t_idx > s_idx, 31 - lax.clz(t_idx ^ s_idx), jnp.where(t_idx == s_idx, -1, -2))
    row = lax.broadcasted_iota(jnp.int32, (L, 1), 0)
    n_levels = int(math.log2(L))
    k_slices = [slice(hd * dk, (hd + 1) * dk) for hd in range(heads)]
    q_all = q_ref[...]
    qb_all = q_all.astype(BF16)

    def scores(qx, kx, lw, acc):
        keep = level == lw
        return [jnp.where(keep, _dot_nt(qx[:, sl], kx[:, sl]), acc[i]) for i, sl in enumerate(k_slices)]

    a = scores(qb_all, k_all.astype(BF16), -1, [0.0] * heads)
    a = scores(qb_all, (k_all * pltpu.roll(f, L - 1, 0)).astype(BF16), 0, a)
    fw = jnp.where((row & 1) == 1, pltpu.roll(b_all, 1, 0), b_all)
    for lw in range(1, n_levels):
        w = 1 << lw
        odd = ((row >> lw) & 1) == 1
        gw = pltpu.roll(fw, L - w, 0)
        e = jnp.exp(jnp.where(odd, b_all, gw) - jnp.where(odd, fw, b_all))
        a = scores((q_all * e).astype(BF16), (k_all * e).astype(BF16), lw, a)
        fw = jnp.where(odd, pltpu.roll(fw, w, 0), fw)

    g = b_all[L - 1:L, :]
    qe_all = (q_all * jnp.exp(b_all)).astype(BF16)
    ke_all = (k_all * jnp.exp(g - b_all)).astype(BF16)
    eg = jnp.exp(g)
    ib_all = i_ref[...].astype(BF16)
    for hd in range(heads):
        ks = k_slices[hd]
        vs = slice(hd * dv, (hd + 1) * dv)
        st = st_scr[hd]
        o = _dot(a[hd].astype(BF16), ib_all[:, vs]) + _dot_nt(qe_all[:, ks], st.astype(BF16))
        st_scr[hd] = st * eg[:, ks] + _dot_tn(ib_all[:, vs], ke_all[:, ks])
        out = _rms(o) * norm_ref[:, vs] * _silu(g_ref[:, vs])
        out_ref[:, vs] = out.astype(out_ref.dtype)


def hgrn_scan(proj, col_block0, lower_bounds, head_norm, batch, layer, *, chunk, heads=H_HEADS, dk=H_DK, dv=H_DV):
    rows = proj.shape[0]
    s_len = rows // batch
    L = min(chunk, s_len)
    nc = s_len // L
    assert L & (L - 1) == 0 and dk == dv
    w = heads * dk
    depth = lower_bounds.shape[0]
    kern = functools.partial(_hgrn_kernel, heads=heads, dk=dk, dv=dv, layer=layer)
    blk = lambda o: pl.BlockSpec((L, w), functools.partial(lambda b, c, o: (b * nc + c, o), o=col_block0 + o))
    return pl.pallas_call(
        kern,
        grid=(batch, nc),
        in_specs=[blk(0), blk(1), blk(2), blk(3),
                  pl.BlockSpec((depth, w), lambda b, c: (0, 0)),
                  pl.BlockSpec((1, w), lambda b, c: (0, 0))],
        out_specs=pl.BlockSpec((L, w), lambda b, c: (b * nc + c, 0)),
        out_shape=jax.ShapeDtypeStruct((rows, w), BF16),
        scratch_shapes=[pltpu.VMEM((heads, dv, dk), F32)],
        compiler_params=_cparams(("parallel", "arbitrary")),
        name="hgrn_scan",
    )(proj, proj, proj, proj, lower_bounds, head_norm.reshape(1, w))


def _ffn(u, w_gate_up, w_down, layer):
    d_ff, d = w_down.shape[1:]
    act = matmul(u, RawWeight(w_gate_up, layer, FFN_UP_TN), d_ff, tm=MM_TALL_TM, w_col_offsets=(0, d_ff),
                 epilogue=_epi_swiglu, out_dtype=BF16, name="ffn_gate_up")
    return matmul(act, cast_tiled_bf16(w_down, layer, MM_TN), d, tm=FFN_DOWN_TM, out_dtype=BF16, name="ffn_down")


def _token_mixers(u, layer, batch, w_in, gate_bias, conv_w, m_norm, lower_bounds, h_norm,
                  w_bm, w_bh, w_gate, b_gate, w_out):
    rows, d = u.shape
    s_len = rows // batch
    gate_lo = 2 * M_HEADS * M_DQK + 2 * M_HEADS * M_DV
    gate_hi = gate_lo + 2 * M_HEADS
    w_in_t = jnp.swapaxes(w_in, 1, 2)
    proj_m = matmul(u, RawWeightT(w_in_t, layer, MM_TN, 0), gate_lo, tm=MM_TALL_TM, name="mix_in_proj_mlstm")
    proj_h = matmul(u, RawWeightT(w_in_t, layer, MM_TN, gate_hi), w_in.shape[2] - gate_hi, tm=MM_TALL_TM,
                    name="mix_in_proj_hgrn")
    gates_col = matmul(u, RawWeightT(w_in_t, layer, 2 * M_HEADS, gate_lo), 2 * M_HEADS, tm=MM_TM,
                       name="mix_gate_proj")
    gates_row = gates_col.reshape(batch, s_len, 2 * M_HEADS).transpose(0, 2, 1)
    ym = mlstm_scan(proj_m, gates_col, gates_row, gate_bias, conv_w, m_norm, batch, chunk=MLSTM_CHUNK)
    yh = hgrn_scan(proj_h, 0, lower_bounds, h_norm, batch, layer, chunk=HGRN_CHUNK)
    gate = matmul(u, RawWeight(w_gate, layer, MM_TN), 2 * d, tm=MM_TM, epilogue=_epi_sigmoid_bias,
                  extras=((b_gate.reshape(1, 2 * d), 0),), out_dtype=BF16, name="merge_gate")
    y_m = matmul(ym, cast_tiled_bf16(w_bm, layer, MM_WIDE_TN), d, tm=MM_TM, name="branch_mlstm")
    z = matmul(yh, cast_tiled_bf16(w_bh, layer, MM_WIDE_TN), d, tm=MM_TM, epilogue=_epi_merge,
               extras=((gate, 0), (gate, d), (y_m, 0)), out_dtype=BF16, name="branch_hgrn_merge")
    return matmul(z, RawWeight(w_out, layer, MM_TN), d, tm=MM_TM, out_dtype=BF16, name="mix_out")


def _xattn(u, h, mem2, mem_norm, batch, layer, w_q, w_kv, w_o, post, next_pre):
    memn = rmsnorm_bf16(mem2, mem_norm)
    kv = matmul(memn, RawWeight(w_kv, layer, MM_TN), w_kv.shape[2], tm=MM_TM, out_dtype=BF16, name="xattn_kv")
    q = matmul(u, cast_tiled_bf16(w_q, layer, MM_WIDE_TN), w_q.shape[2], tm=MM_TM, out_dtype=BF16, name="xattn_q")
    return xattn_out(q, kv, cast_tiled_bf16(w_o, layer, w_o.shape[2]), h, post, next_pre, batch, X_HEADS, X_DH)


def kernel(x, mem, ffn1_pre_norm, ffn1_w_gate_up, ffn1_w_down, ffn1_post_norm, mix_pre_norm, mix_w_in, mlstm_gate_bias, mlstm_conv, mlstm_head_norm, hgrn_lower_bounds, hgrn_head_norm, branch_w_mlstm, branch_w_hgrn, merge_w_gate, merge_b_gate, mix_w_out, mix_post_norm, xattn_pre_norm, xattn_mem_norm, xattn_w_q, xattn_w_kv, xattn_w_o, xattn_post_norm, ffn2_pre_norm, ffn2_w_gate_up, ffn2_w_down, ffn2_post_norm):
    batch, s_len, d = x.shape
    n_mem = mem.shape[1]
    depth = ffn1_pre_norm.shape[0]

    h = x.reshape(batch * s_len, d)
    mem2 = mem.reshape(batch * n_mem, d)
    u = rmsnorm_bf16(h, ffn1_pre_norm[0])
    for l in range(depth):
        y = _ffn(u, ffn1_w_gate_up, ffn1_w_down, l)
        h, u = combine(y, h, ffn1_post_norm[l], 0.5, mix_pre_norm[l])

        y = _token_mixers(u, l, batch, mix_w_in, mlstm_gate_bias[l], mlstm_conv[l], mlstm_head_norm[l],
                          hgrn_lower_bounds, hgrn_head_norm[l], branch_w_mlstm, branch_w_hgrn, merge_w_gate,
                          merge_b_gate[l], mix_w_out)
        h, u = combine(y, h, mix_post_norm[l], 1.0, xattn_pre_norm[l])

        h, u = _xattn(u, h, mem2, xattn_mem_norm[l], batch, l, xattn_w_q, xattn_w_kv, xattn_w_o,
                      xattn_post_norm[l], ffn2_pre_norm[l])

        y = _ffn(u, ffn2_w_gate_up, ffn2_w_down, l)
        nxt = ffn1_pre_norm[l + 1] if l + 1 < depth else None
        h, u = combine(y, h, ffn2_post_norm[l], 0.5, nxt)
    return h.reshape(batch, s_len, d)
```

```python
import functools
import math
from typing import NamedTuple

import jax
import jax.numpy as jnp
from jax import lax
from jax.experimental import pallas as pl
from jax.experimental.pallas import tpu as pltpu

EPS = 1e-6
F32 = jnp.float32
BF16 = jnp.bfloat16

V7X_VMEM_BYTES = 64 * 1024 * 1024
VMEM_LIMIT = V7X_VMEM_BYTES - 8 * 1024 * 1024
LANES = 128
SUBLANES = 8

CONV_K = 4
M_HEADS = 8
M_DQK = 128
M_DV = 256
H_HEADS = 16
H_DK = 128
H_DV = 128
X_HEADS = 4
X_DH = 256
MLSTM_CHUNK = 256
HGRN_CHUNK = 64
CAST_BLOCK_ELEMS = 2 * 1024 * 1024
MM_TM = 1024
MM_TALL_TM = 2048
MAX_DOUBLE_BUFFERED_PANEL_BYTES = 24 * 1024 * 1024
MM_ROW_CHUNK = 256
MM_TN = 512
MM_WIDE_TN = 1024
FFN_UP_TN = 256
FFN_DOWN_TM = 512


def _cparams(sem):
    return pltpu.CompilerParams(dimension_semantics=sem, vmem_limit_bytes=VMEM_LIMIT)


def _dot(a, b):
    return jnp.dot(a, b, preferred_element_type=F32)


def _dot_nt(a, b):
    return lax.dot_general(a, b, (((1,), (1,)), ((), ())), preferred_element_type=F32)


def _dot_tn(a, b):
    return lax.dot_general(a, b, (((0,), (0,)), ((), ())), preferred_element_type=F32)


def _sigmoid(x):
    return 1.0 / (1.0 + jnp.exp(-x))


def _silu(x):
    return x * _sigmoid(x)


def _log_sigmoid(x):
    return jnp.minimum(x, 0.0) - jnp.log(1.0 + jnp.exp(-jnp.abs(x)))


def _rms(x):
    return x * lax.rsqrt(jnp.mean(x * x, axis=-1, keepdims=True) + EPS)


def _rmsnorm_kernel(x_ref, g_ref, o_ref):
    o_ref[...] = (_rms(x_ref[...]) * g_ref[...]).astype(o_ref.dtype)


def rmsnorm_bf16(x, g, tm=256):
    m, d = x.shape
    tm = min(tm, m)
    return pl.pallas_call(
        _rmsnorm_kernel,
        grid=(m // tm,),
        in_specs=[pl.BlockSpec((tm, d), lambda i: (i, 0)), pl.BlockSpec((1, d), lambda i: (0, 0))],
        out_specs=pl.BlockSpec((tm, d), lambda i: (i, 0)),
        out_shape=jax.ShapeDtypeStruct((m, d), BF16),
        compiler_params=_cparams(("parallel",)),
        name="rmsnorm_bf16",
    )(x, g.reshape(1, d))


def _cast_kernel(x_ref, o_ref):
    o_ref[...] = x_ref[...].astype(o_ref.dtype)


def _largest_divisor(n, cap, multiple):
    best = multiple
    for c in range(multiple, min(n, cap) + 1, multiple):
        if n % c == 0:
            best = c
    return best


def cast_tiled_bf16(w3, layer, tn):
    _, k, n = w3.shape
    tn = min(tn, n)
    assert n % tn == 0
    tk = _largest_divisor(k, max(SUBLANES, CAST_BLOCK_ELEMS // tn), 2 * SUBLANES)
    return pl.pallas_call(
        _cast_kernel,
        grid=(n // tn, k // tk),
        in_specs=[pl.BlockSpec((None, tk, tn), lambda j, i: (layer, i, j))],
        out_specs=pl.BlockSpec((None, tk, tn), lambda j, i: (j, i, 0)),
        out_shape=jax.ShapeDtypeStruct((n // tn, k, tn), BF16),
        compiler_params=_cparams(("parallel", "parallel")),
        name="cast_tiled_bf16",
    )(w3)


def _mm_kernel(*refs, n_w, epilogue, transposed, row_chunk, side_cast):
    a_ref = refs[0]
    w_refs = refs[1:1 + n_w]
    if side_cast:
        side_in, o_ref, side_out = refs[-3:]
        extra_refs = refs[1 + n_w:-3]
        tn_c = side_out.shape[2]
        for t in range(side_out.shape[0]):
            side_out[t] = side_in[:, t * tn_c:(t + 1) * tn_c].astype(side_out.dtype)
    else:
        o_ref = refs[-1]
        extra_refs = refs[1 + n_w:-1]
    ws = [(w[0] if transposed else w[...]).astype(BF16) for w in w_refs]
    mm = _dot_nt if transposed else _dot
    tm = a_ref.shape[0]
    rc = min(tm, row_chunk)
    for r0 in range(0, tm, rc):
        r = slice(r0, r0 + rc)
        accs = [mm(a_ref[r, :], w) for w in ws]
        ex = [e[...] if e.shape[0] == 1 else e[r, :] for e in extra_refs]
        o_ref[r, :] = epilogue(accs, ex).astype(o_ref.dtype)


class RawWeight(NamedTuple):
    w3: jax.Array
    layer: int
    tn: int


class RawWeightT(NamedTuple):
    w3t: jax.Array
    layer: int
    tn: int
    row0: int


def matmul(a, w, n_out, *, tm, w_col_offsets=(0,), epilogue=None, extras=(), out_dtype=F32, side_cast=None,
           name="matmul"):
    m, k = a.shape
    tm = min(tm, m)
    row_chunk = MM_ROW_CHUNK if epilogue is not None else tm
    if epilogue is None:
        epilogue = lambda accs, ex: accs[0]
    a_mode = pl.Buffered(1) if 2 * tm * k * a.dtype.itemsize > MAX_DOUBLE_BUFFERED_PANEL_BYTES else None
    in_specs = [pl.BlockSpec((tm, k), lambda i, j: (i, 0), pipeline_mode=a_mode)]
    args = [a]
    transposed = isinstance(w, RawWeightT)
    tn = w.shape[2] if isinstance(w, jax.Array) else w.tn
    assert m % tm == 0 and n_out % tn == 0
    for off in w_col_offsets:
        assert off % tn == 0
        if transposed:
            assert w.w3t.shape[2] == k and w.row0 % SUBLANES == 0
            spec = pl.BlockSpec((pl.Element(1), pl.Element(tn), pl.Element(k)),
                                functools.partial(lambda i, j, o: (w.layer, pl.multiple_of(o + j * tn, SUBLANES), 0),
                                                  o=w.row0 + off))
        elif isinstance(w, RawWeight):
            assert w.w3.shape[1] == k
            spec = pl.BlockSpec((None, k, tn), functools.partial(lambda i, j, o: (w.layer, 0, j + o), o=off // tn))
        else:
            assert w.shape[1] == k
            spec = pl.BlockSpec((None, k, tn), functools.partial(lambda i, j, o: (j + o, 0, 0), o=off // tn))
        in_specs.append(spec)
        args.append(w if isinstance(w, jax.Array) else w[0])
    for arr, off in extras:
        assert off % tn == 0
        rows = arr.shape[0]
        if rows == 1:
            in_specs.append(pl.BlockSpec((1, tn), functools.partial(lambda i, j, o: (0, j + o), o=off // tn)))
        else:
            in_specs.append(pl.BlockSpec((tm, tn), functools.partial(lambda i, j, o: (i, j + o), o=off // tn)))
        args.append(arr)
    n_col = n_out // tn
    out_specs = pl.BlockSpec((tm, tn), lambda i, j: (i, j))
    out_shape = jax.ShapeDtypeStruct((m, n_out), out_dtype)
    if side_cast is not None:
        _, kc, nc = side_cast.w3.shape
        slab = kc // ((m // tm) * n_col)
        assert slab * (m // tm) * n_col == kc and slab % (2 * SUBLANES) == 0 and nc % side_cast.tn == 0
        in_specs.append(pl.BlockSpec((None, slab, nc), lambda i, j: (side_cast.layer, i * n_col + j, 0)))
        args.append(side_cast.w3)
        out_specs = [out_specs, pl.BlockSpec((nc // side_cast.tn, slab, side_cast.tn),
                                             lambda i, j: (0, i * n_col + j, 0))]
        out_shape = [out_shape, jax.ShapeDtypeStruct((nc // side_cast.tn, kc, side_cast.tn), BF16)]
    return pl.pallas_call(
        functools.partial(_mm_kernel, n_w=len(w_col_offsets), epilogue=epilogue, transposed=transposed,
                          row_chunk=row_chunk, side_cast=side_cast is not None),
        grid=(m // tm, n_col),
        in_specs=in_specs,
        out_specs=out_specs,
        out_shape=out_shape,
        compiler_params=_cparams(("parallel", "arbitrary")),
        name=name,
    )(*args)


def _epi_swiglu(accs, ex):
    return _silu(accs[0]) * accs[1]


def _epi_sigmoid_bias(accs, ex):
    return _sigmoid(accs[0] + ex[0])


def _epi_merge(accs, ex):
    gm, gh, ym = ex
    return gm.astype(F32) * ym + gh.astype(F32) * accs[0]


def _combine_kernel(y_ref, h_ref, post_ref, pre_ref, hn_ref, u_ref, *, scale):
    hn = h_ref[...] + scale * (_rms(y_ref[...].astype(F32)) * post_ref[...])
    hn_ref[...] = hn
    u_ref[...] = (_rms(hn) * pre_ref[...]).astype(u_ref.dtype)


def _combine_last_kernel(y_ref, h_ref, post_ref, hn_ref, *, scale):
    hn_ref[...] = h_ref[...] + scale * (_rms(y_ref[...].astype(F32)) * post_ref[...])


def combine(y, h, post, scale, next_pre=None, tm=256):
    m, d = y.shape
    tm = min(tm, m)
    row = pl.BlockSpec((tm, d), lambda i: (i, 0))
    vec = pl.BlockSpec((1, d), lambda i: (0, 0))
    if next_pre is None:
        return pl.pallas_call(
            functools.partial(_combine_last_kernel, scale=scale),
            grid=(m // tm,),
            in_specs=[row, row, vec],
            out_specs=row,
            out_shape=jax.ShapeDtypeStruct((m, d), F32),
            compiler_params=_cparams(("parallel",)),
            name="combine_last",
        )(y, h, post.reshape(1, d)), None
    return pl.pallas_call(
        functools.partial(_combine_kernel, scale=scale),
        grid=(m // tm,),
        in_specs=[row, row, vec, vec],
        out_specs=[row, row],
        out_shape=[jax.ShapeDtypeStruct((m, d), F32), jax.ShapeDtypeStruct((m, d), BF16)],
        compiler_params=_cparams(("parallel",)),
        name="combine",
    )(y, h, post.reshape(1, d), next_pre.reshape(1, d))


def _xattn_kernel(q_ref, k_ref, v_ref, o_ref, *, heads, dh):
    scale = dh ** -0.5
    for hd in range(heads):
        sl = slice(hd * dh, (hd + 1) * dh)
        s = _dot_nt(q_ref[:, sl], k_ref[:, sl]) * scale
        s = s - jnp.max(s, axis=-1, keepdims=True)
        p = jnp.exp(s)
        p = p / jnp.sum(p, axis=-1, keepdims=True)
        o_ref[:, sl] = _dot(p.astype(BF16), v_ref[:, sl]).astype(o_ref.dtype)


def _xattn_out_kernel(q_ref, k_ref, v_ref, wo_ref, h_ref, post_ref, pre_ref, hn_ref, u_ref, o_scr, *, heads, dh):
    _xattn_kernel(q_ref, k_ref, v_ref, o_scr, heads=heads, dh=dh)
    tm = o_scr.shape[0]
    for r0 in range(0, tm, tm // 2):
        r = slice(r0, r0 + tm // 2)
        y = _dot(o_scr[r, :], wo_ref[...])
        hn = h_ref[r, :] + _rms(y) * post_ref[...]
        hn_ref[r, :] = hn
        u_ref[r, :] = (_rms(hn) * pre_ref[...]).astype(u_ref.dtype)


def xattn_out(q, kv, w_o, h, post, next_pre, batch, heads, dh, tm=256):
    rows, width = q.shape
    d = h.shape[1]
    s_len = rows // batch
    n_mem = kv.shape[0] // batch
    tm = min(tm, s_len)
    nt = s_len // tm
    row = pl.BlockSpec((tm, d), lambda b, i: (b * nt + i, 0))
    vec = pl.BlockSpec((1, d), lambda b, i: (0, 0))
    return pl.pallas_call(
        functools.partial(_xattn_out_kernel, heads=heads, dh=dh),
        grid=(batch, nt),
        in_specs=[
            pl.BlockSpec((tm, width), lambda b, i: (b * nt + i, 0)),
            pl.BlockSpec((n_mem, width), lambda b, i: (b, 0)),
            pl.BlockSpec((n_mem, width), lambda b, i: (b, 1)),
            pl.BlockSpec((None, width, d), lambda b, i: (0, 0, 0), pipeline_mode=pl.Buffered(1)),
            row, vec, vec,
        ],
        out_specs=[row, row],
        out_shape=[jax.ShapeDtypeStruct((rows, d), F32), jax.ShapeDtypeStruct((rows, d), BF16)],
        scratch_shapes=[pltpu.VMEM((tm, width), BF16)],
        compiler_params=_cparams(("parallel", "arbitrary")),
        name="xattn_out",
    )(q, kv, kv, w_o, h, post.reshape(1, d), next_pre.reshape(1, d))


def _causal_conv_silu(x, tail, w):
    rows8 = lax.broadcasted_iota(jnp.int32, (SUBLANES, 1), 0)
    y = x * w[CONV_K - 1:CONV_K]
    for d in range(1, CONV_K):
        xs = pltpu.roll(x, d, 0)
        top = jnp.where(rows8 < d, pltpu.roll(tail, d, 0), xs[0:SUBLANES])
        xs = jnp.concatenate([top, xs[SUBLANES:]], axis=0)
        y = y + xs * w[CONV_K - 1 - d:CONV_K - d]
    return _silu(y)


def _mlstm_kernel(q_ref, k_ref, v_ref, og_ref, gc_ref, gr_ref, bc_ref, br_ref, conv_ref, norm_ref, out_ref,
                  c_scr, n_scr, m_scr, tq_scr, tk_scr, *, heads, dqk, dv):
    L = q_ref.shape[0]
    qk_w = heads * dqk

    @pl.when(pl.program_id(1) == 0)
    def _():
        c_scr[...] = jnp.zeros_like(c_scr)
        n_scr[...] = jnp.zeros_like(n_scr)
        m_scr[...] = jnp.zeros_like(m_scr)
        tq_scr[...] = jnp.zeros_like(tq_scr)
        tk_scr[...] = jnp.zeros_like(tk_scr)

    q_raw = q_ref[...]
    k_raw = k_ref[...]
    q_all = _causal_conv_silu(q_raw, tq_scr[...], conv_ref[:, 0:qk_w]) * (dqk ** -0.5)
    k_all = _causal_conv_silu(k_raw, tk_scr[...], conv_ref[:, qk_w:2 * qk_w])
    tq_scr[...] = q_raw[L - SUBLANES:L]
    tk_scr[...] = k_raw[L - SUBLANES:L]

    gc = gc_ref[...] + bc_ref[...]
    gr = gr_ref[0] + br_ref[...]
    t_idx = lax.broadcasted_iota(jnp.int32, (L, L), 0)
    s_idx = lax.broadcasted_iota(jnp.int32, (L, L), 1)
    causal = s_idx <= t_idx
    tri = causal.astype(F32)
    bc_all = jnp.dot(tri, _log_sigmoid(gc), precision=lax.Precision.HIGHEST, preferred_element_type=F32)
    br_all = lax.dot_general(_log_sigmoid(gr), tri, (((1,), (1,)), ((), ())),
                             precision=lax.Precision.HIGHEST, preferred_element_type=F32)

    for hd in range(heads):
        qs = slice(hd * dqk, (hd + 1) * dqk)
        vs = slice(hd * dv, (hd + 1) * dv)
        q = q_all[:, qs]
        k = k_all[:, qs]
        qb = q.astype(BF16)
        kb = k.astype(BF16)
        vb = v_ref[:, vs].astype(BF16)
        b_c = bc_all[:, heads + hd:heads + hd + 1]
        li_c = gc[:, hd:hd + 1]
        b_r = br_all[heads + hd:heads + hd + 1, :]
        li_r = gr[hd:hd + 1, :]
        m_prev = m_scr[hd:hd + 1, 0:1]
        n_prev = n_scr[hd:hd + 1, :]

        d = jnp.where(causal, b_c - b_r + li_r, -jnp.inf)
        m_inter = b_c + m_prev
        m_t = jnp.maximum(m_inter, jnp.max(d, axis=1, keepdims=True))
        w_intra = jnp.exp(d - m_t)
        w_inter = jnp.exp(m_inter - m_t)
        s = _dot_nt(qb, kb) * w_intra
        num = _dot(s.astype(BF16), vb) + w_inter * _dot(qb, c_scr[hd].astype(BF16))
        den = jnp.sum(s, axis=1, keepdims=True) + w_inter * jnp.sum(q * n_prev, axis=1, keepdims=True)
        hh = num * (1.0 / jnp.maximum(jnp.abs(den), jnp.exp(-m_t)))
        out = _rms(hh) * norm_ref[:, vs] * _sigmoid(og_ref[:, vs])
        out_ref[:, vs] = out.astype(out_ref.dtype)

        g = b_c[L - 1:L, :]
        a_c = g - b_c + li_c
        m_new = jnp.maximum(g + m_prev, jnp.max(a_c, axis=0, keepdims=True))
        decay = jnp.exp(g + m_prev - m_new)
        kw = k * jnp.exp(a_c - m_new)
        c_scr[hd] = decay * c_scr[hd] + _dot_tn(kw.astype(BF16), vb)
        n_scr[hd:hd + 1, :] = decay * n_prev + jnp.sum(kw, axis=0, keepdims=True)
        m_scr[hd:hd + 1, :] = jnp.broadcast_to(m_new, (1, LANES))


def mlstm_scan(proj, gates_col, gates_row, gate_bias, conv_w, head_norm, batch, *, chunk, heads=M_HEADS,
               dqk=M_DQK, dv=M_DV):
    rows = proj.shape[0]
    s_len = rows // batch
    L = min(chunk, s_len)
    nc = s_len // L
    qk_w = heads * dqk
    v_w = heads * dv
    assert qk_w % LANES == 0 and v_w % qk_w == 0
    bias_col = gate_bias.reshape(1, 2 * heads)
    bias_row = gate_bias.reshape(2 * heads, 1)
    kern = functools.partial(_mlstm_kernel, heads=heads, dqk=dqk, dv=dv)
    return pl.pallas_call(
        kern,
        grid=(batch, nc),
        in_specs=[
            pl.BlockSpec((L, qk_w), lambda b, c: (b * nc + c, 0)),
            pl.BlockSpec((L, qk_w), lambda b, c: (b * nc + c, 1)),
            pl.BlockSpec((L, v_w), lambda b, c: (b * nc + c, 2 * qk_w // v_w)),
            pl.BlockSpec((L, v_w), lambda b, c: (b * nc + c, 2 * qk_w // v_w + 1)),
            pl.BlockSpec((L, 2 * heads), lambda b, c: (b * nc + c, 0)),
            pl.BlockSpec((1, 2 * heads, L), lambda b, c: (b, 0, c)),
            pl.BlockSpec((1, 2 * heads), lambda b, c: (0, 0)),
            pl.BlockSpec((2 * heads, 1), lambda b, c: (0, 0)),
            pl.BlockSpec((CONV_K, 2 * qk_w), lambda b, c: (0, 0)),
            pl.BlockSpec((1, v_w), lambda b, c: (0, 0)),
        ],
        out_specs=pl.BlockSpec((L, v_w), lambda b, c: (b * nc + c, 0)),
        out_shape=jax.ShapeDtypeStruct((rows, v_w), BF16),
        scratch_shapes=[
            pltpu.VMEM((heads, dqk, dv), F32),
            pltpu.VMEM((heads, dqk), F32),
            pltpu.VMEM((heads, LANES), F32),
            pltpu.VMEM((SUBLANES, qk_w), F32),
            pltpu.VMEM((SUBLANES, qk_w), F32),
        ],
        compiler_params=_cparams(("parallel", "arbitrary")),
        name="mlstm_scan",
    )(proj, proj, proj, proj, gates_col, gates_row, bias_col, bias_row, conv_w, head_norm.reshape(1, v_w))


def _hgrn_kernel(q_ref, f_ref, i_ref, g_ref, lb_ref, norm_ref, out_ref, st_scr, *, heads, dk, dv, layer):
    L = q_ref.shape[0]

    @pl.when(pl.program_id(1) == 0)
    def _():
        st_scr[...] = jnp.zeros_like(st_scr)

    lbp = lb_ref[...]
    e = jnp.exp(lbp - jnp.max(lbp, axis=0, keepdims=True))
    p = e / jnp.sum(e, axis=0, keepdims=True)
    lb = jnp.zeros((1, lbp.shape[1]), F32)
    for j in range(1, layer + 1):
        lb = lb + p[j:j + 1]

    f = lb + (1.0 - lb) * _sigmoid(f_ref[...])
    lg = jnp.log(f)
    k_all = 1.0 - f
    t_idx = lax.broadcasted_iota(jnp.int32, (L, L), 0)
    s_idx = lax.broadcasted_iota(jnp.int32, (L, L), 1)
    tri = (s_idx <= t_idx).astype(F32)
    b_all = jnp.dot(tri, lg, precision=lax.Precision.HIGHEST, preferred_element_type=F32)
    level = jnp.where(t_idx > s_idx, 31 - lax.clz(t_idx ^ s_idx), jnp.where(t_idx == s_idx, -1, -2))
    row = lax.broadcasted_iota(jnp.int32, (L, 1), 0)
    n_levels = int(math.log2(L))
    k_slices = [slice(hd * dk, (hd + 1) * dk) for hd in range(heads)]
    q_all = q_ref[...]
    qb_all = q_all.astype(BF16)

    def scores(qx, kx, lw, acc):
        keep = level == lw
        return [jnp.where(keep, _dot_nt(qx[:, sl], kx[:, sl]), acc[i]) for i, sl in enumerate(k_slices)]

    a = scores(qb_all, k_all.astype(BF16), -1, [0.0] * heads)
    a = scores(qb_all, (k_all * pltpu.roll(f, L - 1, 0)).astype(BF16), 0, a)
    fw = jnp.where((row & 1) == 1, pltpu.roll(b_all, 1, 0), b_all)
    for lw in range(1, n_levels):
        w = 1 << lw
        odd = ((row >> lw) & 1) == 1
        gw = pltpu.roll(fw, L - w, 0)
        e = jnp.exp(jnp.where(odd, b_all, gw) - jnp.where(odd, fw, b_all))
        a = scores((q_all * e).astype(BF16), (k_all * e).astype(BF16), lw, a)
        fw = jnp.where(odd, pltpu.roll(fw, w, 0), fw)

    g = b_all[L - 1:L, :]
    qe_all = (q_all * jnp.exp(b_all)).astype(BF16)
    ke_all = (k_all * jnp.exp(g - b_all)).astype(BF16)
    eg = jnp.exp(g)
    ib_all = i_ref[...].astype(BF16)
    for hd in range(heads):
        ks = k_slices[hd]
        vs = slice(hd * dv, (hd + 1) * dv)
        st = st_scr[hd]
        o = _dot(a[hd].astype(BF16), ib_all[:, vs]) + _dot_nt(qe_all[:, ks], st.astype(BF16))
        st_scr[hd] = st * eg[:, ks] + _dot_tn(ib_all[:, vs], ke_all[:, ks])
        out = _rms(o) * norm_ref[:, vs] * _silu(g_ref[:, vs])
        out_ref[:, vs] = out.astype(out_ref.dtype)


def hgrn_scan(proj, col_block0, lower_bounds, head_norm, batch, layer, *, chunk, heads=H_HEADS, dk=H_DK, dv=H_DV):
    rows = proj.shape[0]
    s_len = rows // batch
    L = min(chunk, s_len)
    nc = s_len // L
    assert L & (L - 1) == 0 and dk == dv
    w = heads * dk
    depth = lower_bounds.shape[0]
    kern = functools.partial(_hgrn_kernel, heads=heads, dk=dk, dv=dv, layer=layer)
    blk = lambda o: pl.BlockSpec((L, w), functools.partial(lambda b, c, o: (b * nc + c, o), o=col_block0 + o))
    return pl.pallas_call(
        kern,
        grid=(batch, nc),
        in_specs=[blk(0), blk(1), blk(2), blk(3),
                  pl.BlockSpec((depth, w), lambda b, c: (0, 0)),
                  pl.BlockSpec((1, w), lambda b, c: (0, 0))],
        out_specs=pl.BlockSpec((L, w), lambda b, c: (b * nc + c, 0)),
        out_shape=jax.ShapeDtypeStruct((rows, w), BF16),
        scratch_shapes=[pltpu.VMEM((heads, dv, dk), F32)],
        compiler_params=_cparams(("parallel", "arbitrary")),
        name="hgrn_scan",
    )(proj, proj, proj, proj, lower_bounds, head_norm.reshape(1, w))


def _ffn(u, w_gate_up, w_down, layer):
    d_ff, d = w_down.shape[1:]
    act, w_dn = matmul(u, RawWeight(w_gate_up, layer, FFN_UP_TN), d_ff, tm=MM_TALL_TM, w_col_offsets=(0, d_ff),
                       epilogue=_epi_swiglu, out_dtype=BF16, side_cast=RawWeight(w_down, layer, MM_TN),
                       name="ffn_gate_up")
    return matmul(act, w_dn, d, tm=FFN_DOWN_TM, out_dtype=BF16, name="ffn_down")


def _token_mixers(u, layer, batch, w_in, gate_bias, conv_w, m_norm, lower_bounds, h_norm,
                  w_bm, w_bh, w_gate, b_gate, w_out):
    rows, d = u.shape
    s_len = rows // batch
    gate_lo = 2 * M_HEADS * M_DQK + 2 * M_HEADS * M_DV
    gate_hi = gate_lo + 2 * M_HEADS
    w_in_t = jnp.swapaxes(w_in, 1, 2)
    proj_m = matmul(u, RawWeightT(w_in_t, layer, MM_TN, 0), gate_lo, tm=MM_TALL_TM, name="mix_in_proj_mlstm")
    proj_h = matmul(u, RawWeightT(w_in_t, layer, MM_TN, gate_hi), w_in.shape[2] - gate_hi, tm=MM_TALL_TM,
                    name="mix_in_proj_hgrn")
    gates_col = matmul(u, RawWeightT(w_in_t, layer, 2 * M_HEADS, gate_lo), 2 * M_HEADS, tm=MM_TM,
                       name="mix_gate_proj")
    gates_row = gates_col.reshape(batch, s_len, 2 * M_HEADS).transpose(0, 2, 1)
    ym = mlstm_scan(proj_m, gates_col, gates_row, gate_bias, conv_w, m_norm, batch, chunk=MLSTM_CHUNK)
    yh = hgrn_scan(proj_h, 0, lower_bounds, h_norm, batch, layer, chunk=HGRN_CHUNK)
    gate = matmul(u, RawWeight(w_gate, layer, MM_TN), 2 * d, tm=MM_TM, epilogue=_epi_sigmoid_bias,
                  extras=((b_gate.reshape(1, 2 * d), 0),), out_dtype=BF16, name="merge_gate")
    y_m = matmul(ym, cast_tiled_bf16(w_bm, layer, MM_WIDE_TN), d, tm=MM_TM, name="branch_mlstm")
    z = matmul(yh, cast_tiled_bf16(w_bh, layer, MM_WIDE_TN), d, tm=MM_TM, epilogue=_epi_merge,
               extras=((gate, 0), (gate, d), (y_m, 0)), out_dtype=BF16, name="branch_hgrn_merge")
    return matmul(z, RawWeight(w_out, layer, MM_TN), d, tm=MM_TM, out_dtype=BF16, name="mix_out")


def _xattn(u, h, mem2, mem_norm, batch, layer, w_q, w_kv, w_o, post, next_pre):
    memn = rmsnorm_bf16(mem2, mem_norm)
    kv = matmul(memn, RawWeight(w_kv, layer, MM_TN), w_kv.shape[2], tm=MM_TM, out_dtype=BF16, name="xattn_kv")
    q = matmul(u, cast_tiled_bf16(w_q, layer, MM_WIDE_TN), w_q.shape[2], tm=MM_TM, out_dtype=BF16, name="xattn_q")
    return xattn_out(q, kv, cast_tiled_bf16(w_o, layer, w_o.shape[2]), h, post, next_pre, batch, X_HEADS, X_DH)


def kernel(x, mem, ffn1_pre_norm, ffn1_w_gate_up, ffn1_w_down, ffn1_post_norm, mix_pre_norm, mix_w_in, mlstm_gate_bias, mlstm_conv, mlstm_head_norm, hgrn_lower_bounds, hgrn_head_norm, branch_w_mlstm, branch_w_hgrn, merge_w_gate, merge_b_gate, mix_w_out, mix_post_norm, xattn_pre_norm, xattn_mem_norm, xattn_w_q, xattn_w_kv, xattn_w_o, xattn_post_norm, ffn2_pre_norm, ffn2_w_gate_up, ffn2_w_down, ffn2_post_norm):
    batch, s_len, d = x.shape
    n_mem = mem.shape[1]
    depth = ffn1_pre_norm.shape[0]

    h = x.reshape(batch * s_len, d)
    mem2 = mem.reshape(batch * n_mem, d)
    u = rmsnorm_bf16(h, ffn1_pre_norm[0])
    for l in range(depth):
        y = _ffn(u, ffn1_w_gate_up, ffn1_w_down, l)
        h, u = combine(y, h, ffn1_post_norm[l], 0.5, mix_pre_norm[l])

        y = _token_mixers(u, l, batch, mix_w_in, mlstm_gate_bias[l], mlstm_conv[l], mlstm_head_norm[l],
                          hgrn_lower_bounds, hgrn_head_norm[l], branch_w_mlstm, branch_w_hgrn, merge_w_gate,
                          merge_b_gate[l], mix_w_out)
        h, u = combine(y, h, mix_post_norm[l], 1.0, xattn_pre_norm[l])

        h, u = _xattn(u, h, mem2, xattn_mem_norm[l], batch, l, xattn_w_q, xattn_w_kv, xattn_w_o,
                      xattn_post_norm[l], ffn2_pre_norm[l])

        y = _ffn(u, ffn2_w_gate_up, ffn2_w_down, l)
        nxt = ffn1_pre_norm[l + 1] if l + 1 < depth else None
        h, u = combine(y, h, ffn2_post_norm[l], 0.5, nxt)
    return h.reshape(batch, s_len, d)
```

```python
import functools
import math
from typing import NamedTuple

import jax
import jax.numpy as jnp
from jax import lax
from jax.experimental import pallas as pl
from jax.experimental.pallas import tpu as pltpu

EPS = 1e-6
F32 = jnp.float32
BF16 = jnp.bfloat16

V7X_VMEM_BYTES = 64 * 1024 * 1024
VMEM_LIMIT = V7X_VMEM_BYTES - 8 * 1024 * 1024
LANES = 128
SUBLANES = 8

CONV_K = 4
M_HEADS = 8
M_DQK = 128
M_DV = 256
H_HEADS = 16
H_DK = 128
H_DV = 128
X_HEADS = 4
X_DH = 256
MLSTM_CHUNK = 256
HGRN_CHUNK = 64
CAST_BLOCK_ELEMS = 2 * 1024 * 1024
MM_TM = 1024
MM_TALL_TM = 2048
MAX_DOUBLE_BUFFERED_PANEL_BYTES = 24 * 1024 * 1024
MM_ROW_CHUNK = 256
MM_TN = 512
MM_WIDE_TN = 1024
FFN_UP_TN = 256
FFN_DOWN_TM = 512


def _cparams(sem):
    return pltpu.CompilerParams(dimension_semantics=sem, vmem_limit_bytes=VMEM_LIMIT)


def _dot(a, b):
    return jnp.dot(a, b, preferred_element_type=F32)


def _dot_nt(a, b):
    return lax.dot_general(a, b, (((1,), (1,)), ((), ())), preferred_element_type=F32)


def _dot_tn(a, b):
    return lax.dot_general(a, b, (((0,), (0,)), ((), ())), preferred_element_type=F32)


def _sigmoid(x):
    return 1.0 / (1.0 + jnp.exp(-x))


def _silu(x):
    return x * _sigmoid(x)


def _log_sigmoid(x):
    return jnp.minimum(x, 0.0) - jnp.log(1.0 + jnp.exp(-jnp.abs(x)))


def _rms(x):
    return x * lax.rsqrt(jnp.mean(x * x, axis=-1, keepdims=True) + EPS)


def _rmsnorm_kernel(x_ref, g_ref, o_ref):
    o_ref[...] = (_rms(x_ref[...]) * g_ref[...]).astype(o_ref.dtype)


def rmsnorm_bf16(x, g, tm=256):
    m, d = x.shape
    tm = min(tm, m)
    return pl.pallas_call(
        _rmsnorm_kernel,
        grid=(m // tm,),
        in_specs=[pl.BlockSpec((tm, d), lambda i: (i, 0)), pl.BlockSpec((1, d), lambda i: (0, 0))],
        out_specs=pl.BlockSpec((tm, d), lambda i: (i, 0)),
        out_shape=jax.ShapeDtypeStruct((m, d), BF16),
        compiler_params=_cparams(("parallel",)),
        name="rmsnorm_bf16",
    )(x, g.reshape(1, d))


def _cast_kernel(x_ref, o_ref):
    o_ref[...] = x_ref[...].astype(o_ref.dtype)


def _largest_divisor(n, cap, multiple):
    best = multiple
    for c in range(multiple, min(n, cap) + 1, multiple):
        if n % c == 0:
            best = c
    return best


def cast_tiled_bf16(w3, layer, tn):
    _, k, n = w3.shape
    tn = min(tn, n)
    assert n % tn == 0
    tk = _largest_divisor(k, max(SUBLANES, CAST_BLOCK_ELEMS // tn), 2 * SUBLANES)
    return pl.pallas_call(
        _cast_kernel,
        grid=(n // tn, k // tk),
        in_specs=[pl.BlockSpec((None, tk, tn), lambda j, i: (layer, i, j))],
        out_specs=pl.BlockSpec((None, tk, tn), lambda j, i: (j, i, 0)),
        out_shape=jax.ShapeDtypeStruct((n // tn, k, tn), BF16),
        compiler_params=_cparams(("parallel", "parallel")),
        name="cast_tiled_bf16",
    )(w3)


def _mm_kernel(*refs, n_w, epilogue, transposed, row_chunk, side_cast):
    a_ref = refs[0]
    w_refs = refs[1:1 + n_w]
    if side_cast:
        side_in, o_ref, side_out = refs[-3:]
        extra_refs = refs[1 + n_w:-3]
        tn_c = side_out.shape[2]
        for t in range(side_out.shape[0]):
            side_out[t] = side_in[:, t * tn_c:(t + 1) * tn_c].astype(side_out.dtype)
    else:
        o_ref = refs[-1]
        extra_refs = refs[1 + n_w:-1]
    ws = [(w[0] if transposed else w[...]).astype(BF16) for w in w_refs]
    mm = _dot_nt if transposed else _dot
    tm = a_ref.shape[0]
    rc = min(tm, row_chunk)
    for r0 in range(0, tm, rc):
        r = slice(r0, r0 + rc)
        accs = [mm(a_ref[r, :], w) for w in ws]
        ex = [e[...] if e.shape[0] == 1 else e[r, :] for e in extra_refs]
        o_ref[r, :] = epilogue(accs, ex).astype(o_ref.dtype)


class RawWeight(NamedTuple):
    w3: jax.Array
    layer: int
    tn: int


class RawWeightT(NamedTuple):
    w3t: jax.Array
    layer: int
    tn: int
    row0: int
    skip_at: int = 0
    skip_rows: int = 0


def matmul(a, w, n_out, *, tm, w_col_offsets=(0,), epilogue=None, extras=(), out_dtype=F32, side_cast=None,
           name="matmul"):
    m, k = a.shape
    tm = min(tm, m)
    row_chunk = MM_ROW_CHUNK if epilogue is not None else tm
    if epilogue is None:
        epilogue = lambda accs, ex: accs[0]
    a_mode = pl.Buffered(1) if 2 * tm * k * a.dtype.itemsize > MAX_DOUBLE_BUFFERED_PANEL_BYTES else None
    in_specs = [pl.BlockSpec((tm, k), lambda i, j: (i, 0), pipeline_mode=a_mode)]
    args = [a]
    transposed = isinstance(w, RawWeightT)
    tn = w.shape[2] if isinstance(w, jax.Array) else w.tn
    assert m % tm == 0 and n_out % tn == 0
    for off in w_col_offsets:
        assert off % tn == 0
        if transposed:
            assert w.w3t.shape[2] == k and w.row0 % SUBLANES == 0
            assert w.skip_at % tn == 0 and w.skip_rows % SUBLANES == 0

            def row_block(i, j, o):
                c = o + j * tn
                r = w.row0 + c + jnp.where(c >= w.skip_at, w.skip_rows, 0)
                return (w.layer, pl.multiple_of(r, SUBLANES), 0)

            spec = pl.BlockSpec((pl.Element(1), pl.Element(tn), pl.Element(k)), functools.partial(row_block, o=off))
        elif isinstance(w, RawWeight):
            assert w.w3.shape[1] == k
            spec = pl.BlockSpec((None, k, tn), functools.partial(lambda i, j, o: (w.layer, 0, j + o), o=off // tn))
        else:
            assert w.shape[1] == k
            spec = pl.BlockSpec((None, k, tn), functools.partial(lambda i, j, o: (j + o, 0, 0), o=off // tn))
        in_specs.append(spec)
        args.append(w if isinstance(w, jax.Array) else w[0])
    for arr, off in extras:
        assert off % tn == 0
        rows = arr.shape[0]
        if rows == 1:
            in_specs.append(pl.BlockSpec((1, tn), functools.partial(lambda i, j, o: (0, j + o), o=off // tn)))
        else:
            in_specs.append(pl.BlockSpec((tm, tn), functools.partial(lambda i, j, o: (i, j + o), o=off // tn)))
        args.append(arr)
    n_col = n_out // tn
    out_specs = pl.BlockSpec((tm, tn), lambda i, j: (i, j))
    out_shape = jax.ShapeDtypeStruct((m, n_out), out_dtype)
    if side_cast is not None:
        _, kc, nc = side_cast.w3.shape
        slab = kc // ((m // tm) * n_col)
        assert slab * (m // tm) * n_col == kc and slab % (2 * SUBLANES) == 0 and nc % side_cast.tn == 0
        in_specs.append(pl.BlockSpec((None, slab, nc), lambda i, j: (side_cast.layer, i * n_col + j, 0)))
        args.append(side_cast.w3)
        out_specs = [out_specs, pl.BlockSpec((nc // side_cast.tn, slab, side_cast.tn),
                                             lambda i, j: (0, i * n_col + j, 0))]
        out_shape = [out_shape, jax.ShapeDtypeStruct((nc // side_cast.tn, kc, side_cast.tn), BF16)]
    return pl.pallas_call(
        functools.partial(_mm_kernel, n_w=len(w_col_offsets), epilogue=epilogue, transposed=transposed,
                          row_chunk=row_chunk, side_cast=side_cast is not None),
        grid=(m // tm, n_col),
        in_specs=in_specs,
        out_specs=out_specs,
        out_shape=out_shape,
        compiler_params=_cparams(("parallel", "arbitrary")),
        name=name,
    )(*args)


def _epi_swiglu(accs, ex):
    return _silu(accs[0]) * accs[1]


def _epi_sigmoid_bias(accs, ex):
    return _sigmoid(accs[0] + ex[0])


def _epi_merge(accs, ex):
    gm, gh, ym = ex
    return gm.astype(F32) * ym + gh.astype(F32) * accs[0]


def _combine_kernel(y_ref, h_ref, post_ref, pre_ref, hn_ref, u_ref, *, scale):
    hn = h_ref[...] + scale * (_rms(y_ref[...].astype(F32)) * post_ref[...])
    hn_ref[...] = hn
    u_ref[...] = (_rms(hn) * pre_ref[...]).astype(u_ref.dtype)


def _combine_last_kernel(y_ref, h_ref, post_ref, hn_ref, *, scale):
    hn_ref[...] = h_ref[...] + scale * (_rms(y_ref[...].astype(F32)) * post_ref[...])


def combine(y, h, post, scale, next_pre=None, tm=256):
    m, d = y.shape
    tm = min(tm, m)
    row = pl.BlockSpec((tm, d), lambda i: (i, 0))
    vec = pl.BlockSpec((1, d), lambda i: (0, 0))
    if next_pre is None:
        return pl.pallas_call(
            functools.partial(_combine_last_kernel, scale=scale),
            grid=(m // tm,),
            in_specs=[row, row, vec],
            out_specs=row,
            out_shape=jax.ShapeDtypeStruct((m, d), F32),
            compiler_params=_cparams(("parallel",)),
            name="combine_last",
        )(y, h, post.reshape(1, d)), None
    return pl.pallas_call(
        functools.partial(_combine_kernel, scale=scale),
        grid=(m // tm,),
        in_specs=[row, row, vec, vec],
        out_specs=[row, row],
        out_shape=[jax.ShapeDtypeStruct((m, d), F32), jax.ShapeDtypeStruct((m, d), BF16)],
        compiler_params=_cparams(("parallel",)),
        name="combine",
    )(y, h, post.reshape(1, d), next_pre.reshape(1, d))


def _xattn_kernel(q_ref, k_ref, v_ref, o_ref, *, heads, dh):
    scale = dh ** -0.5
    for hd in range(heads):
        sl = slice(hd * dh, (hd + 1) * dh)
        s = _dot_nt(q_ref[:, sl], k_ref[:, sl]) * scale
        s = s - jnp.max(s, axis=-1, keepdims=True)
        p = jnp.exp(s)
        p = p / jnp.sum(p, axis=-1, keepdims=True)
        o_ref[:, sl] = _dot(p.astype(BF16), v_ref[:, sl]).astype(o_ref.dtype)


def _xattn_out_kernel(q_ref, k_ref, v_ref, wo_ref, h_ref, post_ref, pre_ref, hn_ref, u_ref, o_scr, *, heads, dh):
    _xattn_kernel(q_ref, k_ref, v_ref, o_scr, heads=heads, dh=dh)
    tm = o_scr.shape[0]
    for r0 in range(0, tm, tm // 2):
        r = slice(r0, r0 + tm // 2)
        y = _dot(o_scr[r, :], wo_ref[...])
        hn = h_ref[r, :] + _rms(y) * post_ref[...]
        hn_ref[r, :] = hn
        u_ref[r, :] = (_rms(hn) * pre_ref[...]).astype(u_ref.dtype)


def xattn_out(q, kv, w_o, h, post, next_pre, batch, heads, dh, tm=256):
    rows, width = q.shape
    d = h.shape[1]
    s_len = rows // batch
    n_mem = kv.shape[0] // batch
    tm = min(tm, s_len)
    nt = s_len // tm
    row = pl.BlockSpec((tm, d), lambda b, i: (b * nt + i, 0))
    vec = pl.BlockSpec((1, d), lambda b, i: (0, 0))
    return pl.pallas_call(
        functools.partial(_xattn_out_kernel, heads=heads, dh=dh),
        grid=(batch, nt),
        in_specs=[
            pl.BlockSpec((tm, width), lambda b, i: (b * nt + i, 0)),
            pl.BlockSpec((n_mem, width), lambda b, i: (b, 0)),
            pl.BlockSpec((n_mem, width), lambda b, i: (b, 1)),
            pl.BlockSpec((None, width, d), lambda b, i: (0, 0, 0), pipeline_mode=pl.Buffered(1)),
            row, vec, vec,
        ],
        out_specs=[row, row],
        out_shape=[jax.ShapeDtypeStruct((rows, d), F32), jax.ShapeDtypeStruct((rows, d), BF16)],
        scratch_shapes=[pltpu.VMEM((tm, width), BF16)],
        compiler_params=_cparams(("parallel", "arbitrary")),
        name="xattn_out",
    )(q, kv, kv, w_o, h, post.reshape(1, d), next_pre.reshape(1, d))


def _causal_conv_silu(x, tail, w):
    rows8 = lax.broadcasted_iota(jnp.int32, (SUBLANES, 1), 0)
    y = x * w[CONV_K - 1:CONV_K]
    for d in range(1, CONV_K):
        xs = pltpu.roll(x, d, 0)
        top = jnp.where(rows8 < d, pltpu.roll(tail, d, 0), xs[0:SUBLANES])
        xs = jnp.concatenate([top, xs[SUBLANES:]], axis=0)
        y = y + xs * w[CONV_K - 1 - d:CONV_K - d]
    return _silu(y)


def _mlstm_kernel(q_ref, k_ref, v_ref, og_ref, gc_ref, gr_ref, bc_ref, br_ref, conv_ref, norm_ref, out_ref,
                  c_scr, n_scr, m_scr, tq_scr, tk_scr, *, heads, dqk, dv):
    L = q_ref.shape[0]
    qk_w = heads * dqk

    @pl.when(pl.program_id(1) == 0)
    def _():
        c_scr[...] = jnp.zeros_like(c_scr)
        n_scr[...] = jnp.zeros_like(n_scr)
        m_scr[...] = jnp.zeros_like(m_scr)
        tq_scr[...] = jnp.zeros_like(tq_scr)
        tk_scr[...] = jnp.zeros_like(tk_scr)

    q_raw = q_ref[...]
    k_raw = k_ref[...]
    q_all = _causal_conv_silu(q_raw, tq_scr[...], conv_ref[:, 0:qk_w]) * (dqk ** -0.5)
    k_all = _causal_conv_silu(k_raw, tk_scr[...], conv_ref[:, qk_w:2 * qk_w])
    tq_scr[...] = q_raw[L - SUBLANES:L]
    tk_scr[...] = k_raw[L - SUBLANES:L]

    gc = gc_ref[...] + bc_ref[...]
    gr = gr_ref[0] + br_ref[...]
    t_idx = lax.broadcasted_iota(jnp.int32, (L, L), 0)
    s_idx = lax.broadcasted_iota(jnp.int32, (L, L), 1)
    causal = s_idx <= t_idx
    tri = causal.astype(F32)
    bc_all = jnp.dot(tri, _log_sigmoid(gc), precision=lax.Precision.HIGHEST, preferred_element_type=F32)
    br_all = lax.dot_general(_log_sigmoid(gr), tri, (((1,), (1,)), ((), ())),
                             precision=lax.Precision.HIGHEST, preferred_element_type=F32)

    for hd in range(heads):
        qs = slice(hd * dqk, (hd + 1) * dqk)
        vs = slice(hd * dv, (hd + 1) * dv)
        q = q_all[:, qs]
        k = k_all[:, qs]
        qb = q.astype(BF16)
        kb = k.astype(BF16)
        vb = v_ref[:, vs].astype(BF16)
        b_c = bc_all[:, heads + hd:heads + hd + 1]
        li_c = gc[:, hd:hd + 1]
        b_r = br_all[heads + hd:heads + hd + 1, :]
        li_r = gr[hd:hd + 1, :]
        m_prev = m_scr[hd:hd + 1, 0:1]
        n_prev = n_scr[hd:hd + 1, :]

        d = jnp.where(causal, b_c - b_r + li_r, -jnp.inf)
        m_inter = b_c + m_prev
        m_t = jnp.maximum(m_inter, jnp.max(d, axis=1, keepdims=True))
        w_intra = jnp.exp(d - m_t)
        w_inter = jnp.exp(m_inter - m_t)
        s = _dot_nt(qb, kb) * w_intra
        num = _dot(s.astype(BF16), vb) + w_inter * _dot(qb, c_scr[hd].astype(BF16))
        den = jnp.sum(s, axis=1, keepdims=True) + w_inter * jnp.sum(q * n_prev, axis=1, keepdims=True)
        hh = num * (1.0 / jnp.maximum(jnp.abs(den), jnp.exp(-m_t)))
        out = _rms(hh) * norm_ref[:, vs] * _sigmoid(og_ref[:, vs])
        out_ref[:, vs] = out.astype(out_ref.dtype)

        g = b_c[L - 1:L, :]
        a_c = g - b_c + li_c
        m_new = jnp.maximum(g + m_prev, jnp.max(a_c, axis=0, keepdims=True))
        decay = jnp.exp(g + m_prev - m_new)
        kw = k * jnp.exp(a_c - m_new)
        c_scr[hd] = decay * c_scr[hd] + _dot_tn(kw.astype(BF16), vb)
        n_scr[hd:hd + 1, :] = decay * n_prev + jnp.sum(kw, axis=0, keepdims=True)
        m_scr[hd:hd + 1, :] = jnp.broadcast_to(m_new, (1, LANES))


def mlstm_scan(proj, gates_col, gates_row, gate_bias, conv_w, head_norm, batch, *, chunk, heads=M_HEADS,
               dqk=M_DQK, dv=M_DV):
    rows = proj.shape[0]
    s_len = rows // batch
    L = min(chunk, s_len)
    nc = s_len // L
    qk_w = heads * dqk
    v_w = heads * dv
    assert qk_w % LANES == 0 and v_w % qk_w == 0
    bias_col = gate_bias.reshape(1, 2 * heads)
    bias_row = gate_bias.reshape(2 * heads, 1)
    kern = functools.partial(_mlstm_kernel, heads=heads, dqk=dqk, dv=dv)
    return pl.pallas_call(
        kern,
        grid=(batch, nc),
        in_specs=[
            pl.BlockSpec((L, qk_w), lambda b, c: (b * nc + c, 0)),
            pl.BlockSpec((L, qk_w), lambda b, c: (b * nc + c, 1)),
            pl.BlockSpec((L, v_w), lambda b, c: (b * nc + c, 2 * qk_w // v_w)),
            pl.BlockSpec((L, v_w), lambda b, c: (b * nc + c, 2 * qk_w // v_w + 1)),
            pl.BlockSpec((L, 2 * heads), lambda b, c: (b * nc + c, 0)),
            pl.BlockSpec((1, 2 * heads, L), lambda b, c: (b, 0, c)),
            pl.BlockSpec((1, 2 * heads), lambda b, c: (0, 0)),
            pl.BlockSpec((2 * heads, 1), lambda b, c: (0, 0)),
            pl.BlockSpec((CONV_K, 2 * qk_w), lambda b, c: (0, 0)),
            pl.BlockSpec((1, v_w), lambda b, c: (0, 0)),
        ],
        out_specs=pl.BlockSpec((L, v_w), lambda b, c: (b * nc + c, 0)),
        out_shape=jax.ShapeDtypeStruct((rows, v_w), BF16),
        scratch_shapes=[
            pltpu.VMEM((heads, dqk, dv), F32),
            pltpu.VMEM((heads, dqk), F32),
            pltpu.VMEM((heads, LANES), F32),
            pltpu.VMEM((SUBLANES, qk_w), F32),
            pltpu.VMEM((SUBLANES, qk_w), F32),
        ],
        compiler_params=_cparams(("parallel", "arbitrary")),
        name="mlstm_scan",
    )(proj, proj, proj, proj, gates_col, gates_row, bias_col, bias_row, conv_w, head_norm.reshape(1, v_w))


def _hgrn_kernel(q_ref, f_ref, i_ref, g_ref, lb_ref, norm_ref, out_ref, st_scr, *, heads, dk, dv, layer):
    L = q_ref.shape[0]

    @pl.when(pl.program_id(1) == 0)
    def _():
        st_scr[...] = jnp.zeros_like(st_scr)

    lbp = lb_ref[...]
    e = jnp.exp(lbp - jnp.max(lbp, axis=0, keepdims=True))
    p = e / jnp.sum(e, axis=0, keepdims=True)
    lb = jnp.zeros((1, lbp.shape[1]), F32)
    for j in range(1, layer + 1):
        lb = lb + p[j:j + 1]

    f = lb + (1.0 - lb) * _sigmoid(f_ref[...])
    lg = jnp.log(f)
    k_all = 1.0 - f
    t_idx = lax.broadcasted_iota(jnp.int32, (L, L), 0)
    s_idx = lax.broadcasted_iota(jnp.int32, (L, L), 1)
    tri = (s_idx <= t_idx).astype(F32)
    b_all = jnp.dot(tri, lg, precision=lax.Precision.HIGHEST, preferred_element_type=F32)
    level = jnp.where(t_idx > s_idx, 31 - lax.clz(t_idx ^ s_idx), jnp.where(t_idx == s_idx, -1, -2))
    row = lax.broadcasted_iota(jnp.int32, (L, 1), 0)
    n_levels = int(math.log2(L))
    k_slices = [slice(hd * dk, (hd + 1) * dk) for hd in range(heads)]
    q_all = q_ref[...]
    qb_all = q_all.astype(BF16)

    def scores(qx, kx, lw, acc):
        keep = level == lw
        return [jnp.where(keep, _dot_nt(qx[:, sl], kx[:, sl]), acc[i]) for i, sl in enumerate(k_slices)]

    a = scores(qb_all, k_all.astype(BF16), -1, [0.0] * heads)
    a = scores(qb_all, (k_all * pltpu.roll(f, L - 1, 0)).astype(BF16), 0, a)
    fw = jnp.where((row & 1) == 1, pltpu.roll(b_all, 1, 0), b_all)
    for lw in range(1, n_levels):
        w = 1 << lw
        odd = ((row >> lw) & 1) == 1
        gw = pltpu.roll(fw, L - w, 0)
        e = jnp.exp(jnp.where(odd, b_all, gw) - jnp.where(odd, fw, b_all))
        a = scores((q_all * e).astype(BF16), (k_all * e).astype(BF16), lw, a)
        fw = jnp.where(odd, pltpu.roll(fw, w, 0), fw)

    g = b_all[L - 1:L, :]
    qe_all = (q_all * jnp.exp(b_all)).astype(BF16)
    ke_all = (k_all * jnp.exp(g - b_all)).astype(BF16)
    eg = jnp.exp(g)
    ib_all = i_ref[...].astype(BF16)
    for hd in range(heads):
        ks = k_slices[hd]
        vs = slice(hd * dv, (hd + 1) * dv)
        st = st_scr[hd]
        o = _dot(a[hd].astype(BF16), ib_all[:, vs]) + _dot_nt(qe_all[:, ks], st.astype(BF16))
        st_scr[hd] = st * eg[:, ks] + _dot_tn(ib_all[:, vs], ke_all[:, ks])
        out = _rms(o) * norm_ref[:, vs] * _silu(g_ref[:, vs])
        out_ref[:, vs] = out.astype(out_ref.dtype)


def hgrn_scan(proj, col_block0, lower_bounds, head_norm, batch, layer, *, chunk, heads=H_HEADS, dk=H_DK, dv=H_DV):
    rows = proj.shape[0]
    s_len = rows // batch
    L = min(chunk, s_len)
    nc = s_len // L
    assert L & (L - 1) == 0 and dk == dv
    w = heads * dk
    depth = lower_bounds.shape[0]
    kern = functools.partial(_hgrn_kernel, heads=heads, dk=dk, dv=dv, layer=layer)
    blk = lambda o: pl.BlockSpec((L, w), functools.partial(lambda b, c, o: (b * nc + c, o), o=col_block0 + o))
    return pl.pallas_call(
        kern,
        grid=(batch, nc),
        in_specs=[blk(0), blk(1), blk(2), blk(3),
                  pl.BlockSpec((depth, w), lambda b, c: (0, 0)),
                  pl.BlockSpec((1, w), lambda b, c: (0, 0))],
        out_specs=pl.BlockSpec((L, w), lambda b, c: (b * nc + c, 0)),
        out_shape=jax.ShapeDtypeStruct((rows, w), BF16),
        scratch_shapes=[pltpu.VMEM((heads, dv, dk), F32)],
        compiler_params=_cparams(("parallel", "arbitrary")),
        name="hgrn_scan",
    )(proj, proj, proj, proj, lower_bounds, head_norm.reshape(1, w))


def _ffn(u, w_gate_up, w_down, layer):
    d_ff, d = w_down.shape[1:]
    act, w_dn = matmul(u, RawWeight(w_gate_up, layer, FFN_UP_TN), d_ff, tm=MM_TALL_TM, w_col_offsets=(0, d_ff),
                       epilogue=_epi_swiglu, out_dtype=BF16, side_cast=RawWeight(w_down, layer, MM_TN),
                       name="ffn_gate_up")
    return matmul(act, w_dn, d, tm=FFN_DOWN_TM, out_dtype=BF16, name="ffn_down")


def _token_mixers(u, layer, batch, w_in, gate_bias, conv_w, m_norm, lower_bounds, h_norm,
                  w_bm, w_bh, w_gate, b_gate, w_out):
    rows, d = u.shape
    s_len = rows // batch
    gate_lo = 2 * M_HEADS * M_DQK + 2 * M_HEADS * M_DV
    n_gates = 2 * M_HEADS
    w_in_t = jnp.swapaxes(w_in, 1, 2)
    proj = matmul(u, RawWeightT(w_in_t, layer, MM_TN, 0, gate_lo, n_gates), w_in.shape[2] - n_gates,
                  tm=MM_TALL_TM, name="mix_in_proj")
    gates_col = matmul(u, RawWeightT(w_in_t, layer, n_gates, gate_lo), n_gates, tm=MM_TM, name="mix_gate_proj")
    gates_row = gates_col.reshape(batch, s_len, n_gates).transpose(0, 2, 1)
    ym = mlstm_scan(proj, gates_col, gates_row, gate_bias, conv_w, m_norm, batch, chunk=MLSTM_CHUNK)
    yh = hgrn_scan(proj, gate_lo // (H_HEADS * H_DK), lower_bounds, h_norm, batch, layer, chunk=HGRN_CHUNK)
    gate = matmul(u, RawWeight(w_gate, layer, MM_TN), 2 * d, tm=MM_TM, epilogue=_epi_sigmoid_bias,
                  extras=((b_gate.reshape(1, 2 * d), 0),), out_dtype=BF16, name="merge_gate")
    y_m = matmul(ym, cast_tiled_bf16(w_bm, layer, MM_WIDE_TN), d, tm=MM_TM, name="branch_mlstm")
    z = matmul(yh, cast_tiled_bf16(w_bh, layer, MM_WIDE_TN), d, tm=MM_TM, epilogue=_epi_merge,
               extras=((gate, 0), (gate, d), (y_m, 0)), out_dtype=BF16, name="branch_hgrn_merge")
    return matmul(z, RawWeight(w_out, layer, MM_TN), d, tm=MM_TM, out_dtype=BF16, name="mix_out")


def _xattn(u, h, mem2, mem_norm, batch, layer, w_q, w_kv, w_o, post, next_pre):
    memn = rmsnorm_bf16(mem2, mem_norm)
    kv = matmul(memn, RawWeight(w_kv, layer, MM_TN), w_kv.shape[2], tm=MM_TM, out_dtype=BF16, name="xattn_kv")
    q = matmul(u, cast_tiled_bf16(w_q, layer, MM_WIDE_TN), w_q.shape[2], tm=MM_TM, out_dtype=BF16, name="xattn_q")
    return xattn_out(q, kv, cast_tiled_bf16(w_o, layer, w_o.shape[2]), h, post, next_pre, batch, X_HEADS, X_DH)


def kernel(x, mem, ffn1_pre_norm, ffn1_w_gate_up, ffn1_w_down, ffn1_post_norm, mix_pre_norm, mix_w_in, mlstm_gate_bias, mlstm_conv, mlstm_head_norm, hgrn_lower_bounds, hgrn_head_norm, branch_w_mlstm, branch_w_hgrn, merge_w_gate, merge_b_gate, mix_w_out, mix_post_norm, xattn_pre_norm, xattn_mem_norm, xattn_w_q, xattn_w_kv, xattn_w_o, xattn_post_norm, ffn2_pre_norm, ffn2_w_gate_up, ffn2_w_down, ffn2_post_norm):
    batch, s_len, d = x.shape
    n_mem = mem.shape[1]
    depth = ffn1_pre_norm.shape[0]

    h = x.reshape(batch * s_len, d)
    mem2 = mem.reshape(batch * n_mem, d)
    u = rmsnorm_bf16(h, ffn1_pre_norm[0])
    for l in range(depth):
        y = _ffn(u, ffn1_w_gate_up, ffn1_w_down, l)
        h, u = combine(y, h, ffn1_post_norm[l], 0.5, mix_pre_norm[l])

        y = _token_mixers(u, l, batch, mix_w_in, mlstm_gate_bias[l], mlstm_conv[l], mlstm_head_norm[l],
                          hgrn_lower_bounds, hgrn_head_norm[l], branch_w_mlstm, branch_w_hgrn, merge_w_gate,
                          merge_b_gate[l], mix_w_out)
        h, u = combine(y, h, mix_post_norm[l], 1.0, xattn_pre_norm[l])

        h, u = _xattn(u, h, mem2, xattn_mem_norm[l], batch, l, xattn_w_q, xattn_w_kv, xattn_w_o,
                      xattn_post_norm[l], ffn2_pre_norm[l])

        y = _ffn(u, ffn2_w_gate_up, ffn2_w_down, l)
        nxt = ffn1_pre_norm[l + 1] if l + 1 < depth else None
        h, u = combine(y, h, ffn2_post_norm[l], 0.5, nxt)
    return h.reshape(batch, s_len, d)
```

```python
import functools
import math
from typing import NamedTuple

import jax
import jax.numpy as jnp
from jax import lax
from jax.experimental import pallas as pl
from jax.experimental.pallas import tpu as pltpu

EPS = 1e-6
F32 = jnp.float32
BF16 = jnp.bfloat16

V7X_VMEM_BYTES = 64 * 1024 * 1024
VMEM_LIMIT = V7X_VMEM_BYTES - 8 * 1024 * 1024
LANES = 128
SUBLANES = 8

CONV_K = 4
M_HEADS = 8
M_DQK = 128
M_DV = 256
H_HEADS = 16
H_DK = 128
H_DV = 128
X_HEADS = 4
X_DH = 256
MLSTM_CHUNK = 256
HGRN_CHUNK = 64
CAST_BLOCK_ELEMS = 2 * 1024 * 1024
ROW_TILE = 256
MM_TM = 1024
MM_TALL_TM = 2048
MAX_DOUBLE_BUFFERED_PANEL_BYTES = 24 * 1024 * 1024
MM_ROW_CHUNK = 256
MM_TN = 512
MM_WIDE_TN = 1024
FFN_UP_TN = 256
FFN_DOWN_TM = 512


def _cparams(sem):
    return pltpu.CompilerParams(dimension_semantics=sem, vmem_limit_bytes=VMEM_LIMIT)


def _dot(a, b):
    return jnp.dot(a, b, preferred_element_type=F32)


def _dot_nt(a, b):
    return lax.dot_general(a, b, (((1,), (1,)), ((), ())), preferred_element_type=F32)


def _dot_tn(a, b):
    return lax.dot_general(a, b, (((0,), (0,)), ((), ())), preferred_element_type=F32)


def _sigmoid(x):
    return 1.0 / (1.0 + jnp.exp(-x))


def _silu(x):
    return x * _sigmoid(x)


def _log_sigmoid(x):
    return jnp.minimum(x, 0.0) - jnp.log(1.0 + jnp.exp(-jnp.abs(x)))


def _rms(x):
    return x * lax.rsqrt(jnp.mean(x * x, axis=-1, keepdims=True) + EPS)


def _rmsnorm_kernel(x_ref, g_ref, o_ref):
    o_ref[...] = (_rms(x_ref[...]) * g_ref[...]).astype(o_ref.dtype)


def rmsnorm_bf16(x, g):
    m, d = x.shape
    tm = min(ROW_TILE, m)
    return pl.pallas_call(
        _rmsnorm_kernel,
        grid=(m // tm,),
        in_specs=[pl.BlockSpec((tm, d), lambda i: (i, 0)), pl.BlockSpec((1, d), lambda i: (0, 0))],
        out_specs=pl.BlockSpec((tm, d), lambda i: (i, 0)),
        out_shape=jax.ShapeDtypeStruct((m, d), BF16),
        compiler_params=_cparams(("parallel",)),
        name="rmsnorm_bf16",
    )(x, g.reshape(1, d))


def _cast_kernel(x_ref, o_ref):
    o_ref[...] = x_ref[...].astype(o_ref.dtype)


def _largest_divisor(n, cap, multiple):
    best = multiple
    for c in range(multiple, min(n, cap) + 1, multiple):
        if n % c == 0:
            best = c
    return best


def cast_tiled_bf16(w3, layer, tn):
    _, k, n = w3.shape
    tn = min(tn, n)
    assert n % tn == 0
    tk = _largest_divisor(k, max(SUBLANES, CAST_BLOCK_ELEMS // tn), 2 * SUBLANES)
    return pl.pallas_call(
        _cast_kernel,
        grid=(n // tn, k // tk),
        in_specs=[pl.BlockSpec((None, tk, tn), lambda j, i: (layer, i, j))],
        out_specs=pl.BlockSpec((None, tk, tn), lambda j, i: (j, i, 0)),
        out_shape=jax.ShapeDtypeStruct((n // tn, k, tn), BF16),
        compiler_params=_cparams(("parallel", "parallel")),
        name="cast_tiled_bf16",
    )(w3)


def _mm_kernel(*refs, n_w, epilogue, transposed, row_chunk, side_cast):
    a_ref = refs[0]
    w_refs = refs[1:1 + n_w]
    if side_cast:
        side_in, o_ref, side_out = refs[-3:]
        extra_refs = refs[1 + n_w:-3]
        tn_c = side_out.shape[2]
        for t in range(side_out.shape[0]):
            side_out[t] = side_in[:, t * tn_c:(t + 1) * tn_c].astype(side_out.dtype)
    else:
        o_ref = refs[-1]
        extra_refs = refs[1 + n_w:-1]
    ws = [(w[0] if transposed else w[...]).astype(BF16) for w in w_refs]
    mm = _dot_nt if transposed else _dot
    tm = a_ref.shape[0]
    rc = min(tm, row_chunk)
    for r0 in range(0, tm, rc):
        r = slice(r0, r0 + rc)
        accs = [mm(a_ref[r, :], w) for w in ws]
        ex = [e[...] if e.shape[0] == 1 else e[r, :] for e in extra_refs]
        o_ref[r, :] = epilogue(accs, ex).astype(o_ref.dtype)


class RawWeight(NamedTuple):
    w3: jax.Array
    layer: int
    tn: int


class RawWeightT(NamedTuple):
    w3t: jax.Array
    layer: int
    tn: int
    row0: int
    skip_at: int = 0
    skip_rows: int = 0


def matmul(a, w, n_out, *, tm, w_col_offsets=(0,), epilogue=None, extras=(), out_dtype=F32, side_cast=None,
           name="matmul"):
    m, k = a.shape
    tm = min(tm, m)
    row_chunk = MM_ROW_CHUNK if epilogue is not None else tm
    if epilogue is None:
        epilogue = lambda accs, ex: accs[0]
    a_mode = pl.Buffered(1) if 2 * tm * k * a.dtype.itemsize > MAX_DOUBLE_BUFFERED_PANEL_BYTES else None
    in_specs = [pl.BlockSpec((tm, k), lambda i, j: (i, 0), pipeline_mode=a_mode)]
    args = [a]
    transposed = isinstance(w, RawWeightT)
    tn = w.shape[2] if isinstance(w, jax.Array) else w.tn
    assert m % tm == 0 and n_out % tn == 0
    for off in w_col_offsets:
        assert off % tn == 0
        if transposed:
            assert w.w3t.shape[2] == k and w.row0 % SUBLANES == 0
            assert w.skip_at % tn == 0 and w.skip_rows % SUBLANES == 0

            def row_block(i, j, o):
                c = o + j * tn
                r = w.row0 + c + jnp.where(c >= w.skip_at, w.skip_rows, 0)
                return (w.layer, pl.multiple_of(r, SUBLANES), 0)

            spec = pl.BlockSpec((pl.Element(1), pl.Element(tn), pl.Element(k)), functools.partial(row_block, o=off))
        elif isinstance(w, RawWeight):
            assert w.w3.shape[1] == k
            spec = pl.BlockSpec((None, k, tn), functools.partial(lambda i, j, o: (w.layer, 0, j + o), o=off // tn))
        else:
            assert w.shape[1] == k
            spec = pl.BlockSpec((None, k, tn), functools.partial(lambda i, j, o: (j + o, 0, 0), o=off // tn))
        in_specs.append(spec)
        args.append(w if isinstance(w, jax.Array) else w[0])
    for arr, off in extras:
        assert off % tn == 0
        rows = arr.shape[0]
        if rows == 1:
            in_specs.append(pl.BlockSpec((1, tn), functools.partial(lambda i, j, o: (0, j + o), o=off // tn)))
        else:
            in_specs.append(pl.BlockSpec((tm, tn), functools.partial(lambda i, j, o: (i, j + o), o=off // tn)))
        args.append(arr)
    n_col = n_out // tn
    out_specs = pl.BlockSpec((tm, tn), lambda i, j: (i, j))
    out_shape = jax.ShapeDtypeStruct((m, n_out), out_dtype)
    if side_cast is not None:
        _, kc, nc = side_cast.w3.shape
        slab = kc // ((m // tm) * n_col)
        assert slab * (m // tm) * n_col == kc and slab % (2 * SUBLANES) == 0 and nc % side_cast.tn == 0
        in_specs.append(pl.BlockSpec((None, slab, nc), lambda i, j: (side_cast.layer, i * n_col + j, 0)))
        args.append(side_cast.w3)
        out_specs = [out_specs, pl.BlockSpec((nc // side_cast.tn, slab, side_cast.tn),
                                             lambda i, j: (0, i * n_col + j, 0))]
        out_shape = [out_shape, jax.ShapeDtypeStruct((nc // side_cast.tn, kc, side_cast.tn), BF16)]
    return pl.pallas_call(
        functools.partial(_mm_kernel, n_w=len(w_col_offsets), epilogue=epilogue, transposed=transposed,
                          row_chunk=row_chunk, side_cast=side_cast is not None),
        grid=(m // tm, n_col),
        in_specs=in_specs,
        out_specs=out_specs,
        out_shape=out_shape,
        compiler_params=_cparams(("parallel", "arbitrary")),
        name=name,
    )(*args)


def _epi_swiglu(accs, ex):
    return _silu(accs[0]) * accs[1]


def _epi_sigmoid_bias(accs, ex):
    return _sigmoid(accs[0] + ex[0])


def _epi_merge(accs, ex):
    gm, gh, ym = ex
    return gm.astype(F32) * ym + gh.astype(F32) * accs[0]


def _combine_kernel(y_ref, h_ref, post_ref, pre_ref, hn_ref, u_ref, *, scale):
    hn = h_ref[...] + scale * (_rms(y_ref[...].astype(F32)) * post_ref[...])
    hn_ref[...] = hn
    u_ref[...] = (_rms(hn) * pre_ref[...]).astype(u_ref.dtype)


def _combine_last_kernel(y_ref, h_ref, post_ref, hn_ref, *, scale):
    hn_ref[...] = h_ref[...] + scale * (_rms(y_ref[...].astype(F32)) * post_ref[...])


def combine(y, h, post, scale, next_pre=None):
    m, d = y.shape
    tm = min(ROW_TILE, m)
    row = pl.BlockSpec((tm, d), lambda i: (i, 0))
    vec = pl.BlockSpec((1, d), lambda i: (0, 0))
    if next_pre is None:
        return pl.pallas_call(
            functools.partial(_combine_last_kernel, scale=scale),
            grid=(m // tm,),
            in_specs=[row, row, vec],
            out_specs=row,
            out_shape=jax.ShapeDtypeStruct((m, d), F32),
            compiler_params=_cparams(("parallel",)),
            name="combine_last",
        )(y, h, post.reshape(1, d)), None
    return pl.pallas_call(
        functools.partial(_combine_kernel, scale=scale),
        grid=(m // tm,),
        in_specs=[row, row, vec, vec],
        out_specs=[row, row],
        out_shape=[jax.ShapeDtypeStruct((m, d), F32), jax.ShapeDtypeStruct((m, d), BF16)],
        compiler_params=_cparams(("parallel",)),
        name="combine",
    )(y, h, post.reshape(1, d), next_pre.reshape(1, d))


def _xattn_kernel(q_ref, k_ref, v_ref, o_ref, *, heads, dh):
    scale = dh ** -0.5
    for hd in range(heads):
        sl = slice(hd * dh, (hd + 1) * dh)
        s = _dot_nt(q_ref[:, sl], k_ref[:, sl]) * scale
        s = s - jnp.max(s, axis=-1, keepdims=True)
        p = jnp.exp(s)
        p = p / jnp.sum(p, axis=-1, keepdims=True)
        o_ref[:, sl] = _dot(p.astype(BF16), v_ref[:, sl]).astype(o_ref.dtype)


def _xattn_out_kernel(q_ref, k_ref, v_ref, wo_ref, h_ref, post_ref, pre_ref, hn_ref, u_ref, o_scr, *, heads, dh):
    _xattn_kernel(q_ref, k_ref, v_ref, o_scr, heads=heads, dh=dh)
    tm = o_scr.shape[0]
    for r0 in range(0, tm, tm // 2):
        r = slice(r0, r0 + tm // 2)
        y = _dot(o_scr[r, :], wo_ref[...])
        hn = h_ref[r, :] + _rms(y) * post_ref[...]
        hn_ref[r, :] = hn
        u_ref[r, :] = (_rms(hn) * pre_ref[...]).astype(u_ref.dtype)


def xattn_out(q, kv, w_o, h, post, next_pre, batch, heads, dh):
    rows, width = q.shape
    d = h.shape[1]
    s_len = rows // batch
    n_mem = kv.shape[0] // batch
    tm = min(ROW_TILE, s_len)
    nt = s_len // tm
    row = pl.BlockSpec((tm, d), lambda b, i: (b * nt + i, 0))
    vec = pl.BlockSpec((1, d), lambda b, i: (0, 0))
    return pl.pallas_call(
        functools.partial(_xattn_out_kernel, heads=heads, dh=dh),
        grid=(batch, nt),
        in_specs=[
            pl.BlockSpec((tm, width), lambda b, i: (b * nt + i, 0)),
            pl.BlockSpec((n_mem, width), lambda b, i: (b, 0)),
            pl.BlockSpec((n_mem, width), lambda b, i: (b, 1)),
            pl.BlockSpec((None, width, d), lambda b, i: (0, 0, 0), pipeline_mode=pl.Buffered(1)),
            row, vec, vec,
        ],
        out_specs=[row, row],
        out_shape=[jax.ShapeDtypeStruct((rows, d), F32), jax.ShapeDtypeStruct((rows, d), BF16)],
        scratch_shapes=[pltpu.VMEM((tm, width), BF16)],
        compiler_params=_cparams(("parallel", "arbitrary")),
        name="xattn_out",
    )(q, kv, kv, w_o, h, post.reshape(1, d), next_pre.reshape(1, d))


def _causal_conv_silu(x, tail, w):
    rows8 = lax.broadcasted_iota(jnp.int32, (SUBLANES, 1), 0)
    y = x * w[CONV_K - 1:CONV_K]
    for d in range(1, CONV_K):
        xs = pltpu.roll(x, d, 0)
        top = jnp.where(rows8 < d, pltpu.roll(tail, d, 0), xs[0:SUBLANES])
        xs = jnp.concatenate([top, xs[SUBLANES:]], axis=0)
        y = y + xs * w[CONV_K - 1 - d:CONV_K - d]
    return _silu(y)


def _mlstm_kernel(q_ref, k_ref, v_ref, og_ref, gc_ref, gr_ref, bc_ref, br_ref, conv_ref, norm_ref, out_ref,
                  c_scr, n_scr, m_scr, tq_scr, tk_scr, *, heads, dqk, dv):
    L = q_ref.shape[0]
    qk_w = heads * dqk

    @pl.when(pl.program_id(1) == 0)
    def _():
        c_scr[...] = jnp.zeros_like(c_scr)
        n_scr[...] = jnp.zeros_like(n_scr)
        m_scr[...] = jnp.zeros_like(m_scr)
        tq_scr[...] = jnp.zeros_like(tq_scr)
        tk_scr[...] = jnp.zeros_like(tk_scr)

    q_raw = q_ref[...]
    k_raw = k_ref[...]
    q_all = _causal_conv_silu(q_raw, tq_scr[...], conv_ref[:, 0:qk_w]) * (dqk ** -0.5)
    k_all = _causal_conv_silu(k_raw, tk_scr[...], conv_ref[:, qk_w:2 * qk_w])
    tq_scr[...] = q_raw[L - SUBLANES:L]
    tk_scr[...] = k_raw[L - SUBLANES:L]

    gc = gc_ref[...] + bc_ref[...]
    gr = gr_ref[0] + br_ref[...]
    t_idx = lax.broadcasted_iota(jnp.int32, (L, L), 0)
    s_idx = lax.broadcasted_iota(jnp.int32, (L, L), 1)
    causal = s_idx <= t_idx
    tri = causal.astype(F32)
    bc_all = jnp.dot(tri, _log_sigmoid(gc), precision=lax.Precision.HIGHEST, preferred_element_type=F32)
    br_all = lax.dot_general(_log_sigmoid(gr), tri, (((1,), (1,)), ((), ())),
                             precision=lax.Precision.HIGHEST, preferred_element_type=F32)

    for hd in range(heads):
        qs = slice(hd * dqk, (hd + 1) * dqk)
        vs = slice(hd * dv, (hd + 1) * dv)
        q = q_all[:, qs]
        k = k_all[:, qs]
        qb = q.astype(BF16)
        kb = k.astype(BF16)
        vb = v_ref[:, vs].astype(BF16)
        b_c = bc_all[:, heads + hd:heads + hd + 1]
        li_c = gc[:, hd:hd + 1]
        b_r = br_all[heads + hd:heads + hd + 1, :]
        li_r = gr[hd:hd + 1, :]
        m_prev = m_scr[hd:hd + 1, 0:1]
        n_prev = n_scr[hd:hd + 1, :]

        d = jnp.where(causal, b_c - b_r + li_r, -jnp.inf)
        m_inter = b_c + m_prev
        m_t = jnp.maximum(m_inter, jnp.max(d, axis=1, keepdims=True))
        w_intra = jnp.exp(d - m_t)
        w_inter = jnp.exp(m_inter - m_t)
        s = _dot_nt(qb, kb) * w_intra
        num = _dot(s.astype(BF16), vb) + w_inter * _dot(qb, c_scr[hd].astype(BF16))
        den = jnp.sum(s, axis=1, keepdims=True) + w_inter * jnp.sum(q * n_prev, axis=1, keepdims=True)
        hh = num * (1.0 / jnp.maximum(jnp.abs(den), jnp.exp(-m_t)))
        out = _rms(hh) * norm_ref[:, vs] * _sigmoid(og_ref[:, vs])
        out_ref[:, vs] = out.astype(out_ref.dtype)

        g = b_c[L - 1:L, :]
        a_c = g - b_c + li_c
        m_new = jnp.maximum(g + m_prev, jnp.max(a_c, axis=0, keepdims=True))
        decay = jnp.exp(g + m_prev - m_new)
        kw = k * jnp.exp(a_c - m_new)
        c_scr[hd] = decay * c_scr[hd] + _dot_tn(kw.astype(BF16), vb)
        n_scr[hd:hd + 1, :] = decay * n_prev + jnp.sum(kw, axis=0, keepdims=True)
        m_scr[hd:hd + 1, :] = jnp.broadcast_to(m_new, (1, LANES))


def mlstm_scan(proj, gates_col, gates_row, gate_bias, conv_w, head_norm, batch, *, chunk, heads=M_HEADS,
               dqk=M_DQK, dv=M_DV):
    rows = proj.shape[0]
    s_len = rows // batch
    L = min(chunk, s_len)
    nc = s_len // L
    qk_w = heads * dqk
    v_w = heads * dv
    assert qk_w % LANES == 0 and v_w % qk_w == 0
    bias_col = gate_bias.reshape(1, 2 * heads)
    bias_row = gate_bias.reshape(2 * heads, 1)
    kern = functools.partial(_mlstm_kernel, heads=heads, dqk=dqk, dv=dv)
    return pl.pallas_call(
        kern,
        grid=(batch, nc),
        in_specs=[
            pl.BlockSpec((L, qk_w), lambda b, c: (b * nc + c, 0)),
            pl.BlockSpec((L, qk_w), lambda b, c: (b * nc + c, 1)),
            pl.BlockSpec((L, v_w), lambda b, c: (b * nc + c, 2 * qk_w // v_w)),
            pl.BlockSpec((L, v_w), lambda b, c: (b * nc + c, 2 * qk_w // v_w + 1)),
            pl.BlockSpec((L, 2 * heads), lambda b, c: (b * nc + c, 0)),
            pl.BlockSpec((1, 2 * heads, L), lambda b, c: (b, 0, c)),
            pl.BlockSpec((1, 2 * heads), lambda b, c: (0, 0)),
            pl.BlockSpec((2 * heads, 1), lambda b, c: (0, 0)),
            pl.BlockSpec((CONV_K, 2 * qk_w), lambda b, c: (0, 0)),
            pl.BlockSpec((1, v_w), lambda b, c: (0, 0)),
        ],
        out_specs=pl.BlockSpec((L, v_w), lambda b, c: (b * nc + c, 0)),
        out_shape=jax.ShapeDtypeStruct((rows, v_w), BF16),
        scratch_shapes=[
            pltpu.VMEM((heads, dqk, dv), F32),
            pltpu.VMEM((heads, dqk), F32),
            pltpu.VMEM((heads, LANES), F32),
            pltpu.VMEM((SUBLANES, qk_w), F32),
            pltpu.VMEM((SUBLANES, qk_w), F32),
        ],
        compiler_params=_cparams(("parallel", "arbitrary")),
        name="mlstm_scan",
    )(proj, proj, proj, proj, gates_col, gates_row, bias_col, bias_row, conv_w, head_norm.reshape(1, v_w))


def _hgrn_kernel(q_ref, f_ref, i_ref, g_ref, lb_ref, norm_ref, out_ref, st_scr, *, heads, dk, dv, layer):
    L = q_ref.shape[0]

    @pl.when(pl.program_id(1) == 0)
    def _():
        st_scr[...] = jnp.zeros_like(st_scr)

    lbp = lb_ref[...]
    e = jnp.exp(lbp - jnp.max(lbp, axis=0, keepdims=True))
    p = e / jnp.sum(e, axis=0, keepdims=True)
    lb = jnp.zeros((1, lbp.shape[1]), F32)
    for j in range(1, layer + 1):
        lb = lb + p[j:j + 1]

    f = lb + (1.0 - lb) * _sigmoid(f_ref[...])
    lg = jnp.log(f)
    k_all = 1.0 - f
    t_idx = lax.broadcasted_iota(jnp.int32, (L, L), 0)
    s_idx = lax.broadcasted_iota(jnp.int32, (L, L), 1)
    tri = (s_idx <= t_idx).astype(F32)
    b_all = jnp.dot(tri, lg, precision=lax.Precision.HIGHEST, preferred_element_type=F32)
    level = jnp.where(t_idx > s_idx, 31 - lax.clz(t_idx ^ s_idx), jnp.where(t_idx == s_idx, -1, -2))
    row = lax.broadcasted_iota(jnp.int32, (L, 1), 0)
    n_levels = int(math.log2(L))
    k_slices = [slice(hd * dk, (hd + 1) * dk) for hd in range(heads)]
    q_all = q_ref[...]
    qb_all = q_all.astype(BF16)

    def scores(qx, kx, lw, acc):
        keep = level == lw
        return [jnp.where(keep, _dot_nt(qx[:, sl], kx[:, sl]), acc[i]) for i, sl in enumerate(k_slices)]

    a = scores(qb_all, k_all.astype(BF16), -1, [0.0] * heads)
    a = scores(qb_all, (k_all * pltpu.roll(f, L - 1, 0)).astype(BF16), 0, a)
    fw = jnp.where((row & 1) == 1, pltpu.roll(b_all, 1, 0), b_all)
    for lw in range(1, n_levels):
        w = 1 << lw
        odd = ((row >> lw) & 1) == 1
        gw = pltpu.roll(fw, L - w, 0)
        e = jnp.exp(jnp.where(odd, b_all, gw) - jnp.where(odd, fw, b_all))
        a = scores((q_all * e).astype(BF16), (k_all * e).astype(BF16), lw, a)
        fw = jnp.where(odd, pltpu.roll(fw, w, 0), fw)

    g = b_all[L - 1:L, :]
    qe_all = (q_all * jnp.exp(b_all)).astype(BF16)
    ke_all = (k_all * jnp.exp(g - b_all)).astype(BF16)
    eg = jnp.exp(g)
    ib_all = i_ref[...].astype(BF16)
    for hd in range(heads):
        ks = k_slices[hd]
        vs = slice(hd * dv, (hd + 1) * dv)
        st = st_scr[hd]
        o = _dot(a[hd].astype(BF16), ib_all[:, vs]) + _dot_nt(qe_all[:, ks], st.astype(BF16))
        st_scr[hd] = st * eg[:, ks] + _dot_tn(ib_all[:, vs], ke_all[:, ks])
        out = _rms(o) * norm_ref[:, vs] * _silu(g_ref[:, vs])
        out_ref[:, vs] = out.astype(out_ref.dtype)


def hgrn_scan(proj, col_block0, lower_bounds, head_norm, batch, layer, *, chunk, heads=H_HEADS, dk=H_DK, dv=H_DV):
    rows = proj.shape[0]
    s_len = rows // batch
    L = min(chunk, s_len)
    nc = s_len // L
    assert L & (L - 1) == 0 and dk == dv
    w = heads * dk
    depth = lower_bounds.shape[0]
    kern = functools.partial(_hgrn_kernel, heads=heads, dk=dk, dv=dv, layer=layer)
    blk = lambda o: pl.BlockSpec((L, w), functools.partial(lambda b, c, o: (b * nc + c, o), o=col_block0 + o))
    return pl.pallas_call(
        kern,
        grid=(batch, nc),
        in_specs=[blk(0), blk(1), blk(2), blk(3),
                  pl.BlockSpec((depth, w), lambda b, c: (0, 0)),
                  pl.BlockSpec((1, w), lambda b, c: (0, 0))],
        out_specs=pl.BlockSpec((L, w), lambda b, c: (b * nc + c, 0)),
        out_shape=jax.ShapeDtypeStruct((rows, w), BF16),
        scratch_shapes=[pltpu.VMEM((heads, dv, dk), F32)],
        compiler_params=_cparams(("parallel", "arbitrary")),
        name="hgrn_scan",
    )(proj, proj, proj, proj, lower_bounds, head_norm.reshape(1, w))


def _ffn(u, w_gate_up, w_down, layer):
    d_ff, d = w_down.shape[1:]
    act, w_dn = matmul(u, RawWeight(w_gate_up, layer, FFN_UP_TN), d_ff, tm=MM_TALL_TM, w_col_offsets=(0, d_ff),
                       epilogue=_epi_swiglu, out_dtype=BF16, side_cast=RawWeight(w_down, layer, MM_TN),
                       name="ffn_gate_up")
    return matmul(act, w_dn, d, tm=FFN_DOWN_TM, out_dtype=BF16, name="ffn_down")


def _token_mixers(u, layer, batch, w_in, gate_bias, conv_w, m_norm, lower_bounds, h_norm,
                  w_bm, w_bh, w_gate, b_gate, w_out):
    rows, d = u.shape
    s_len = rows // batch
    gate_lo = 2 * M_HEADS * M_DQK + 2 * M_HEADS * M_DV
    n_gates = 2 * M_HEADS
    w_in_t = jnp.swapaxes(w_in, 1, 2)
    proj = matmul(u, RawWeightT(w_in_t, layer, MM_TN, 0, gate_lo, n_gates), w_in.shape[2] - n_gates,
                  tm=MM_TALL_TM, name="mix_in_proj")
    gates_col = matmul(u, RawWeightT(w_in_t, layer, n_gates, gate_lo), n_gates, tm=MM_TM, name="mix_gate_proj")
    gates_row = gates_col.reshape(batch, s_len, n_gates).transpose(0, 2, 1)
    ym = mlstm_scan(proj, gates_col, gates_row, gate_bias, conv_w, m_norm, batch, chunk=MLSTM_CHUNK)
    yh = hgrn_scan(proj, gate_lo // (H_HEADS * H_DK), lower_bounds, h_norm, batch, layer, chunk=HGRN_CHUNK)
    gate = matmul(u, RawWeight(w_gate, layer, MM_TN), 2 * d, tm=MM_TM, epilogue=_epi_sigmoid_bias,
                  extras=((b_gate.reshape(1, 2 * d), 0),), out_dtype=BF16, name="merge_gate")
    y_m = matmul(ym, cast_tiled_bf16(w_bm, layer, MM_WIDE_TN), d, tm=MM_TM, name="branch_mlstm")
    z = matmul(yh, cast_tiled_bf16(w_bh, layer, MM_WIDE_TN), d, tm=MM_TM, epilogue=_epi_merge,
               extras=((gate, 0), (gate, d), (y_m, 0)), out_dtype=BF16, name="branch_hgrn_merge")
    return matmul(z, RawWeight(w_out, layer, MM_TN), d, tm=MM_TM, out_dtype=BF16, name="mix_out")


def _xattn(u, h, mem2, mem_norm, batch, layer, w_q, w_kv, w_o, post, next_pre):
    memn = rmsnorm_bf16(mem2, mem_norm)
    kv = matmul(memn, RawWeight(w_kv, layer, MM_TN), w_kv.shape[2], tm=MM_TM, out_dtype=BF16, name="xattn_kv")
    q = matmul(u, cast_tiled_bf16(w_q, layer, MM_WIDE_TN), w_q.shape[2], tm=MM_TM, out_dtype=BF16, name="xattn_q")
    return xattn_out(q, kv, cast_tiled_bf16(w_o, layer, w_o.shape[2]), h, post, next_pre, batch, X_HEADS, X_DH)


def kernel(x, mem, ffn1_pre_norm, ffn1_w_gate_up, ffn1_w_down, ffn1_post_norm, mix_pre_norm, mix_w_in, mlstm_gate_bias, mlstm_conv, mlstm_head_norm, hgrn_lower_bounds, hgrn_head_norm, branch_w_mlstm, branch_w_hgrn, merge_w_gate, merge_b_gate, mix_w_out, mix_post_norm, xattn_pre_norm, xattn_mem_norm, xattn_w_q, xattn_w_kv, xattn_w_o, xattn_post_norm, ffn2_pre_norm, ffn2_w_gate_up, ffn2_w_down, ffn2_post_norm):
    batch, s_len, d = x.shape
    n_mem = mem.shape[1]
    depth = ffn1_pre_norm.shape[0]

    h = x.reshape(batch * s_len, d)
    mem2 = mem.reshape(batch * n_mem, d)
    u = rmsnorm_bf16(h, ffn1_pre_norm[0])
    for l in range(depth):
        y = _ffn(u, ffn1_w_gate_up, ffn1_w_down, l)
        h, u = combine(y, h, ffn1_post_norm[l], 0.5, mix_pre_norm[l])

        y = _token_mixers(u, l, batch, mix_w_in, mlstm_gate_bias[l], mlstm_conv[l], mlstm_head_norm[l],
                          hgrn_lower_bounds, hgrn_head_norm[l], branch_w_mlstm, branch_w_hgrn, merge_w_gate,
                          merge_b_gate[l], mix_w_out)
        h, u = combine(y, h, mix_post_norm[l], 1.0, xattn_pre_norm[l])

        h, u = _xattn(u, h, mem2, xattn_mem_norm[l], batch, l, xattn_w_q, xattn_w_kv, xattn_w_o,
                      xattn_post_norm[l], ffn2_pre_norm[l])

        y = _ffn(u, ffn2_w_gate_up, ffn2_w_down, l)
        nxt = ffn1_pre_norm[l + 1] if l + 1 < depth else None
        h, u = combine(y, h, ffn2_post_norm[l], 0.5, nxt)
    return h.reshape(batch, s_len, d)
```

```python
import functools
import math
from typing import NamedTuple

import jax
import jax.numpy as jnp
from jax import lax
from jax.experimental import pallas as pl
from jax.experimental.pallas import tpu as pltpu

EPS = 1e-6
F32 = jnp.float32
BF16 = jnp.bfloat16

V7X_VMEM_BYTES = 64 * 1024 * 1024
VMEM_LIMIT = V7X_VMEM_BYTES - 4 * 1024 * 1024
F32_BYTES = 4
BF16_BYTES = 2
LANES = 128
SUBLANES = 8

CONV_K = 4
M_HEADS = 8
M_DQK = 128
M_DV = 256
H_HEADS = 16
H_DK = 128
H_DV = 128
X_HEADS = 4
X_DH = 256
MLSTM_CHUNK = 256
HGRN_CHUNK = 64
CAST_BLOCK_ELEMS = 2 * 1024 * 1024
ROW_TILE = 256
ROW_PANELS = (2048, 1024, 512)
MIN_COLS_FOR_SINGLE_BUFFERED_PANEL = 24
VMEM_ESTIMATE_SLACK = 2 * 1024 * 1024
MM_ROW_CHUNK = 256
MM_TN = 512
MM_WIDE_TN = 1024
FFN_UP_TN = 256


def _cparams(sem):
    return pltpu.CompilerParams(dimension_semantics=sem, vmem_limit_bytes=VMEM_LIMIT)


def _dot(a, b):
    return jnp.dot(a, b, preferred_element_type=F32)


def _dot_nt(a, b):
    return lax.dot_general(a, b, (((1,), (1,)), ((), ())), preferred_element_type=F32)


def _dot_tn(a, b):
    return lax.dot_general(a, b, (((0,), (0,)), ((), ())), preferred_element_type=F32)


def _sigmoid(x):
    return 1.0 / (1.0 + jnp.exp(-x))


def _silu(x):
    return x * _sigmoid(x)


def _log_sigmoid(x):
    return jnp.minimum(x, 0.0) - jnp.log(1.0 + jnp.exp(-jnp.abs(x)))


def _rms(x):
    return x * lax.rsqrt(jnp.mean(x * x, axis=-1, keepdims=True) + EPS)


def _rmsnorm_kernel(x_ref, g_ref, o_ref):
    o_ref[...] = (_rms(x_ref[...]) * g_ref[...]).astype(o_ref.dtype)


def rmsnorm_bf16(x, g):
    m, d = x.shape
    tm = min(ROW_TILE, m)
    return pl.pallas_call(
        _rmsnorm_kernel,
        grid=(m // tm,),
        in_specs=[pl.BlockSpec((tm, d), lambda i: (i, 0)), pl.BlockSpec((1, d), lambda i: (0, 0))],
        out_specs=pl.BlockSpec((tm, d), lambda i: (i, 0)),
        out_shape=jax.ShapeDtypeStruct((m, d), BF16),
        compiler_params=_cparams(("parallel",)),
        name="rmsnorm_bf16",
    )(x, g.reshape(1, d))


def _cast_kernel(x_ref, o_ref):
    o_ref[...] = x_ref[...].astype(o_ref.dtype)


def _largest_divisor(n, cap, multiple):
    best = multiple
    for c in range(multiple, min(n, cap) + 1, multiple):
        if n % c == 0:
            best = c
    return best


def cast_tiled_bf16(w3, layer, tn):
    _, k, n = w3.shape
    tn = min(tn, n)
    assert n % tn == 0
    tk = _largest_divisor(k, max(SUBLANES, CAST_BLOCK_ELEMS // tn), 2 * SUBLANES)
    return pl.pallas_call(
        _cast_kernel,
        grid=(n // tn, k // tk),
        in_specs=[pl.BlockSpec((None, tk, tn), lambda j, i: (layer, i, j))],
        out_specs=pl.BlockSpec((None, tk, tn), lambda j, i: (j, i, 0)),
        out_shape=jax.ShapeDtypeStruct((n // tn, k, tn), BF16),
        compiler_params=_cparams(("parallel", "parallel")),
        name="cast_tiled_bf16",
    )(w3)


def _mm_kernel(*refs, n_w, epilogue, transposed, row_chunk, side_cast):
    a_ref = refs[0]
    w_refs = refs[1:1 + n_w]
    if side_cast:
        side_in, o_ref, side_out = refs[-3:]
        extra_refs = refs[1 + n_w:-3]
        tn_c = side_out.shape[2]
        for t in range(side_out.shape[0]):
            side_out[t] = side_in[:, t * tn_c:(t + 1) * tn_c].astype(side_out.dtype)
    else:
        o_ref = refs[-1]
        extra_refs = refs[1 + n_w:-1]
    ws = [(w[0] if transposed else w[...]).astype(BF16) for w in w_refs]
    mm = _dot_nt if transposed else _dot
    tm = a_ref.shape[0]
    rc = min(tm, row_chunk)
    for r0 in range(0, tm, rc):
        r = slice(r0, r0 + rc)
        accs = [mm(a_ref[r, :], w) for w in ws]
        ex = [e[...] if e.shape[0] == 1 else e[r, :] for e in extra_refs]
        o_ref[r, :] = epilogue(accs, ex).astype(o_ref.dtype)


class RawWeight(NamedTuple):
    w3: jax.Array
    layer: int
    tn: int


class RawWeightT(NamedTuple):
    w3t: jax.Array
    layer: int
    tn: int
    row0: int
    skip_at: int = 0
    skip_rows: int = 0


def _pick_row_panel(m, k, n_col, fixed_bytes, per_row_bytes):
    budget = VMEM_LIMIT - VMEM_ESTIMATE_SLACK
    for tm in ROW_PANELS:
        if m % tm:
            continue
        for buffers in (2, 1):
            if buffers == 1 and n_col < MIN_COLS_FOR_SINGLE_BUFFERED_PANEL:
                continue
            if fixed_bytes + tm * (per_row_bytes + buffers * k * BF16_BYTES) <= budget:
                return tm, buffers
    raise ValueError("no row panel fits VMEM")


def matmul(a, w, n_out, *, w_col_offsets=(0,), epilogue=None, extras=(), out_dtype=F32, side_cast=None,
           name="matmul"):
    m, k = a.shape
    transposed = isinstance(w, RawWeightT)
    tn = w.shape[2] if isinstance(w, jax.Array) else w.tn
    assert a.dtype == BF16 and n_out % tn == 0
    n_col = n_out // tn
    w_tile = len(w_col_offsets) * k * tn
    fixed_bytes = w_tile * (2 * BF16_BYTES if isinstance(w, jax.Array) else 2 * F32_BYTES + BF16_BYTES)
    per_row_bytes = 2 * tn * (jnp.dtype(out_dtype).itemsize
                              + sum(arr.dtype.itemsize for arr, _ in extras if arr.shape[0] > 1))
    if side_cast is not None:
        side_elems = side_cast.w3.shape[1] * side_cast.w3.shape[2]
        per_row_bytes += 2 * (F32_BYTES + BF16_BYTES) * side_elems // (m * n_col)
    acc_row_bytes = len(w_col_offsets) * tn * F32_BYTES
    if epilogue is None:
        per_row_bytes += acc_row_bytes
    else:
        fixed_bytes += MM_ROW_CHUNK * acc_row_bytes
    tm, a_buffers = _pick_row_panel(m, k, n_col, fixed_bytes, per_row_bytes)
    row_chunk = MM_ROW_CHUNK if epilogue is not None else tm
    if epilogue is None:
        epilogue = lambda accs, ex: accs[0]
    in_specs = [pl.BlockSpec((tm, k), lambda i, j: (i, 0), pipeline_mode=pl.Buffered(a_buffers))]
    args = [a]
    for off in w_col_offsets:
        assert off % tn == 0
        if transposed:
            assert w.w3t.shape[2] == k and w.row0 % SUBLANES == 0
            assert w.skip_at % tn == 0 and w.skip_rows % SUBLANES == 0

            def row_block(i, j, o):
                c = o + j * tn
                r = w.row0 + c + jnp.where(c >= w.skip_at, w.skip_rows, 0)
                return (w.layer, pl.multiple_of(r, SUBLANES), 0)

            spec = pl.BlockSpec((pl.Element(1), pl.Element(tn), pl.Element(k)), functools.partial(row_block, o=off))
        elif isinstance(w, RawWeight):
            assert w.w3.shape[1] == k
            spec = pl.BlockSpec((None, k, tn), functools.partial(lambda i, j, o: (w.layer, 0, j + o), o=off // tn))
        else:
            assert w.shape[1] == k
            spec = pl.BlockSpec((None, k, tn), functools.partial(lambda i, j, o: (j + o, 0, 0), o=off // tn))
        in_specs.append(spec)
        args.append(w if isinstance(w, jax.Array) else w[0])
    for arr, off in extras:
        assert off % tn == 0
        rows = arr.shape[0]
        if rows == 1:
            in_specs.append(pl.BlockSpec((1, tn), functools.partial(lambda i, j, o: (0, j + o), o=off // tn)))
        else:
            in_specs.append(pl.BlockSpec((tm, tn), functools.partial(lambda i, j, o: (i, j + o), o=off // tn)))
        args.append(arr)
    out_specs = pl.BlockSpec((tm, tn), lambda i, j: (i, j))
    out_shape = jax.ShapeDtypeStruct((m, n_out), out_dtype)
    if side_cast is not None:
        _, kc, nc = side_cast.w3.shape
        slab = kc // ((m // tm) * n_col)
        assert slab * (m // tm) * n_col == kc and slab % (2 * SUBLANES) == 0 and nc % side_cast.tn == 0
        in_specs.append(pl.BlockSpec((None, slab, nc), lambda i, j: (side_cast.layer, i * n_col + j, 0)))
        args.append(side_cast.w3)
        out_specs = [out_specs, pl.BlockSpec((nc // side_cast.tn, slab, side_cast.tn),
                                             lambda i, j: (0, i * n_col + j, 0))]
        out_shape = [out_shape, jax.ShapeDtypeStruct((nc // side_cast.tn, kc, side_cast.tn), BF16)]
    return pl.pallas_call(
        functools.partial(_mm_kernel, n_w=len(w_col_offsets), epilogue=epilogue, transposed=transposed,
                          row_chunk=row_chunk, side_cast=side_cast is not None),
        grid=(m // tm, n_col),
        in_specs=in_specs,
        out_specs=out_specs,
        out_shape=out_shape,
        compiler_params=_cparams(("parallel", "arbitrary")),
        name=name,
    )(*args)


def _epi_swiglu(accs, ex):
    return _silu(accs[0]) * accs[1]


def _epi_sigmoid_bias(accs, ex):
    return _sigmoid(accs[0] + ex[0])


def _epi_merge(accs, ex):
    gm, gh, ym = ex
    return gm.astype(F32) * ym + gh.astype(F32) * accs[0]


def _combine_kernel(y_ref, h_ref, post_ref, pre_ref, hn_ref, u_ref, *, scale):
    hn = h_ref[...] + scale * (_rms(y_ref[...].astype(F32)) * post_ref[...])
    hn_ref[...] = hn
    u_ref[...] = (_rms(hn) * pre_ref[...]).astype(u_ref.dtype)


def _combine_last_kernel(y_ref, h_ref, post_ref, hn_ref, *, scale):
    hn_ref[...] = h_ref[...] + scale * (_rms(y_ref[...].astype(F32)) * post_ref[...])


def combine(y, h, post, scale, next_pre=None):
    m, d = y.shape
    tm = min(ROW_TILE, m)
    row = pl.BlockSpec((tm, d), lambda i: (i, 0))
    vec = pl.BlockSpec((1, d), lambda i: (0, 0))
    if next_pre is None:
        return pl.pallas_call(
            functools.partial(_combine_last_kernel, scale=scale),
            grid=(m // tm,),
            in_specs=[row, row, vec],
            out_specs=row,
            out_shape=jax.ShapeDtypeStruct((m, d), F32),
            compiler_params=_cparams(("parallel",)),
            name="combine_last",
        )(y, h, post.reshape(1, d)), None
    return pl.pallas_call(
        functools.partial(_combine_kernel, scale=scale),
        grid=(m // tm,),
        in_specs=[row, row, vec, vec],
        out_specs=[row, row],
        out_shape=[jax.ShapeDtypeStruct((m, d), F32), jax.ShapeDtypeStruct((m, d), BF16)],
        compiler_params=_cparams(("parallel",)),
        name="combine",
    )(y, h, post.reshape(1, d), next_pre.reshape(1, d))


def _xattn_kernel(q_ref, k_ref, v_ref, o_ref, *, heads, dh):
    scale = dh ** -0.5
    for hd in range(heads):
        sl = slice(hd * dh, (hd + 1) * dh)
        s = _dot_nt(q_ref[:, sl], k_ref[:, sl]) * scale
        s = s - jnp.max(s, axis=-1, keepdims=True)
        p = jnp.exp(s)
        p = p / jnp.sum(p, axis=-1, keepdims=True)
        o_ref[:, sl] = _dot(p.astype(BF16), v_ref[:, sl]).astype(o_ref.dtype)


def _xattn_out_kernel(q_ref, k_ref, v_ref, wo_ref, h_ref, post_ref, pre_ref, hn_ref, u_ref, o_scr, *, heads, dh):
    _xattn_kernel(q_ref, k_ref, v_ref, o_scr, heads=heads, dh=dh)
    tm = o_scr.shape[0]
    for r0 in range(0, tm, tm // 2):
        r = slice(r0, r0 + tm // 2)
        y = _dot(o_scr[r, :], wo_ref[...])
        hn = h_ref[r, :] + _rms(y) * post_ref[...]
        hn_ref[r, :] = hn
        u_ref[r, :] = (_rms(hn) * pre_ref[...]).astype(u_ref.dtype)


def xattn_out(q, kv, w_o, h, post, next_pre, batch, heads, dh):
    rows, width = q.shape
    d = h.shape[1]
    s_len = rows // batch
    n_mem = kv.shape[0] // batch
    tm = min(ROW_TILE, s_len)
    nt = s_len // tm
    row = pl.BlockSpec((tm, d), lambda b, i: (b * nt + i, 0))
    vec = pl.BlockSpec((1, d), lambda b, i: (0, 0))
    return pl.pallas_call(
        functools.partial(_xattn_out_kernel, heads=heads, dh=dh),
        grid=(batch, nt),
        in_specs=[
            pl.BlockSpec((tm, width), lambda b, i: (b * nt + i, 0)),
            pl.BlockSpec((n_mem, width), lambda b, i: (b, 0)),
            pl.BlockSpec((n_mem, width), lambda b, i: (b, 1)),
            pl.BlockSpec((None, width, d), lambda b, i: (0, 0, 0), pipeline_mode=pl.Buffered(1)),
            row, vec, vec,
        ],
        out_specs=[row, row],
        out_shape=[jax.ShapeDtypeStruct((rows, d), F32), jax.ShapeDtypeStruct((rows, d), BF16)],
        scratch_shapes=[pltpu.VMEM((tm, width), BF16)],
        compiler_params=_cparams(("parallel", "arbitrary")),
        name="xattn_out",
    )(q, kv, kv, w_o, h, post.reshape(1, d), next_pre.reshape(1, d))


def _causal_conv_silu(x, tail, w):
    rows8 = lax.broadcasted_iota(jnp.int32, (SUBLANES, 1), 0)
    y = x * w[CONV_K - 1:CONV_K]
    for d in range(1, CONV_K):
        xs = pltpu.roll(x, d, 0)
        top = jnp.where(rows8 < d, pltpu.roll(tail, d, 0), xs[0:SUBLANES])
        xs = jnp.concatenate([top, xs[SUBLANES:]], axis=0)
        y = y + xs * w[CONV_K - 1 - d:CONV_K - d]
    return _silu(y)


def _mlstm_kernel(q_ref, k_ref, v_ref, og_ref, gc_ref, gr_ref, bc_ref, br_ref, conv_ref, norm_ref, out_ref,
                  c_scr, n_scr, m_scr, tq_scr, tk_scr, *, heads, dqk, dv):
    L = q_ref.shape[0]
    qk_w = heads * dqk

    @pl.when(pl.program_id(1) == 0)
    def _():
        c_scr[...] = jnp.zeros_like(c_scr)
        n_scr[...] = jnp.zeros_like(n_scr)
        m_scr[...] = jnp.zeros_like(m_scr)
        tq_scr[...] = jnp.zeros_like(tq_scr)
        tk_scr[...] = jnp.zeros_like(tk_scr)

    q_raw = q_ref[...]
    k_raw = k_ref[...]
    q_all = _causal_conv_silu(q_raw, tq_scr[...], conv_ref[:, 0:qk_w]) * (dqk ** -0.5)
    k_all = _causal_conv_silu(k_raw, tk_scr[...], conv_ref[:, qk_w:2 * qk_w])
    tq_scr[...] = q_raw[L - SUBLANES:L]
    tk_scr[...] = k_raw[L - SUBLANES:L]

    gc = gc_ref[...] + bc_ref[...]
    gr = gr_ref[0] + br_ref[...]
    t_idx = lax.broadcasted_iota(jnp.int32, (L, L), 0)
    s_idx = lax.broadcasted_iota(jnp.int32, (L, L), 1)
    causal = s_idx <= t_idx
    tri = causal.astype(F32)
    bc_all = jnp.dot(tri, _log_sigmoid(gc), precision=lax.Precision.HIGHEST, preferred_element_type=F32)
    br_all = lax.dot_general(_log_sigmoid(gr), tri, (((1,), (1,)), ((), ())),
                             precision=lax.Precision.HIGHEST, preferred_element_type=F32)

    for hd in range(heads):
        qs = slice(hd * dqk, (hd + 1) * dqk)
        vs = slice(hd * dv, (hd + 1) * dv)
        q = q_all[:, qs]
        k = k_all[:, qs]
        qb = q.astype(BF16)
        kb = k.astype(BF16)
        vb = v_ref[:, vs].astype(BF16)
        b_c = bc_all[:, heads + hd:heads + hd + 1]
        li_c = gc[:, hd:hd + 1]
        b_r = br_all[heads + hd:heads + hd + 1, :]
        li_r = gr[hd:hd + 1, :]
        m_prev = m_scr[hd:hd + 1, 0:1]
        n_prev = n_scr[hd:hd + 1, :]

        d = jnp.where(causal, b_c - b_r + li_r, -jnp.inf)
        m_inter = b_c + m_prev
        m_t = jnp.maximum(m_inter, jnp.max(d, axis=1, keepdims=True))
        w_intra = jnp.exp(d - m_t)
        w_inter = jnp.exp(m_inter - m_t)
        s = _dot_nt(qb, kb) * w_intra
        num = _dot(s.astype(BF16), vb) + w_inter * _dot(qb, c_scr[hd].astype(BF16))
        den = jnp.sum(s, axis=1, keepdims=True) + w_inter * jnp.sum(q * n_prev, axis=1, keepdims=True)
        hh = num * (1.0 / jnp.maximum(jnp.abs(den), jnp.exp(-m_t)))
        out = _rms(hh) * norm_ref[:, vs] * _sigmoid(og_ref[:, vs])
        out_ref[:, vs] = out.astype(out_ref.dtype)

        g = b_c[L - 1:L, :]
        a_c = g - b_c + li_c
        m_new = jnp.maximum(g + m_prev, jnp.max(a_c, axis=0, keepdims=True))
        decay = jnp.exp(g + m_prev - m_new)
        kw = k * jnp.exp(a_c - m_new)
        c_scr[hd] = decay * c_scr[hd] + _dot_tn(kw.astype(BF16), vb)
        n_scr[hd:hd + 1, :] = decay * n_prev + jnp.sum(kw, axis=0, keepdims=True)
        m_scr[hd:hd + 1, :] = jnp.broadcast_to(m_new, (1, LANES))


def mlstm_scan(proj, gates_col, gates_row, gate_bias, conv_w, head_norm, batch, *, chunk, heads=M_HEADS,
               dqk=M_DQK, dv=M_DV):
    rows = proj.shape[0]
    s_len = rows // batch
    L = min(chunk, s_len)
    nc = s_len // L
    qk_w = heads * dqk
    v_w = heads * dv
    assert qk_w % LANES == 0 and v_w % qk_w == 0
    bias_col = gate_bias.reshape(1, 2 * heads)
    bias_row = gate_bias.reshape(2 * heads, 1)
    kern = functools.partial(_mlstm_kernel, heads=heads, dqk=dqk, dv=dv)
    return pl.pallas_call(
        kern,
        grid=(batch, nc),
        in_specs=[
            pl.BlockSpec((L, qk_w), lambda b, c: (b * nc + c, 0)),
            pl.BlockSpec((L, qk_w), lambda b, c: (b * nc + c, 1)),
            pl.BlockSpec((L, v_w), lambda b, c: (b * nc + c, 2 * qk_w // v_w)),
            pl.BlockSpec((L, v_w), lambda b, c: (b * nc + c, 2 * qk_w // v_w + 1)),
            pl.BlockSpec((L, 2 * heads), lambda b, c: (b * nc + c, 0)),
            pl.BlockSpec((1, 2 * heads, L), lambda b, c: (b, 0, c)),
            pl.BlockSpec((1, 2 * heads), lambda b, c: (0, 0)),
            pl.BlockSpec((2 * heads, 1), lambda b, c: (0, 0)),
            pl.BlockSpec((CONV_K, 2 * qk_w), lambda b, c: (0, 0)),
            pl.BlockSpec((1, v_w), lambda b, c: (0, 0)),
        ],
        out_specs=pl.BlockSpec((L, v_w), lambda b, c: (b * nc + c, 0)),
        out_shape=jax.ShapeDtypeStruct((rows, v_w), BF16),
        scratch_shapes=[
            pltpu.VMEM((heads, dqk, dv), F32),
            pltpu.VMEM((heads, dqk), F32),
            pltpu.VMEM((heads, LANES), F32),
            pltpu.VMEM((SUBLANES, qk_w), F32),
            pltpu.VMEM((SUBLANES, qk_w), F32),
        ],
        compiler_params=_cparams(("parallel", "arbitrary")),
        name="mlstm_scan",
    )(proj, proj, proj, proj, gates_col, gates_row, bias_col, bias_row, conv_w, head_norm.reshape(1, v_w))


def _hgrn_kernel(q_ref, f_ref, i_ref, g_ref, lb_ref, norm_ref, out_ref, st_scr, *, heads, dk, dv, layer):
    L = q_ref.shape[0]

    @pl.when(pl.program_id(1) == 0)
    def _():
        st_scr[...] = jnp.zeros_like(st_scr)

    lbp = lb_ref[...]
    e = jnp.exp(lbp - jnp.max(lbp, axis=0, keepdims=True))
    p = e / jnp.sum(e, axis=0, keepdims=True)
    lb = jnp.zeros((1, lbp.shape[1]), F32)
    for j in range(1, layer + 1):
        lb = lb + p[j:j + 1]

    f = lb + (1.0 - lb) * _sigmoid(f_ref[...])
    lg = jnp.log(f)
    k_all = 1.0 - f
    t_idx = lax.broadcasted_iota(jnp.int32, (L, L), 0)
    s_idx = lax.broadcasted_iota(jnp.int32, (L, L), 1)
    tri = (s_idx <= t_idx).astype(F32)
    b_all = jnp.dot(tri, lg, precision=lax.Precision.HIGHEST, preferred_element_type=F32)
    level = jnp.where(t_idx > s_idx, 31 - lax.clz(t_idx ^ s_idx), jnp.where(t_idx == s_idx, -1, -2))
    row = lax.broadcasted_iota(jnp.int32, (L, 1), 0)
    n_levels = int(math.log2(L))
    k_slices = [slice(hd * dk, (hd + 1) * dk) for hd in range(heads)]
    q_all = q_ref[...]
    qb_all = q_all.astype(BF16)

    def scores(qx, kx, lw, acc):
        keep = level == lw
        return [jnp.where(keep, _dot_nt(qx[:, sl], kx[:, sl]), acc[i]) for i, sl in enumerate(k_slices)]

    a = scores(qb_all, k_all.astype(BF16), -1, [0.0] * heads)
    a = scores(qb_all, (k_all * pltpu.roll(f, L - 1, 0)).astype(BF16), 0, a)
    fw = jnp.where((row & 1) == 1, pltpu.roll(b_all, 1, 0), b_all)
    for lw in range(1, n_levels):
        w = 1 << lw
        odd = ((row >> lw) & 1) == 1
        gw = pltpu.roll(fw, L - w, 0)
        e = jnp.exp(jnp.where(odd, b_all, gw) - jnp.where(odd, fw, b_all))
        a = scores((q_all * e).astype(BF16), (k_all * e).astype(BF16), lw, a)
        fw = jnp.where(odd, pltpu.roll(fw, w, 0), fw)

    g = b_all[L - 1:L, :]
    qe_all = (q_all * jnp.exp(b_all)).astype(BF16)
    ke_all = (k_all * jnp.exp(g - b_all)).astype(BF16)
    eg = jnp.exp(g)
    ib_all = i_ref[...].astype(BF16)
    for hd in range(heads):
        ks = k_slices[hd]
        vs = slice(hd * dv, (hd + 1) * dv)
        st = st_scr[hd]
        o = _dot(a[hd].astype(BF16), ib_all[:, vs]) + _dot_nt(qe_all[:, ks], st.astype(BF16))
        st_scr[hd] = st * eg[:, ks] + _dot_tn(ib_all[:, vs], ke_all[:, ks])
        out = _rms(o) * norm_ref[:, vs] * _silu(g_ref[:, vs])
        out_ref[:, vs] = out.astype(out_ref.dtype)


def hgrn_scan(proj, col_block0, lower_bounds, head_norm, batch, layer, *, chunk, heads=H_HEADS, dk=H_DK, dv=H_DV):
    rows = proj.shape[0]
    s_len = rows // batch
    L = min(chunk, s_len)
    nc = s_len // L
    assert L & (L - 1) == 0 and dk == dv
    w = heads * dk
    depth = lower_bounds.shape[0]
    kern = functools.partial(_hgrn_kernel, heads=heads, dk=dk, dv=dv, layer=layer)
    blk = lambda o: pl.BlockSpec((L, w), functools.partial(lambda b, c, o: (b * nc + c, o), o=col_block0 + o))
    return pl.pallas_call(
        kern,
        grid=(batch, nc),
        in_specs=[blk(0), blk(1), blk(2), blk(3),
                  pl.BlockSpec((depth, w), lambda b, c: (0, 0)),
                  pl.BlockSpec((1, w), lambda b, c: (0, 0))],
        out_specs=pl.BlockSpec((L, w), lambda b, c: (b * nc + c, 0)),
        out_shape=jax.ShapeDtypeStruct((rows, w), BF16),
        scratch_shapes=[pltpu.VMEM((heads, dv, dk), F32)],
        compiler_params=_cparams(("parallel", "arbitrary")),
        name="hgrn_scan",
    )(proj, proj, proj, proj, lower_bounds, head_norm.reshape(1, w))


def _ffn(u, w_gate_up, w_down, layer):
    d_ff, d = w_down.shape[1:]
    act, w_dn = matmul(u, RawWeight(w_gate_up, layer, FFN_UP_TN), d_ff, w_col_offsets=(0, d_ff),
                       epilogue=_epi_swiglu, out_dtype=BF16, side_cast=RawWeight(w_down, layer, MM_TN),
                       name="ffn_gate_up")
    return matmul(act, w_dn, d, out_dtype=BF16, name="ffn_down")


def _token_mixers(u, layer, batch, w_in, gate_bias, conv_w, m_norm, lower_bounds, h_norm,
                  w_bm, w_bh, w_gate, b_gate, w_out):
    rows, d = u.shape
    s_len = rows // batch
    gate_lo = 2 * M_HEADS * M_DQK + 2 * M_HEADS * M_DV
    n_gates = 2 * M_HEADS
    w_in_t = jnp.swapaxes(w_in, 1, 2)
    proj = matmul(u, RawWeightT(w_in_t, layer, MM_TN, 0, gate_lo, n_gates), w_in.shape[2] - n_gates,
                  name="mix_in_proj")
    gates_col = matmul(u, RawWeightT(w_in_t, layer, n_gates, gate_lo), n_gates, name="mix_gate_proj")
    gates_row = gates_col.reshape(batch, s_len, n_gates).transpose(0, 2, 1)
    ym = mlstm_scan(proj, gates_col, gates_row, gate_bias, conv_w, m_norm, batch, chunk=MLSTM_CHUNK)
    yh = hgrn_scan(proj, gate_lo // (H_HEADS * H_DK), lower_bounds, h_norm, batch, layer, chunk=HGRN_CHUNK)
    gate = matmul(u, RawWeight(w_gate, layer, MM_TN), 2 * d, epilogue=_epi_sigmoid_bias,
                  extras=((b_gate.reshape(1, 2 * d), 0),), out_dtype=BF16, name="merge_gate")
    y_m = matmul(ym, cast_tiled_bf16(w_bm, layer, MM_WIDE_TN), d, name="branch_mlstm")
    z = matmul(yh, cast_tiled_bf16(w_bh, layer, MM_WIDE_TN), d, epilogue=_epi_merge,
               extras=((gate, 0), (gate, d), (y_m, 0)), out_dtype=BF16, name="branch_hgrn_merge")
    return matmul(z, RawWeight(w_out, layer, MM_TN), d, out_dtype=BF16, name="mix_out")


def _xattn(u, h, mem2, mem_norm, batch, layer, w_q, w_kv, w_o, post, next_pre):
    memn = rmsnorm_bf16(mem2, mem_norm)
    kv = matmul(memn, RawWeight(w_kv, layer, MM_TN), w_kv.shape[2], out_dtype=BF16, name="xattn_kv")
    q = matmul(u, cast_tiled_bf16(w_q, layer, MM_WIDE_TN), w_q.shape[2], out_dtype=BF16, name="xattn_q")
    return xattn_out(q, kv, cast_tiled_bf16(w_o, layer, w_o.shape[2]), h, post, next_pre, batch, X_HEADS, X_DH)


def kernel(x, mem, ffn1_pre_norm, ffn1_w_gate_up, ffn1_w_down, ffn1_post_norm, mix_pre_norm, mix_w_in, mlstm_gate_bias, mlstm_conv, mlstm_head_norm, hgrn_lower_bounds, hgrn_head_norm, branch_w_mlstm, branch_w_hgrn, merge_w_gate, merge_b_gate, mix_w_out, mix_post_norm, xattn_pre_norm, xattn_mem_norm, xattn_w_q, xattn_w_kv, xattn_w_o, xattn_post_norm, ffn2_pre_norm, ffn2_w_gate_up, ffn2_w_down, ffn2_post_norm):
    batch, s_len, d = x.shape
    n_mem = mem.shape[1]
    depth = ffn1_pre_norm.shape[0]

    h = x.reshape(batch * s_len, d)
    mem2 = mem.reshape(batch * n_mem, d)
    u = rmsnorm_bf16(h, ffn1_pre_norm[0])
    for l in range(depth):
        y = _ffn(u, ffn1_w_gate_up, ffn1_w_down, l)
        h, u = combine(y, h, ffn1_post_norm[l], 0.5, mix_pre_norm[l])

        y = _token_mixers(u, l, batch, mix_w_in, mlstm_gate_bias[l], mlstm_conv[l], mlstm_head_norm[l],
                          hgrn_lower_bounds, hgrn_head_norm[l], branch_w_mlstm, branch_w_hgrn, merge_w_gate,
                          merge_b_gate[l], mix_w_out)
        h, u = combine(y, h, mix_post_norm[l], 1.0, xattn_pre_norm[l])

        h, u = _xattn(u, h, mem2, xattn_mem_norm[l], batch, l, xattn_w_q, xattn_w_kv, xattn_w_o,
                      xattn_post_norm[l], ffn2_pre_norm[l])

        y = _ffn(u, ffn2_w_gate_up, ffn2_w_down, l)
        nxt = ffn1_pre_norm[l + 1] if l + 1 < depth else None
        h, u = combine(y, h, ffn2_post_norm[l], 0.5, nxt)
    return h.reshape(batch, s_len, d)
```

```python
import functools
import math
from typing import NamedTuple

import jax
import jax.numpy as jnp
from jax import lax
from jax.experimental import pallas as pl
from jax.experimental.pallas import tpu as pltpu

EPS = 1e-6
F32 = jnp.float32
BF16 = jnp.bfloat16

V7X_VMEM_BYTES = 64 * 1024 * 1024
VMEM_LIMIT = V7X_VMEM_BYTES - 4 * 1024 * 1024
F32_BYTES = 4
BF16_BYTES = 2
LANES = 128
SUBLANES = 8

CONV_K = 4
M_HEADS = 8
M_DQK = 128
M_DV = 256
H_HEADS = 16
H_DK = 128
H_DV = 128
X_HEADS = 4
X_DH = 256
MLSTM_CHUNK = 256
HGRN_CHUNK = 64
CAST_BLOCK_ELEMS = 2 * 1024 * 1024
ROW_TILE = 256
ROW_PANELS = (2048, 1024, 512)
MIN_COLS_FOR_SINGLE_BUFFERED_PANEL = 24
VMEM_ESTIMATE_SLACK = 2 * 1024 * 1024
MM_ROW_CHUNK = 256
MM_TN = 512
MM_WIDE_TN = 1024
FFN_UP_TN = 256
FFN_DOWN_TN = 256


def _cparams(sem):
    return pltpu.CompilerParams(dimension_semantics=sem, vmem_limit_bytes=VMEM_LIMIT)


def _dot(a, b):
    return jnp.dot(a, b, preferred_element_type=F32)


def _dot_nt(a, b):
    return lax.dot_general(a, b, (((1,), (1,)), ((), ())), preferred_element_type=F32)


def _dot_tn(a, b):
    return lax.dot_general(a, b, (((0,), (0,)), ((), ())), preferred_element_type=F32)


def _sigmoid(x):
    return 1.0 / (1.0 + jnp.exp(-x))


def _silu(x):
    return x * _sigmoid(x)


def _log_sigmoid(x):
    return jnp.minimum(x, 0.0) - jnp.log(1.0 + jnp.exp(-jnp.abs(x)))


def _rms(x):
    return x * lax.rsqrt(jnp.mean(x * x, axis=-1, keepdims=True) + EPS)


def _rmsnorm_kernel(x_ref, g_ref, o_ref):
    o_ref[...] = (_rms(x_ref[...]) * g_ref[...]).astype(o_ref.dtype)


def rmsnorm_bf16(x, g):
    m, d = x.shape
    tm = min(ROW_TILE, m)
    return pl.pallas_call(
        _rmsnorm_kernel,
        grid=(m // tm,),
        in_specs=[pl.BlockSpec((tm, d), lambda i: (i, 0)), pl.BlockSpec((1, d), lambda i: (0, 0))],
        out_specs=pl.BlockSpec((tm, d), lambda i: (i, 0)),
        out_shape=jax.ShapeDtypeStruct((m, d), BF16),
        compiler_params=_cparams(("parallel",)),
        name="rmsnorm_bf16",
    )(x, g.reshape(1, d))


def _cast_kernel(x_ref, o_ref):
    o_ref[...] = x_ref[...].astype(o_ref.dtype)


def _largest_divisor(n, cap, multiple):
    best = multiple
    for c in range(multiple, min(n, cap) + 1, multiple):
        if n % c == 0:
            best = c
    return best


def cast_tiled_bf16(w3, layer, tn):
    _, k, n = w3.shape
    tn = min(tn, n)
    assert n % tn == 0
    tk = _largest_divisor(k, max(SUBLANES, CAST_BLOCK_ELEMS // tn), 2 * SUBLANES)
    return pl.pallas_call(
        _cast_kernel,
        grid=(n // tn, k // tk),
        in_specs=[pl.BlockSpec((None, tk, tn), lambda j, i: (layer, i, j))],
        out_specs=pl.BlockSpec((None, tk, tn), lambda j, i: (j, i, 0)),
        out_shape=jax.ShapeDtypeStruct((n // tn, k, tn), BF16),
        compiler_params=_cparams(("parallel", "parallel")),
        name="cast_tiled_bf16",
    )(w3)


def _mm_kernel(*refs, n_w, epilogue, transposed, row_chunk, side_cast):
    a_ref = refs[0]
    w_refs = refs[1:1 + n_w]
    if side_cast:
        side_in, o_ref, side_out = refs[-3:]
        extra_refs = refs[1 + n_w:-3]
        tn_c = side_out.shape[2]
        for t in range(side_out.shape[0]):
            side_out[t] = side_in[:, t * tn_c:(t + 1) * tn_c].astype(side_out.dtype)
    else:
        o_ref = refs[-1]
        extra_refs = refs[1 + n_w:-1]
    ws = [(w[0] if transposed else w[...]).astype(BF16) for w in w_refs]
    mm = _dot_nt if transposed else _dot
    tm = a_ref.shape[0]
    rc = min(tm, row_chunk)
    for r0 in range(0, tm, rc):
        r = slice(r0, r0 + rc)
        accs = [mm(a_ref[r, :], w) for w in ws]
        ex = [e[...] if e.shape[0] == 1 else e[r, :] for e in extra_refs]
        o_ref[r, :] = epilogue(accs, ex).astype(o_ref.dtype)


class RawWeight(NamedTuple):
    w3: jax.Array
    layer: int
    tn: int


class RawWeightT(NamedTuple):
    w3t: jax.Array
    layer: int
    tn: int
    row0: int
    skip_at: int = 0
    skip_rows: int = 0


def _pick_row_panel(m, k, n_col, fixed_bytes, per_row_bytes):
    budget = VMEM_LIMIT - VMEM_ESTIMATE_SLACK
    for tm in ROW_PANELS:
        if m % tm:
            continue
        for buffers in (2, 1):
            if buffers == 1 and n_col < MIN_COLS_FOR_SINGLE_BUFFERED_PANEL:
                continue
            if fixed_bytes + tm * (per_row_bytes + buffers * k * BF16_BYTES) <= budget:
                return tm, buffers
    raise ValueError("no row panel fits VMEM")


def matmul(a, w, n_out, *, w_col_offsets=(0,), epilogue=None, extras=(), out_dtype=F32, side_cast=None,
           name="matmul"):
    m, k = a.shape
    transposed = isinstance(w, RawWeightT)
    tn = w.shape[2] if isinstance(w, jax.Array) else w.tn
    assert a.dtype == BF16 and n_out % tn == 0
    n_col = n_out // tn
    w_tile = len(w_col_offsets) * k * tn
    fixed_bytes = w_tile * (2 * BF16_BYTES if isinstance(w, jax.Array) else 2 * F32_BYTES + BF16_BYTES)
    per_row_bytes = 2 * tn * (jnp.dtype(out_dtype).itemsize
                              + sum(arr.dtype.itemsize for arr, _ in extras if arr.shape[0] > 1))
    if side_cast is not None:
        side_elems = side_cast.w3.shape[1] * side_cast.w3.shape[2]
        per_row_bytes += 2 * (F32_BYTES + BF16_BYTES) * side_elems // (m * n_col)
    acc_row_bytes = len(w_col_offsets) * tn * F32_BYTES
    if epilogue is None:
        per_row_bytes += acc_row_bytes
    else:
        fixed_bytes += MM_ROW_CHUNK * acc_row_bytes
    tm, a_buffers = _pick_row_panel(m, k, n_col, fixed_bytes, per_row_bytes)
    row_chunk = MM_ROW_CHUNK if epilogue is not None else tm
    if epilogue is None:
        epilogue = lambda accs, ex: accs[0]
    in_specs = [pl.BlockSpec((tm, k), lambda i, j: (i, 0), pipeline_mode=pl.Buffered(a_buffers))]
    args = [a]
    for off in w_col_offsets:
        assert off % tn == 0
        if transposed:
            assert w.w3t.shape[2] == k and w.row0 % SUBLANES == 0
            assert w.skip_at % tn == 0 and w.skip_rows % SUBLANES == 0

            def row_block(i, j, o):
                c = o + j * tn
                r = w.row0 + c + jnp.where(c >= w.skip_at, w.skip_rows, 0)
                return (w.layer, pl.multiple_of(r, SUBLANES), 0)

            spec = pl.BlockSpec((pl.Element(1), pl.Element(tn), pl.Element(k)), functools.partial(row_block, o=off))
        elif isinstance(w, RawWeight):
            assert w.w3.shape[1] == k
            spec = pl.BlockSpec((None, k, tn), functools.partial(lambda i, j, o: (w.layer, 0, j + o), o=off // tn))
        else:
            assert w.shape[1] == k
            spec = pl.BlockSpec((None, k, tn), functools.partial(lambda i, j, o: (j + o, 0, 0), o=off // tn))
        in_specs.append(spec)
        args.append(w if isinstance(w, jax.Array) else w[0])
    for arr, off in extras:
        assert off % tn == 0
        rows = arr.shape[0]
        if rows == 1:
            in_specs.append(pl.BlockSpec((1, tn), functools.partial(lambda i, j, o: (0, j + o), o=off // tn)))
        else:
            in_specs.append(pl.BlockSpec((tm, tn), functools.partial(lambda i, j, o: (i, j + o), o=off // tn)))
        args.append(arr)
    out_specs = pl.BlockSpec((tm, tn), lambda i, j: (i, j))
    out_shape = jax.ShapeDtypeStruct((m, n_out), out_dtype)
    if side_cast is not None:
        _, kc, nc = side_cast.w3.shape
        slab = kc // ((m // tm) * n_col)
        assert slab * (m // tm) * n_col == kc and slab % (2 * SUBLANES) == 0 and nc % side_cast.tn == 0
        in_specs.append(pl.BlockSpec((None, slab, nc), lambda i, j: (side_cast.layer, i * n_col + j, 0)))
        args.append(side_cast.w3)
        out_specs = [out_specs, pl.BlockSpec((nc // side_cast.tn, slab, side_cast.tn),
                                             lambda i, j: (0, i * n_col + j, 0))]
        out_shape = [out_shape, jax.ShapeDtypeStruct((nc // side_cast.tn, kc, side_cast.tn), BF16)]
    return pl.pallas_call(
        functools.partial(_mm_kernel, n_w=len(w_col_offsets), epilogue=epilogue, transposed=transposed,
                          row_chunk=row_chunk, side_cast=side_cast is not None),
        grid=(m // tm, n_col),
        in_specs=in_specs,
        out_specs=out_specs,
        out_shape=out_shape,
        compiler_params=_cparams(("parallel", "arbitrary")),
        name=name,
    )(*args)


def _epi_swiglu(accs, ex):
    return _silu(accs[0]) * accs[1]


def _epi_sigmoid_bias(accs, ex):
    return _sigmoid(accs[0] + ex[0])


def _epi_merge(accs, ex):
    gm, gh, ym = ex
    return gm.astype(F32) * ym + gh.astype(F32) * accs[0]


def _combine_kernel(y_ref, h_ref, post_ref, pre_ref, hn_ref, u_ref, *, scale):
    hn = h_ref[...] + scale * (_rms(y_ref[...].astype(F32)) * post_ref[...])
    hn_ref[...] = hn
    u_ref[...] = (_rms(hn) * pre_ref[...]).astype(u_ref.dtype)


def _combine_last_kernel(y_ref, h_ref, post_ref, hn_ref, *, scale):
    hn_ref[...] = h_ref[...] + scale * (_rms(y_ref[...].astype(F32)) * post_ref[...])


def combine(y, h, post, scale, next_pre=None):
    m, d = y.shape
    tm = min(ROW_TILE, m)
    row = pl.BlockSpec((tm, d), lambda i: (i, 0))
    vec = pl.BlockSpec((1, d), lambda i: (0, 0))
    if next_pre is None:
        return pl.pallas_call(
            functools.partial(_combine_last_kernel, scale=scale),
            grid=(m // tm,),
            in_specs=[row, row, vec],
            out_specs=row,
            out_shape=jax.ShapeDtypeStruct((m, d), F32),
            compiler_params=_cparams(("parallel",)),
            name="combine_last",
        )(y, h, post.reshape(1, d)), None
    return pl.pallas_call(
        functools.partial(_combine_kernel, scale=scale),
        grid=(m // tm,),
        in_specs=[row, row, vec, vec],
        out_specs=[row, row],
        out_shape=[jax.ShapeDtypeStruct((m, d), F32), jax.ShapeDtypeStruct((m, d), BF16)],
        compiler_params=_cparams(("parallel",)),
        name="combine",
    )(y, h, post.reshape(1, d), next_pre.reshape(1, d))


def _xattn_kernel(q_ref, k_ref, v_ref, o_ref, *, heads, dh):
    scale = dh ** -0.5
    for hd in range(heads):
        sl = slice(hd * dh, (hd + 1) * dh)
        s = _dot_nt(q_ref[:, sl], k_ref[:, sl]) * scale
        s = s - jnp.max(s, axis=-1, keepdims=True)
        p = jnp.exp(s)
        p = p / jnp.sum(p, axis=-1, keepdims=True)
        o_ref[:, sl] = _dot(p.astype(BF16), v_ref[:, sl]).astype(o_ref.dtype)


def _xattn_out_kernel(q_ref, k_ref, v_ref, wo_ref, h_ref, post_ref, pre_ref, hn_ref, u_ref, o_scr, *, heads, dh):
    _xattn_kernel(q_ref, k_ref, v_ref, o_scr, heads=heads, dh=dh)
    tm = o_scr.shape[0]
    for r0 in range(0, tm, tm // 2):
        r = slice(r0, r0 + tm // 2)
        y = _dot(o_scr[r, :], wo_ref[...])
        hn = h_ref[r, :] + _rms(y) * post_ref[...]
        hn_ref[r, :] = hn
        u_ref[r, :] = (_rms(hn) * pre_ref[...]).astype(u_ref.dtype)


def xattn_out(q, kv, w_o, h, post, next_pre, batch, heads, dh):
    rows, width = q.shape
    d = h.shape[1]
    s_len = rows // batch
    n_mem = kv.shape[0] // batch
    tm = min(ROW_TILE, s_len)
    nt = s_len // tm
    row = pl.BlockSpec((tm, d), lambda b, i: (b * nt + i, 0))
    vec = pl.BlockSpec((1, d), lambda b, i: (0, 0))
    return pl.pallas_call(
        functools.partial(_xattn_out_kernel, heads=heads, dh=dh),
        grid=(batch, nt),
        in_specs=[
            pl.BlockSpec((tm, width), lambda b, i: (b * nt + i, 0)),
            pl.BlockSpec((n_mem, width), lambda b, i: (b, 0)),
            pl.BlockSpec((n_mem, width), lambda b, i: (b, 1)),
            pl.BlockSpec((None, width, d), lambda b, i: (0, 0, 0), pipeline_mode=pl.Buffered(1)),
            row, vec, vec,
        ],
        out_specs=[row, row],
        out_shape=[jax.ShapeDtypeStruct((rows, d), F32), jax.ShapeDtypeStruct((rows, d), BF16)],
        scratch_shapes=[pltpu.VMEM((tm, width), BF16)],
        compiler_params=_cparams(("parallel", "arbitrary")),
        name="xattn_out",
    )(q, kv, kv, w_o, h, post.reshape(1, d), next_pre.reshape(1, d))


def _causal_conv_silu(x, tail, w):
    rows8 = lax.broadcasted_iota(jnp.int32, (SUBLANES, 1), 0)
    y = x * w[CONV_K - 1:CONV_K]
    for d in range(1, CONV_K):
        xs = pltpu.roll(x, d, 0)
        top = jnp.where(rows8 < d, pltpu.roll(tail, d, 0), xs[0:SUBLANES])
        xs = jnp.concatenate([top, xs[SUBLANES:]], axis=0)
        y = y + xs * w[CONV_K - 1 - d:CONV_K - d]
    return _silu(y)


def _mlstm_kernel(q_ref, k_ref, v_ref, og_ref, gc_ref, gr_ref, bc_ref, br_ref, conv_ref, norm_ref, out_ref,
                  c_scr, n_scr, m_scr, tq_scr, tk_scr, *, heads, dqk, dv):
    L = q_ref.shape[0]
    qk_w = heads * dqk

    @pl.when(pl.program_id(1) == 0)
    def _():
        c_scr[...] = jnp.zeros_like(c_scr)
        n_scr[...] = jnp.zeros_like(n_scr)
        m_scr[...] = jnp.zeros_like(m_scr)
        tq_scr[...] = jnp.zeros_like(tq_scr)
        tk_scr[...] = jnp.zeros_like(tk_scr)

    q_raw = q_ref[...]
    k_raw = k_ref[...]
    q_all = _causal_conv_silu(q_raw, tq_scr[...], conv_ref[:, 0:qk_w]) * (dqk ** -0.5)
    k_all = _causal_conv_silu(k_raw, tk_scr[...], conv_ref[:, qk_w:2 * qk_w])
    tq_scr[...] = q_raw[L - SUBLANES:L]
    tk_scr[...] = k_raw[L - SUBLANES:L]

    gc = gc_ref[...] + bc_ref[...]
    gr = gr_ref[0] + br_ref[...]
    t_idx = lax.broadcasted_iota(jnp.int32, (L, L), 0)
    s_idx = lax.broadcasted_iota(jnp.int32, (L, L), 1)
    causal = s_idx <= t_idx
    tri = causal.astype(F32)
    bc_all = jnp.dot(tri, _log_sigmoid(gc), precision=lax.Precision.HIGHEST, preferred_element_type=F32)
    br_all = lax.dot_general(_log_sigmoid(gr), tri, (((1,), (1,)), ((), ())),
                             precision=lax.Precision.HIGHEST, preferred_element_type=F32)

    for hd in range(heads):
        qs = slice(hd * dqk, (hd + 1) * dqk)
        vs = slice(hd * dv, (hd + 1) * dv)
        q = q_all[:, qs]
        k = k_all[:, qs]
        qb = q.astype(BF16)
        kb = k.astype(BF16)
        vb = v_ref[:, vs].astype(BF16)
        b_c = bc_all[:, heads + hd:heads + hd + 1]
        li_c = gc[:, hd:hd + 1]
        b_r = br_all[heads + hd:heads + hd + 1, :]
        li_r = gr[hd:hd + 1, :]
        m_prev = m_scr[hd:hd + 1, 0:1]
        n_prev = n_scr[hd:hd + 1, :]

        d = jnp.where(causal, b_c - b_r + li_r, -jnp.inf)
        m_inter = b_c + m_prev
        m_t = jnp.maximum(m_inter, jnp.max(d, axis=1, keepdims=True))
        w_intra = jnp.exp(d - m_t)
        w_inter = jnp.exp(m_inter - m_t)
        s = _dot_nt(qb, kb) * w_intra
        num = _dot(s.astype(BF16), vb) + w_inter * _dot(qb, c_scr[hd].astype(BF16))
        den = jnp.sum(s, axis=1, keepdims=True) + w_inter * jnp.sum(q * n_prev, axis=1, keepdims=True)
        hh = num * (1.0 / jnp.maximum(jnp.abs(den), jnp.exp(-m_t)))
        out = _rms(hh) * norm_ref[:, vs] * _sigmoid(og_ref[:, vs])
        out_ref[:, vs] = out.astype(out_ref.dtype)

        g = b_c[L - 1:L, :]
        a_c = g - b_c + li_c
        m_new = jnp.maximum(g + m_prev, jnp.max(a_c, axis=0, keepdims=True))
        decay = jnp.exp(g + m_prev - m_new)
        kw = k * jnp.exp(a_c - m_new)
        c_scr[hd] = decay * c_scr[hd] + _dot_tn(kw.astype(BF16), vb)
        n_scr[hd:hd + 1, :] = decay * n_prev + jnp.sum(kw, axis=0, keepdims=True)
        m_scr[hd:hd + 1, :] = jnp.broadcast_to(m_new, (1, LANES))


def mlstm_scan(proj, gates_col, gates_row, gate_bias, conv_w, head_norm, batch, *, chunk, heads=M_HEADS,
               dqk=M_DQK, dv=M_DV):
    rows = proj.shape[0]
    s_len = rows // batch
    L = min(chunk, s_len)
    nc = s_len // L
    qk_w = heads * dqk
    v_w = heads * dv
    assert qk_w % LANES == 0 and v_w % qk_w == 0
    bias_col = gate_bias.reshape(1, 2 * heads)
    bias_row = gate_bias.reshape(2 * heads, 1)
    kern = functools.partial(_mlstm_kernel, heads=heads, dqk=dqk, dv=dv)
    return pl.pallas_call(
        kern,
        grid=(batch, nc),
        in_specs=[
            pl.BlockSpec((L, qk_w), lambda b, c: (b * nc + c, 0)),
            pl.BlockSpec((L, qk_w), lambda b, c: (b * nc + c, 1)),
            pl.BlockSpec((L, v_w), lambda b, c: (b * nc + c, 2 * qk_w // v_w)),
            pl.BlockSpec((L, v_w), lambda b, c: (b * nc + c, 2 * qk_w // v_w + 1)),
            pl.BlockSpec((L, 2 * heads), lambda b, c: (b * nc + c, 0)),
            pl.BlockSpec((1, 2 * heads, L), lambda b, c: (b, 0, c)),
            pl.BlockSpec((1, 2 * heads), lambda b, c: (0, 0)),
            pl.BlockSpec((2 * heads, 1), lambda b, c: (0, 0)),
            pl.BlockSpec((CONV_K, 2 * qk_w), lambda b, c: (0, 0)),
            pl.BlockSpec((1, v_w), lambda b, c: (0, 0)),
        ],
        out_specs=pl.BlockSpec((L, v_w), lambda b, c: (b * nc + c, 0)),
        out_shape=jax.ShapeDtypeStruct((rows, v_w), BF16),
        scratch_shapes=[
            pltpu.VMEM((heads, dqk, dv), F32),
            pltpu.VMEM((heads, dqk), F32),
            pltpu.VMEM((heads, LANES), F32),
            pltpu.VMEM((SUBLANES, qk_w), F32),
            pltpu.VMEM((SUBLANES, qk_w), F32),
        ],
        compiler_params=_cparams(("parallel", "arbitrary")),
        name="mlstm_scan",
    )(proj, proj, proj, proj, gates_col, gates_row, bias_col, bias_row, conv_w, head_norm.reshape(1, v_w))


def _hgrn_kernel(q_ref, f_ref, i_ref, g_ref, lb_ref, norm_ref, out_ref, st_scr, *, heads, dk, dv, layer):
    L = q_ref.shape[0]

    @pl.when(pl.program_id(1) == 0)
    def _():
        st_scr[...] = jnp.zeros_like(st_scr)

    lbp = lb_ref[...]
    e = jnp.exp(lbp - jnp.max(lbp, axis=0, keepdims=True))
    p = e / jnp.sum(e, axis=0, keepdims=True)
    lb = jnp.zeros((1, lbp.shape[1]), F32)
    for j in range(1, layer + 1):
        lb = lb + p[j:j + 1]

    f = lb + (1.0 - lb) * _sigmoid(f_ref[...])
    lg = jnp.log(f)
    k_all = 1.0 - f
    t_idx = lax.broadcasted_iota(jnp.int32, (L, L), 0)
    s_idx = lax.broadcasted_iota(jnp.int32, (L, L), 1)
    tri = (s_idx <= t_idx).astype(F32)
    b_all = jnp.dot(tri, lg, precision=lax.Precision.HIGHEST, preferred_element_type=F32)
    level = jnp.where(t_idx > s_idx, 31 - lax.clz(t_idx ^ s_idx), jnp.where(t_idx == s_idx, -1, -2))
    row = lax.broadcasted_iota(jnp.int32, (L, 1), 0)
    n_levels = int(math.log2(L))
    k_slices = [slice(hd * dk, (hd + 1) * dk) for hd in range(heads)]
    q_all = q_ref[...]
    qb_all = q_all.astype(BF16)

    def scores(qx, kx, lw, acc):
        keep = level == lw
        return [jnp.where(keep, _dot_nt(qx[:, sl], kx[:, sl]), acc[i]) for i, sl in enumerate(k_slices)]

    a = scores(qb_all, k_all.astype(BF16), -1, [0.0] * heads)
    a = scores(qb_all, (k_all * pltpu.roll(f, L - 1, 0)).astype(BF16), 0, a)
    fw = jnp.where((row & 1) == 1, pltpu.roll(b_all, 1, 0), b_all)
    for lw in range(1, n_levels):
        w = 1 << lw
        odd = ((row >> lw) & 1) == 1
        gw = pltpu.roll(fw, L - w, 0)
        e = jnp.exp(jnp.where(odd, b_all, gw) - jnp.where(odd, fw, b_all))
        a = scores((q_all * e).astype(BF16), (k_all * e).astype(BF16), lw, a)
        fw = jnp.where(odd, pltpu.roll(fw, w, 0), fw)

    g = b_all[L - 1:L, :]
    qe_all = (q_all * jnp.exp(b_all)).astype(BF16)
    ke_all = (k_all * jnp.exp(g - b_all)).astype(BF16)
    eg = jnp.exp(g)
    ib_all = i_ref[...].astype(BF16)
    for hd in range(heads):
        ks = k_slices[hd]
        vs = slice(hd * dv, (hd + 1) * dv)
        st = st_scr[hd]
        o = _dot(a[hd].astype(BF16), ib_all[:, vs]) + _dot_nt(qe_all[:, ks], st.astype(BF16))
        st_scr[hd] = st * eg[:, ks] + _dot_tn(ib_all[:, vs], ke_all[:, ks])
        out = _rms(o) * norm_ref[:, vs] * _silu(g_ref[:, vs])
        out_ref[:, vs] = out.astype(out_ref.dtype)


def hgrn_scan(proj, col_block0, lower_bounds, head_norm, batch, layer, *, chunk, heads=H_HEADS, dk=H_DK, dv=H_DV):
    rows = proj.shape[0]
    s_len = rows // batch
    L = min(chunk, s_len)
    nc = s_len // L
    assert L & (L - 1) == 0 and dk == dv
    w = heads * dk
    depth = lower_bounds.shape[0]
    kern = functools.partial(_hgrn_kernel, heads=heads, dk=dk, dv=dv, layer=layer)
    blk = lambda o: pl.BlockSpec((L, w), functools.partial(lambda b, c, o: (b * nc + c, o), o=col_block0 + o))
    return pl.pallas_call(
        kern,
        grid=(batch, nc),
        in_specs=[blk(0), blk(1), blk(2), blk(3),
                  pl.BlockSpec((depth, w), lambda b, c: (0, 0)),
                  pl.BlockSpec((1, w), lambda b, c: (0, 0))],
        out_specs=pl.BlockSpec((L, w), lambda b, c: (b * nc + c, 0)),
        out_shape=jax.ShapeDtypeStruct((rows, w), BF16),
        scratch_shapes=[pltpu.VMEM((heads, dv, dk), F32)],
        compiler_params=_cparams(("parallel", "arbitrary")),
        name="hgrn_scan",
    )(proj, proj, proj, proj, lower_bounds, head_norm.reshape(1, w))


def _ffn(u, w_gate_up, w_down, layer):
    d_ff, d = w_down.shape[1:]
    act, w_dn = matmul(u, RawWeight(w_gate_up, layer, FFN_UP_TN), d_ff, w_col_offsets=(0, d_ff),
                       epilogue=_epi_swiglu, out_dtype=BF16, side_cast=RawWeight(w_down, layer, FFN_DOWN_TN),
                       name="ffn_gate_up")
    return matmul(act, w_dn, d, out_dtype=BF16, name="ffn_down")


def _token_mixers(u, layer, batch, w_in, gate_bias, conv_w, m_norm, lower_bounds, h_norm,
                  w_bm, w_bh, w_gate, b_gate, w_out):
    rows, d = u.shape
    s_len = rows // batch
    gate_lo = 2 * M_HEADS * M_DQK + 2 * M_HEADS * M_DV
    n_gates = 2 * M_HEADS
    w_in_t = jnp.swapaxes(w_in, 1, 2)
    proj = matmul(u, RawWeightT(w_in_t, layer, MM_TN, 0, gate_lo, n_gates), w_in.shape[2] - n_gates,
                  name="mix_in_proj")
    gates_col = matmul(u, RawWeightT(w_in_t, layer, n_gates, gate_lo), n_gates, name="mix_gate_proj")
    gates_row = gates_col.reshape(batch, s_len, n_gates).transpose(0, 2, 1)
    ym = mlstm_scan(proj, gates_col, gates_row, gate_bias, conv_w, m_norm, batch, chunk=MLSTM_CHUNK)
    yh = hgrn_scan(proj, gate_lo // (H_HEADS * H_DK), lower_bounds, h_norm, batch, layer, chunk=HGRN_CHUNK)
    gate = matmul(u, RawWeight(w_gate, layer, MM_TN), 2 * d, epilogue=_epi_sigmoid_bias,
                  extras=((b_gate.reshape(1, 2 * d), 0),), out_dtype=BF16, name="merge_gate")
    y_m = matmul(ym, cast_tiled_bf16(w_bm, layer, MM_WIDE_TN), d, name="branch_mlstm")
    z = matmul(yh, cast_tiled_bf16(w_bh, layer, MM_WIDE_TN), d, epilogue=_epi_merge,
               extras=((gate, 0), (gate, d), (y_m, 0)), out_dtype=BF16, name="branch_hgrn_merge")
    return matmul(z, RawWeight(w_out, layer, MM_TN), d, out_dtype=BF16, name="mix_out")


def _xattn(u, h, mem2, mem_norm, batch, layer, w_q, w_kv, w_o, post, next_pre):
    memn = rmsnorm_bf16(mem2, mem_norm)
    kv = matmul(memn, RawWeight(w_kv, layer, MM_TN), w_kv.shape[2], out_dtype=BF16, name="xattn_kv")
    q = matmul(u, cast_tiled_bf16(w_q, layer, MM_WIDE_TN), w_q.shape[2], out_dtype=BF16, name="xattn_q")
    return xattn_out(q, kv, cast_tiled_bf16(w_o, layer, w_o.shape[2]), h, post, next_pre, batch, X_HEADS, X_DH)


def kernel(x, mem, ffn1_pre_norm, ffn1_w_gate_up, ffn1_w_down, ffn1_post_norm, mix_pre_norm, mix_w_in, mlstm_gate_bias, mlstm_conv, mlstm_head_norm, hgrn_lower_bounds, hgrn_head_norm, branch_w_mlstm, branch_w_hgrn, merge_w_gate, merge_b_gate, mix_w_out, mix_post_norm, xattn_pre_norm, xattn_mem_norm, xattn_w_q, xattn_w_kv, xattn_w_o, xattn_post_norm, ffn2_pre_norm, ffn2_w_gate_up, ffn2_w_down, ffn2_post_norm):
    batch, s_len, d = x.shape
    n_mem = mem.shape[1]
    depth = ffn1_pre_norm.shape[0]

    h = x.reshape(batch * s_len, d)
    mem2 = mem.reshape(batch * n_mem, d)
    u = rmsnorm_bf16(h, ffn1_pre_norm[0])
    for l in range(depth):
        y = _ffn(u, ffn1_w_gate_up, ffn1_w_down, l)
        h, u = combine(y, h, ffn1_post_norm[l], 0.5, mix_pre_norm[l])

        y = _token_mixers(u, l, batch, mix_w_in, mlstm_gate_bias[l], mlstm_conv[l], mlstm_head_norm[l],
                          hgrn_lower_bounds, hgrn_head_norm[l], branch_w_mlstm, branch_w_hgrn, merge_w_gate,
                          merge_b_gate[l], mix_w_out)
        h, u = combine(y, h, mix_post_norm[l], 1.0, xattn_pre_norm[l])

        h, u = _xattn(u, h, mem2, xattn_mem_norm[l], batch, l, xattn_w_q, xattn_w_kv, xattn_w_o,
                      xattn_post_norm[l], ffn2_pre_norm[l])

        y = _ffn(u, ffn2_w_gate_up, ffn2_w_down, l)
        nxt = ffn1_pre_norm[l + 1] if l + 1 < depth else None
        h, u = combine(y, h, ffn2_post_norm[l], 0.5, nxt)
    return h.reshape(batch, s_len, d)
```

```python
import functools
import math
from typing import NamedTuple

import jax
import jax.numpy as jnp
from jax import lax
from jax.experimental import pallas as pl
from jax.experimental.pallas import tpu as pltpu

EPS = 1e-6
F32 = jnp.float32
BF16 = jnp.bfloat16

V7X_VMEM_BYTES = 64 * 1024 * 1024
VMEM_LIMIT = V7X_VMEM_BYTES - 4 * 1024 * 1024
F32_BYTES = 4
BF16_BYTES = 2
LANES = 128
SUBLANES = 8

CONV_K = 4
M_HEADS = 8
M_DQK = 128
M_DV = 256
H_HEADS = 16
H_DK = 128
H_DV = 128
X_HEADS = 4
X_DH = 256
MLSTM_CHUNK = 256
HGRN_CHUNK = 64
CAST_BLOCK_ELEMS = 2 * 1024 * 1024
ROW_TILE = 256
ROW_PANELS = (2048, 1024, 512)
MIN_COLS_FOR_SINGLE_BUFFERED_PANEL = 24
VMEM_ESTIMATE_SLACK = 2 * 1024 * 1024
MM_ROW_CHUNK = 128
MM_TN = 512
MM_WIDE_TN = 1024
FFN_UP_TN = 256


def _cparams(sem):
    return pltpu.CompilerParams(dimension_semantics=sem, vmem_limit_bytes=VMEM_LIMIT)


def _dot(a, b):
    return jnp.dot(a, b, preferred_element_type=F32)


def _dot_nt(a, b):
    return lax.dot_general(a, b, (((1,), (1,)), ((), ())), preferred_element_type=F32)


def _dot_tn(a, b):
    return lax.dot_general(a, b, (((0,), (0,)), ((), ())), preferred_element_type=F32)


def _sigmoid(x):
    return 1.0 / (1.0 + jnp.exp(-x))


def _silu(x):
    return x * _sigmoid(x)


def _log_sigmoid(x):
    return jnp.minimum(x, 0.0) - jnp.log(1.0 + jnp.exp(-jnp.abs(x)))


def _rms(x):
    return x * lax.rsqrt(jnp.mean(x * x, axis=-1, keepdims=True) + EPS)


def _rmsnorm_kernel(x_ref, g_ref, o_ref):
    o_ref[...] = (_rms(x_ref[...]) * g_ref[...]).astype(o_ref.dtype)


def rmsnorm_bf16(x, g):
    m, d = x.shape
    tm = min(ROW_TILE, m)
    return pl.pallas_call(
        _rmsnorm_kernel,
        grid=(m // tm,),
        in_specs=[pl.BlockSpec((tm, d), lambda i: (i, 0)), pl.BlockSpec((1, d), lambda i: (0, 0))],
        out_specs=pl.BlockSpec((tm, d), lambda i: (i, 0)),
        out_shape=jax.ShapeDtypeStruct((m, d), BF16),
        compiler_params=_cparams(("parallel",)),
        name="rmsnorm_bf16",
    )(x, g.reshape(1, d))


def _cast_kernel(x_ref, o_ref):
    o_ref[...] = x_ref[...].astype(o_ref.dtype)


def _largest_divisor(n, cap, multiple):
    best = multiple
    for c in range(multiple, min(n, cap) + 1, multiple):
        if n % c == 0:
            best = c
    return best


def cast_tiled_bf16(w3, layer, tn):
    _, k, n = w3.shape
    tn = min(tn, n)
    assert n % tn == 0
    tk = _largest_divisor(k, max(SUBLANES, CAST_BLOCK_ELEMS // tn), 2 * SUBLANES)
    return pl.pallas_call(
        _cast_kernel,
        grid=(n // tn, k // tk),
        in_specs=[pl.BlockSpec((None, tk, tn), lambda j, i: (layer, i, j))],
        out_specs=pl.BlockSpec((None, tk, tn), lambda j, i: (j, i, 0)),
        out_shape=jax.ShapeDtypeStruct((n // tn, k, tn), BF16),
        compiler_params=_cparams(("parallel", "parallel")),
        name="cast_tiled_bf16",
    )(w3)


def _mm_kernel(*refs, n_w, epilogue, transposed, row_chunk, side_cast):
    a_ref = refs[0]
    w_refs = refs[1:1 + n_w]
    if side_cast:
        side_in, o_ref, side_out = refs[-3:]
        extra_refs = refs[1 + n_w:-3]
        tn_c = side_out.shape[2]
        for t in range(side_out.shape[0]):
            side_out[t] = side_in[:, t * tn_c:(t + 1) * tn_c].astype(side_out.dtype)
    else:
        o_ref = refs[-1]
        extra_refs = refs[1 + n_w:-1]
    ws = [(w[0] if transposed else w[...]).astype(BF16) for w in w_refs]
    mm = _dot_nt if transposed else _dot
    tm = a_ref.shape[0]
    rc = min(tm, row_chunk)
    for r0 in range(0, tm, rc):
        r = slice(r0, r0 + rc)
        accs = [mm(a_ref[r, :], w) for w in ws]
        ex = [e[...] if e.shape[0] == 1 else e[r, :] for e in extra_refs]
        o_ref[r, :] = epilogue(accs, ex).astype(o_ref.dtype)


class RawWeight(NamedTuple):
    w3: jax.Array
    layer: int
    tn: int


class RawWeightT(NamedTuple):
    w3t: jax.Array
    layer: int
    tn: int
    row0: int
    skip_at: int = 0
    skip_rows: int = 0


def _pick_row_panel(m, k, n_col, fixed_bytes, per_row_bytes):
    budget = VMEM_LIMIT - VMEM_ESTIMATE_SLACK
    for tm in ROW_PANELS:
        if m % tm:
            continue
        for buffers in (2, 1):
            if buffers == 1 and n_col < MIN_COLS_FOR_SINGLE_BUFFERED_PANEL:
                continue
            if fixed_bytes + tm * (per_row_bytes + buffers * k * BF16_BYTES) <= budget:
                return tm, buffers
    raise ValueError("no row panel fits VMEM")


def matmul(a, w, n_out, *, w_col_offsets=(0,), epilogue=None, extras=(), out_dtype=F32, side_cast=None,
           name="matmul"):
    m, k = a.shape
    transposed = isinstance(w, RawWeightT)
    tn = w.shape[2] if isinstance(w, jax.Array) else w.tn
    assert a.dtype == BF16 and n_out % tn == 0
    n_col = n_out // tn
    w_tile = len(w_col_offsets) * k * tn
    fixed_bytes = w_tile * (2 * BF16_BYTES if isinstance(w, jax.Array) else 2 * F32_BYTES + BF16_BYTES)
    per_row_bytes = 2 * tn * (jnp.dtype(out_dtype).itemsize
                              + sum(arr.dtype.itemsize for arr, _ in extras if arr.shape[0] > 1))
    if side_cast is not None:
        side_elems = side_cast.w3.shape[1] * side_cast.w3.shape[2]
        per_row_bytes += 2 * (F32_BYTES + BF16_BYTES) * side_elems // (m * n_col)
    acc_row_bytes = len(w_col_offsets) * tn * F32_BYTES
    if epilogue is None:
        per_row_bytes += acc_row_bytes
    else:
        fixed_bytes += MM_ROW_CHUNK * acc_row_bytes
    tm, a_buffers = _pick_row_panel(m, k, n_col, fixed_bytes, per_row_bytes)
    row_chunk = MM_ROW_CHUNK if epilogue is not None else tm
    if epilogue is None:
        epilogue = lambda accs, ex: accs[0]
    in_specs = [pl.BlockSpec((tm, k), lambda i, j: (i, 0), pipeline_mode=pl.Buffered(a_buffers))]
    args = [a]
    for off in w_col_offsets:
        assert off % tn == 0
        if transposed:
            assert w.w3t.shape[2] == k and w.row0 % SUBLANES == 0
            assert w.skip_at % tn == 0 and w.skip_rows % SUBLANES == 0

            def row_block(i, j, o):
                c = o + j * tn
                r = w.row0 + c + jnp.where(c >= w.skip_at, w.skip_rows, 0)
                return (w.layer, pl.multiple_of(r, SUBLANES), 0)

            spec = pl.BlockSpec((pl.Element(1), pl.Element(tn), pl.Element(k)), functools.partial(row_block, o=off))
        elif isinstance(w, RawWeight):
            assert w.w3.shape[1] == k
            spec = pl.BlockSpec((None, k, tn), functools.partial(lambda i, j, o: (w.layer, 0, j + o), o=off // tn))
        else:
            assert w.shape[1] == k
            spec = pl.BlockSpec((None, k, tn), functools.partial(lambda i, j, o: (j + o, 0, 0), o=off // tn))
        in_specs.append(spec)
        args.append(w if isinstance(w, jax.Array) else w[0])
    for arr, off in extras:
        assert off % tn == 0
        rows = arr.shape[0]
        if rows == 1:
            in_specs.append(pl.BlockSpec((1, tn), functools.partial(lambda i, j, o: (0, j + o), o=off // tn)))
        else:
            in_specs.append(pl.BlockSpec((tm, tn), functools.partial(lambda i, j, o: (i, j + o), o=off // tn)))
        args.append(arr)
    out_specs = pl.BlockSpec((tm, tn), lambda i, j: (i, j))
    out_shape = jax.ShapeDtypeStruct((m, n_out), out_dtype)
    if side_cast is not None:
        _, kc, nc = side_cast.w3.shape
        slab = kc // ((m // tm) * n_col)
        assert slab * (m // tm) * n_col == kc and slab % (2 * SUBLANES) == 0 and nc % side_cast.tn == 0
        in_specs.append(pl.BlockSpec((None, slab, nc), lambda i, j: (side_cast.layer, i * n_col + j, 0)))
        args.append(side_cast.w3)
        out_specs = [out_specs, pl.BlockSpec((nc // side_cast.tn, slab, side_cast.tn),
                                             lambda i, j: (0, i * n_col + j, 0))]
        out_shape = [out_shape, jax.ShapeDtypeStruct((nc // side_cast.tn, kc, side_cast.tn), BF16)]
    return pl.pallas_call(
        functools.partial(_mm_kernel, n_w=len(w_col_offsets), epilogue=epilogue, transposed=transposed,
                          row_chunk=row_chunk, side_cast=side_cast is not None),
        grid=(m // tm, n_col),
        in_specs=in_specs,
        out_specs=out_specs,
        out_shape=out_shape,
        compiler_params=_cparams(("parallel", "arbitrary")),
        name=name,
    )(*args)


def _epi_swiglu(accs, ex):
    return _silu(accs[0]) * accs[1]


def _epi_sigmoid_bias(accs, ex):
    return _sigmoid(accs[0] + ex[0])


def _epi_merge(accs, ex):
    gm, gh, ym = ex
    return gm.astype(F32) * ym + gh.astype(F32) * accs[0]


def _combine_kernel(y_ref, h_ref, post_ref, pre_ref, hn_ref, u_ref, *, scale):
    hn = h_ref[...] + scale * (_rms(y_ref[...].astype(F32)) * post_ref[...])
    hn_ref[...] = hn
    u_ref[...] = (_rms(hn) * pre_ref[...]).astype(u_ref.dtype)


def _combine_last_kernel(y_ref, h_ref, post_ref, hn_ref, *, scale):
    hn_ref[...] = h_ref[...] + scale * (_rms(y_ref[...].astype(F32)) * post_ref[...])


def combine(y, h, post, scale, next_pre=None):
    m, d = y.shape
    tm = min(ROW_TILE, m)
    row = pl.BlockSpec((tm, d), lambda i: (i, 0))
    vec = pl.BlockSpec((1, d), lambda i: (0, 0))
    if next_pre is None:
        return pl.pallas_call(
            functools.partial(_combine_last_kernel, scale=scale),
            grid=(m // tm,),
            in_specs=[row, row, vec],
            out_specs=row,
            out_shape=jax.ShapeDtypeStruct((m, d), F32),
            compiler_params=_cparams(("parallel",)),
            name="combine_last",
        )(y, h, post.reshape(1, d)), None
    return pl.pallas_call(
        functools.partial(_combine_kernel, scale=scale),
        grid=(m // tm,),
        in_specs=[row, row, vec, vec],
        out_specs=[row, row],
        out_shape=[jax.ShapeDtypeStruct((m, d), F32), jax.ShapeDtypeStruct((m, d), BF16)],
        compiler_params=_cparams(("parallel",)),
        name="combine",
    )(y, h, post.reshape(1, d), next_pre.reshape(1, d))


def _xattn_kernel(q_ref, k_ref, v_ref, o_ref, *, heads, dh):
    scale = dh ** -0.5
    for hd in range(heads):
        sl = slice(hd * dh, (hd + 1) * dh)
        s = _dot_nt(q_ref[:, sl], k_ref[:, sl]) * scale
        s = s - jnp.max(s, axis=-1, keepdims=True)
        p = jnp.exp(s)
        p = p / jnp.sum(p, axis=-1, keepdims=True)
        o_ref[:, sl] = _dot(p.astype(BF16), v_ref[:, sl]).astype(o_ref.dtype)


def _xattn_out_kernel(q_ref, k_ref, v_ref, wo_ref, h_ref, post_ref, pre_ref, hn_ref, u_ref, o_scr, *, heads, dh):
    _xattn_kernel(q_ref, k_ref, v_ref, o_scr, heads=heads, dh=dh)
    tm = o_scr.shape[0]
    for r0 in range(0, tm, tm // 2):
        r = slice(r0, r0 + tm // 2)
        y = _dot(o_scr[r, :], wo_ref[...])
        hn = h_ref[r, :] + _rms(y) * post_ref[...]
        hn_ref[r, :] = hn
        u_ref[r, :] = (_rms(hn) * pre_ref[...]).astype(u_ref.dtype)


def xattn_out(q, kv, w_o, h, post, next_pre, batch, heads, dh):
    rows, width = q.shape
    d = h.shape[1]
    s_len = rows // batch
    n_mem = kv.shape[0] // batch
    tm = min(ROW_TILE, s_len)
    nt = s_len // tm
    row = pl.BlockSpec((tm, d), lambda b, i: (b * nt + i, 0))
    vec = pl.BlockSpec((1, d), lambda b, i: (0, 0))
    return pl.pallas_call(
        functools.partial(_xattn_out_kernel, heads=heads, dh=dh),
        grid=(batch, nt),
        in_specs=[
            pl.BlockSpec((tm, width), lambda b, i: (b * nt + i, 0)),
            pl.BlockSpec((n_mem, width), lambda b, i: (b, 0)),
            pl.BlockSpec((n_mem, width), lambda b, i: (b, 1)),
            pl.BlockSpec((None, width, d), lambda b, i: (0, 0, 0), pipeline_mode=pl.Buffered(1)),
            row, vec, vec,
        ],
        out_specs=[row, row],
        out_shape=[jax.ShapeDtypeStruct((rows, d), F32), jax.ShapeDtypeStruct((rows, d), BF16)],
        scratch_shapes=[pltpu.VMEM((tm, width), BF16)],
        compiler_params=_cparams(("parallel", "arbitrary")),
        name="xattn_out",
    )(q, kv, kv, w_o, h, post.reshape(1, d), next_pre.reshape(1, d))


def _causal_conv_silu(x, tail, w):
    rows8 = lax.broadcasted_iota(jnp.int32, (SUBLANES, 1), 0)
    y = x * w[CONV_K - 1:CONV_K]
    for d in range(1, CONV_K):
        xs = pltpu.roll(x, d, 0)
        top = jnp.where(rows8 < d, pltpu.roll(tail, d, 0), xs[0:SUBLANES])
        xs = jnp.concatenate([top, xs[SUBLANES:]], axis=0)
        y = y + xs * w[CONV_K - 1 - d:CONV_K - d]
    return _silu(y)


def _mlstm_kernel(q_ref, k_ref, v_ref, og_ref, gc_ref, gr_ref, bc_ref, br_ref, conv_ref, norm_ref, out_ref,
                  c_scr, n_scr, m_scr, tq_scr, tk_scr, *, heads, dqk, dv):
    L = q_ref.shape[0]
    qk_w = heads * dqk

    @pl.when(pl.program_id(1) == 0)
    def _():
        c_scr[...] = jnp.zeros_like(c_scr)
        n_scr[...] = jnp.zeros_like(n_scr)
        m_scr[...] = jnp.zeros_like(m_scr)
        tq_scr[...] = jnp.zeros_like(tq_scr)
        tk_scr[...] = jnp.zeros_like(tk_scr)

    q_raw = q_ref[...]
    k_raw = k_ref[...]
    q_all = _causal_conv_silu(q_raw, tq_scr[...], conv_ref[:, 0:qk_w]) * (dqk ** -0.5)
    k_all = _causal_conv_silu(k_raw, tk_scr[...], conv_ref[:, qk_w:2 * qk_w])
    tq_scr[...] = q_raw[L - SUBLANES:L]
    tk_scr[...] = k_raw[L - SUBLANES:L]

    gc = gc_ref[...] + bc_ref[...]
    gr = gr_ref[0] + br_ref[...]
    t_idx = lax.broadcasted_iota(jnp.int32, (L, L), 0)
    s_idx = lax.broadcasted_iota(jnp.int32, (L, L), 1)
    causal = s_idx <= t_idx
    tri = causal.astype(F32)
    bc_all = jnp.dot(tri, _log_sigmoid(gc), precision=lax.Precision.HIGHEST, preferred_element_type=F32)
    br_all = lax.dot_general(_log_sigmoid(gr), tri, (((1,), (1,)), ((), ())),
                             precision=lax.Precision.HIGHEST, preferred_element_type=F32)

    for hd in range(heads):
        qs = slice(hd * dqk, (hd + 1) * dqk)
        vs = slice(hd * dv, (hd + 1) * dv)
        q = q_all[:, qs]
        k = k_all[:, qs]
        qb = q.astype(BF16)
        kb = k.astype(BF16)
        vb = v_ref[:, vs].astype(BF16)
        b_c = bc_all[:, heads + hd:heads + hd + 1]
        li_c = gc[:, hd:hd + 1]
        b_r = br_all[heads + hd:heads + hd + 1, :]
        li_r = gr[hd:hd + 1, :]
        m_prev = m_scr[hd:hd + 1, 0:1]
        n_prev = n_scr[hd:hd + 1, :]

        d = jnp.where(causal, b_c - b_r + li_r, -jnp.inf)
        m_inter = b_c + m_prev
        m_t = jnp.maximum(m_inter, jnp.max(d, axis=1, keepdims=True))
        w_intra = jnp.exp(d - m_t)
        w_inter = jnp.exp(m_inter - m_t)
        s = _dot_nt(qb, kb) * w_intra
        num = _dot(s.astype(BF16), vb) + w_inter * _dot(qb, c_scr[hd].astype(BF16))
        den = jnp.sum(s, axis=1, keepdims=True) + w_inter * jnp.sum(q * n_prev, axis=1, keepdims=True)
        hh = num * (1.0 / jnp.maximum(jnp.abs(den), jnp.exp(-m_t)))
        out = _rms(hh) * norm_ref[:, vs] * _sigmoid(og_ref[:, vs])
        out_ref[:, vs] = out.astype(out_ref.dtype)

        g = b_c[L - 1:L, :]
        a_c = g - b_c + li_c
        m_new = jnp.maximum(g + m_prev, jnp.max(a_c, axis=0, keepdims=True))
        decay = jnp.exp(g + m_prev - m_new)
        kw = k * jnp.exp(a_c - m_new)
        c_scr[hd] = decay * c_scr[hd] + _dot_tn(kw.astype(BF16), vb)
        n_scr[hd:hd + 1, :] = decay * n_prev + jnp.sum(kw, axis=0, keepdims=True)
        m_scr[hd:hd + 1, :] = jnp.broadcast_to(m_new, (1, LANES))


def mlstm_scan(proj, gates_col, gates_row, gate_bias, conv_w, head_norm, batch, *, chunk, heads=M_HEADS,
               dqk=M_DQK, dv=M_DV):
    rows = proj.shape[0]
    s_len = rows // batch
    L = min(chunk, s_len)
    nc = s_len // L
    qk_w = heads * dqk
    v_w = heads * dv
    assert qk_w % LANES == 0 and v_w % qk_w == 0
    bias_col = gate_bias.reshape(1, 2 * heads)
    bias_row = gate_bias.reshape(2 * heads, 1)
    kern = functools.partial(_mlstm_kernel, heads=heads, dqk=dqk, dv=dv)
    return pl.pallas_call(
        kern,
        grid=(batch, nc),
        in_specs=[
            pl.BlockSpec((L, qk_w), lambda b, c: (b * nc + c, 0)),
            pl.BlockSpec((L, qk_w), lambda b, c: (b * nc + c, 1)),
            pl.BlockSpec((L, v_w), lambda b, c: (b * nc + c, 2 * qk_w // v_w)),
            pl.BlockSpec((L, v_w), lambda b, c: (b * nc + c, 2 * qk_w // v_w + 1)),
            pl.BlockSpec((L, 2 * heads), lambda b, c: (b * nc + c, 0)),
            pl.BlockSpec((1, 2 * heads, L), lambda b, c: (b, 0, c)),
            pl.BlockSpec((1, 2 * heads), lambda b, c: (0, 0)),
            pl.BlockSpec((2 * heads, 1), lambda b, c: (0, 0)),
            pl.BlockSpec((CONV_K, 2 * qk_w), lambda b, c: (0, 0)),
            pl.BlockSpec((1, v_w), lambda b, c: (0, 0)),
        ],
        out_specs=pl.BlockSpec((L, v_w), lambda b, c: (b * nc + c, 0)),
        out_shape=jax.ShapeDtypeStruct((rows, v_w), BF16),
        scratch_shapes=[
            pltpu.VMEM((heads, dqk, dv), F32),
            pltpu.VMEM((heads, dqk), F32),
            pltpu.VMEM((heads, LANES), F32),
            pltpu.VMEM((SUBLANES, qk_w), F32),
            pltpu.VMEM((SUBLANES, qk_w), F32),
        ],
        compiler_params=_cparams(("parallel", "arbitrary")),
        name="mlstm_scan",
    )(proj, proj, proj, proj, gates_col, gates_row, bias_col, bias_row, conv_w, head_norm.reshape(1, v_w))


def _hgrn_kernel(q_ref, f_ref, i_ref, g_ref, lb_ref, norm_ref, out_ref, st_scr, *, heads, dk, dv, layer):
    L = q_ref.shape[0]

    @pl.when(pl.program_id(1) == 0)
    def _():
        st_scr[...] = jnp.zeros_like(st_scr)

    lbp = lb_ref[...]
    e = jnp.exp(lbp - jnp.max(lbp, axis=0, keepdims=True))
    p = e / jnp.sum(e, axis=0, keepdims=True)
    lb = jnp.zeros((1, lbp.shape[1]), F32)
    for j in range(1, layer + 1):
        lb = lb + p[j:j + 1]

    f = lb + (1.0 - lb) * _sigmoid(f_ref[...])
    lg = jnp.log(f)
    k_all = 1.0 - f
    t_idx = lax.broadcasted_iota(jnp.int32, (L, L), 0)
    s_idx = lax.broadcasted_iota(jnp.int32, (L, L), 1)
    tri = (s_idx <= t_idx).astype(F32)
    b_all = jnp.dot(tri, lg, precision=lax.Precision.HIGHEST, preferred_element_type=F32)
    level = jnp.where(t_idx > s_idx, 31 - lax.clz(t_idx ^ s_idx), jnp.where(t_idx == s_idx, -1, -2))
    row = lax.broadcasted_iota(jnp.int32, (L, 1), 0)
    n_levels = int(math.log2(L))
    k_slices = [slice(hd * dk, (hd + 1) * dk) for hd in range(heads)]
    q_all = q_ref[...]
    qb_all = q_all.astype(BF16)

    def scores(qx, kx, lw, acc):
        keep = level == lw
        return [jnp.where(keep, _dot_nt(qx[:, sl], kx[:, sl]), acc[i]) for i, sl in enumerate(k_slices)]

    a = scores(qb_all, k_all.astype(BF16), -1, [0.0] * heads)
    a = scores(qb_all, (k_all * pltpu.roll(f, L - 1, 0)).astype(BF16), 0, a)
    fw = jnp.where((row & 1) == 1, pltpu.roll(b_all, 1, 0), b_all)
    for lw in range(1, n_levels):
        w = 1 << lw
        odd = ((row >> lw) & 1) == 1
        gw = pltpu.roll(fw, L - w, 0)
        e = jnp.exp(jnp.where(odd, b_all, gw) - jnp.where(odd, fw, b_all))
        a = scores((q_all * e).astype(BF16), (k_all * e).astype(BF16), lw, a)
        fw = jnp.where(odd, pltpu.roll(fw, w, 0), fw)

    g = b_all[L - 1:L, :]
    qe_all = (q_all * jnp.exp(b_all)).astype(BF16)
    ke_all = (k_all * jnp.exp(g - b_all)).astype(BF16)
    eg = jnp.exp(g)
    ib_all = i_ref[...].astype(BF16)
    for hd in range(heads):
        ks = k_slices[hd]
        vs = slice(hd * dv, (hd + 1) * dv)
        st = st_scr[hd]
        o = _dot(a[hd].astype(BF16), ib_all[:, vs]) + _dot_nt(qe_all[:, ks], st.astype(BF16))
        st_scr[hd] = st * eg[:, ks] + _dot_tn(ib_all[:, vs], ke_all[:, ks])
        out = _rms(o) * norm_ref[:, vs] * _silu(g_ref[:, vs])
        out_ref[:, vs] = out.astype(out_ref.dtype)


def hgrn_scan(proj, col_block0, lower_bounds, head_norm, batch, layer, *, chunk, heads=H_HEADS, dk=H_DK, dv=H_DV):
    rows = proj.shape[0]
    s_len = rows // batch
    L = min(chunk, s_len)
    nc = s_len // L
    assert L & (L - 1) == 0 and dk == dv
    w = heads * dk
    depth = lower_bounds.shape[0]
    kern = functools.partial(_hgrn_kernel, heads=heads, dk=dk, dv=dv, layer=layer)
    blk = lambda o: pl.BlockSpec((L, w), functools.partial(lambda b, c, o: (b * nc + c, o), o=col_block0 + o))
    return pl.pallas_call(
        kern,
        grid=(batch, nc),
        in_specs=[blk(0), blk(1), blk(2), blk(3),
                  pl.BlockSpec((depth, w), lambda b, c: (0, 0)),
                  pl.BlockSpec((1, w), lambda b, c: (0, 0))],
        out_specs=pl.BlockSpec((L, w), lambda b, c: (b * nc + c, 0)),
        out_shape=jax.ShapeDtypeStruct((rows, w), BF16),
        scratch_shapes=[pltpu.VMEM((heads, dv, dk), F32)],
        compiler_params=_cparams(("parallel", "arbitrary")),
        name="hgrn_scan",
    )(proj, proj, proj, proj, lower_bounds, head_norm.reshape(1, w))


def _ffn(u, w_gate_up, w_down, layer):
    d_ff, d = w_down.shape[1:]
    act, w_dn = matmul(u, RawWeight(w_gate_up, layer, FFN_UP_TN), d_ff, w_col_offsets=(0, d_ff),
                       epilogue=_epi_swiglu, out_dtype=BF16, side_cast=RawWeight(w_down, layer, MM_TN),
                       name="ffn_gate_up")
    return matmul(act, w_dn, d, out_dtype=BF16, name="ffn_down")


def _token_mixers(u, layer, batch, w_in, gate_bias, conv_w, m_norm, lower_bounds, h_norm,
                  w_bm, w_bh, w_gate, b_gate, w_out):
    rows, d = u.shape
    s_len = rows // batch
    gate_lo = 2 * M_HEADS * M_DQK + 2 * M_HEADS * M_DV
    n_gates = 2 * M_HEADS
    w_in_t = jnp.swapaxes(w_in, 1, 2)
    proj = matmul(u, RawWeightT(w_in_t, layer, MM_TN, 0, gate_lo, n_gates), w_in.shape[2] - n_gates,
                  name="mix_in_proj")
    gates_col = matmul(u, RawWeightT(w_in_t, layer, n_gates, gate_lo), n_gates, name="mix_gate_proj")
    gates_row = gates_col.reshape(batch, s_len, n_gates).transpose(0, 2, 1)
    ym = mlstm_scan(proj, gates_col, gates_row, gate_bias, conv_w, m_norm, batch, chunk=MLSTM_CHUNK)
    yh = hgrn_scan(proj, gate_lo // (H_HEADS * H_DK), lower_bounds, h_norm, batch, layer, chunk=HGRN_CHUNK)
    gate = matmul(u, RawWeight(w_gate, layer, MM_TN), 2 * d, epilogue=_epi_sigmoid_bias,
                  extras=((b_gate.reshape(1, 2 * d), 0),), out_dtype=BF16, name="merge_gate")
    y_m = matmul(ym, cast_tiled_bf16(w_bm, layer, MM_WIDE_TN), d, name="branch_mlstm")
    z = matmul(yh, cast_tiled_bf16(w_bh, layer, MM_WIDE_TN), d, epilogue=_epi_merge,
               extras=((gate, 0), (gate, d), (y_m, 0)), out_dtype=BF16, name="branch_hgrn_merge")
    return matmul(z, RawWeight(w_out, layer, MM_TN), d, out_dtype=BF16, name="mix_out")


def _xattn(u, h, mem2, mem_norm, batch, layer, w_q, w_kv, w_o, post, next_pre):
    memn = rmsnorm_bf16(mem2, mem_norm)
    kv = matmul(memn, RawWeight(w_kv, layer, MM_TN), w_kv.shape[2], out_dtype=BF16, name="xattn_kv")
    q = matmul(u, cast_tiled_bf16(w_q, layer, MM_WIDE_TN), w_q.shape[2], out_dtype=BF16, name="xattn_q")
    return xattn_out(q, kv, cast_tiled_bf16(w_o, layer, w_o.shape[2]), h, post, next_pre, batch, X_HEADS, X_DH)


def kernel(x, mem, ffn1_pre_norm, ffn1_w_gate_up, ffn1_w_down, ffn1_post_norm, mix_pre_norm, mix_w_in, mlstm_gate_bias, mlstm_conv, mlstm_head_norm, hgrn_lower_bounds, hgrn_head_norm, branch_w_mlstm, branch_w_hgrn, merge_w_gate, merge_b_gate, mix_w_out, mix_post_norm, xattn_pre_norm, xattn_mem_norm, xattn_w_q, xattn_w_kv, xattn_w_o, xattn_post_norm, ffn2_pre_norm, ffn2_w_gate_up, ffn2_w_down, ffn2_post_norm):
    batch, s_len, d = x.shape
    n_mem = mem.shape[1]
    depth = ffn1_pre_norm.shape[0]

    h = x.reshape(batch * s_len, d)
    mem2 = mem.reshape(batch * n_mem, d)
    u = rmsnorm_bf16(h, ffn1_pre_norm[0])
    for l in range(depth):
        y = _ffn(u, ffn1_w_gate_up, ffn1_w_down, l)
        h, u = combine(y, h, ffn1_post_norm[l], 0.5, mix_pre_norm[l])

        y = _token_mixers(u, l, batch, mix_w_in, mlstm_gate_bias[l], mlstm_conv[l], mlstm_head_norm[l],
                          hgrn_lower_bounds, hgrn_head_norm[l], branch_w_mlstm, branch_w_hgrn, merge_w_gate,
                          merge_b_gate[l], mix_w_out)
        h, u = combine(y, h, mix_post_norm[l], 1.0, xattn_pre_norm[l])

        h, u = _xattn(u, h, mem2, xattn_mem_norm[l], batch, l, xattn_w_q, xattn_w_kv, xattn_w_o,
                      xattn_post_norm[l], ffn2_pre_norm[l])

        y = _ffn(u, ffn2_w_gate_up, ffn2_w_down, l)
        nxt = ffn1_pre_norm[l + 1] if l + 1 < depth else None
        h, u = combine(y, h, ffn2_post_norm[l], 0.5, nxt)
    return h.reshape(batch, s_len, d)
```

```python
import functools
import math
from typing import NamedTuple

import jax
import jax.numpy as jnp
from jax import lax
from jax.experimental import pallas as pl
from jax.experimental.pallas import tpu as pltpu

EPS = 1e-6
F32 = jnp.float32
BF16 = jnp.bfloat16

V7X_VMEM_BYTES = 64 * 1024 * 1024
VMEM_LIMIT = V7X_VMEM_BYTES - 4 * 1024 * 1024
F32_BYTES = 4
BF16_BYTES = 2
LANES = 128
SUBLANES = 8

CONV_K = 4
M_HEADS = 8
M_DQK = 128
M_DV = 256
H_HEADS = 16
H_DK = 128
H_DV = 128
X_HEADS = 4
X_DH = 256
MLSTM_CHUNK = 256
HGRN_CHUNK = 64
CAST_BLOCK_ELEMS = 2 * 1024 * 1024
ROW_TILE = 256
ROW_PANELS = (2048, 1024, 512)
MIN_COLS_FOR_SINGLE_BUFFERED_PANEL = 24
VMEM_ESTIMATE_SLACK = 2 * 1024 * 1024
MM_ROW_CHUNK = 128
MM_TN = 512
MM_WIDE_TN = 1024
FFN_UP_TN = 256


def _cparams(sem):
    return pltpu.CompilerParams(dimension_semantics=sem, vmem_limit_bytes=VMEM_LIMIT)


def _dot(a, b):
    return jnp.dot(a, b, preferred_element_type=F32)


def _dot_nt(a, b):
    return lax.dot_general(a, b, (((1,), (1,)), ((), ())), preferred_element_type=F32)


def _dot_tn(a, b):
    return lax.dot_general(a, b, (((0,), (0,)), ((), ())), preferred_element_type=F32)


def _sigmoid(x):
    return 1.0 / (1.0 + jnp.exp(-x))


def _silu(x):
    return x * _sigmoid(x)


def _log_sigmoid(x):
    return jnp.minimum(x, 0.0) - jnp.log(1.0 + jnp.exp(-jnp.abs(x)))


def _rms(x):
    return x * lax.rsqrt(jnp.mean(x * x, axis=-1, keepdims=True) + EPS)


def _rmsnorm_kernel(x_ref, g_ref, o_ref):
    o_ref[...] = (_rms(x_ref[...]) * g_ref[...]).astype(o_ref.dtype)


def rmsnorm_bf16(x, g):
    m, d = x.shape
    tm = min(ROW_TILE, m)
    return pl.pallas_call(
        _rmsnorm_kernel,
        grid=(m // tm,),
        in_specs=[pl.BlockSpec((tm, d), lambda i: (i, 0)), pl.BlockSpec((1, d), lambda i: (0, 0))],
        out_specs=pl.BlockSpec((tm, d), lambda i: (i, 0)),
        out_shape=jax.ShapeDtypeStruct((m, d), BF16),
        compiler_params=_cparams(("parallel",)),
        name="rmsnorm_bf16",
    )(x, g.reshape(1, d))


def _cast_kernel(x_ref, o_ref):
    o_ref[...] = x_ref[...].astype(o_ref.dtype)


def _largest_divisor(n, cap, multiple):
    best = multiple
    for c in range(multiple, min(n, cap) + 1, multiple):
        if n % c == 0:
            best = c
    return best


def cast_tiled_bf16(w3, layer, tn):
    _, k, n = w3.shape
    tn = min(tn, n)
    assert n % tn == 0
    tk = _largest_divisor(k, max(SUBLANES, CAST_BLOCK_ELEMS // tn), 2 * SUBLANES)
    return pl.pallas_call(
        _cast_kernel,
        grid=(n // tn, k // tk),
        in_specs=[pl.BlockSpec((None, tk, tn), lambda j, i: (layer, i, j))],
        out_specs=pl.BlockSpec((None, tk, tn), lambda j, i: (j, i, 0)),
        out_shape=jax.ShapeDtypeStruct((n // tn, k, tn), BF16),
        compiler_params=_cparams(("parallel", "parallel")),
        name="cast_tiled_bf16",
    )(w3)


def _mm_kernel(*refs, n_w, epilogue, transposed, row_chunk, side_cast, side_proj):
    a_ref = refs[0]
    w_refs = refs[1:1 + n_w]
    if side_cast:
        side_in, o_ref, side_out = refs[-3:]
        extra_refs = refs[1 + n_w:-3]
        tn_c = side_out.shape[2]
        for t in range(side_out.shape[0]):
            side_out[t] = side_in[:, t * tn_c:(t + 1) * tn_c].astype(side_out.dtype)
    elif side_proj:
        proj_w, o_ref, proj_out = refs[-3:]
        extra_refs = refs[1 + n_w:-3]

        @pl.when(pl.program_id(1) == 0)
        def _():
            proj_out[...] = _dot_nt(a_ref[...], proj_w[0].astype(BF16))
    else:
        o_ref = refs[-1]
        extra_refs = refs[1 + n_w:-1]
    ws = [(w[0] if transposed else w[...]).astype(BF16) for w in w_refs]
    mm = _dot_nt if transposed else _dot
    tm = a_ref.shape[0]
    rc = min(tm, row_chunk)
    for r0 in range(0, tm, rc):
        r = slice(r0, r0 + rc)
        accs = [mm(a_ref[r, :], w) for w in ws]
        ex = [e[...] if e.shape[0] == 1 else e[r, :] for e in extra_refs]
        o_ref[r, :] = epilogue(accs, ex).astype(o_ref.dtype)


class RawWeight(NamedTuple):
    w3: jax.Array
    layer: int
    tn: int


class RawWeightT(NamedTuple):
    w3t: jax.Array
    layer: int
    tn: int
    row0: int
    skip_at: int = 0
    skip_rows: int = 0


def _pick_row_panel(m, k, n_col, fixed_bytes, per_row_bytes):
    budget = VMEM_LIMIT - VMEM_ESTIMATE_SLACK
    for tm in ROW_PANELS:
        if m % tm:
            continue
        for buffers in (2, 1):
            if buffers == 1 and n_col < MIN_COLS_FOR_SINGLE_BUFFERED_PANEL:
                continue
            if fixed_bytes + tm * (per_row_bytes + buffers * k * BF16_BYTES) <= budget:
                return tm, buffers
    raise ValueError("no row panel fits VMEM")


def matmul(a, w, n_out, *, w_col_offsets=(0,), epilogue=None, extras=(), out_dtype=F32, side_cast=None,
           side_proj=None, name="matmul"):
    m, k = a.shape
    transposed = isinstance(w, RawWeightT)
    tn = w.shape[2] if isinstance(w, jax.Array) else w.tn
    assert a.dtype == BF16 and n_out % tn == 0
    n_col = n_out // tn
    w_tile = len(w_col_offsets) * k * tn
    fixed_bytes = w_tile * (2 * BF16_BYTES if isinstance(w, jax.Array) else 2 * F32_BYTES + BF16_BYTES)
    per_row_bytes = 2 * tn * (jnp.dtype(out_dtype).itemsize
                              + sum(arr.dtype.itemsize for arr, _ in extras if arr.shape[0] > 1))
    if side_cast is not None:
        side_elems = side_cast.w3.shape[1] * side_cast.w3.shape[2]
        per_row_bytes += 2 * (F32_BYTES + BF16_BYTES) * side_elems // (m * n_col)
    if side_proj is not None:
        per_row_bytes += 2 * LANES * F32_BYTES
    acc_row_bytes = len(w_col_offsets) * tn * F32_BYTES
    if epilogue is None:
        per_row_bytes += acc_row_bytes
    else:
        fixed_bytes += MM_ROW_CHUNK * acc_row_bytes
    tm, a_buffers = _pick_row_panel(m, k, n_col, fixed_bytes, per_row_bytes)
    row_chunk = MM_ROW_CHUNK if epilogue is not None else tm
    if epilogue is None:
        epilogue = lambda accs, ex: accs[0]
    in_specs = [pl.BlockSpec((tm, k), lambda i, j: (i, 0), pipeline_mode=pl.Buffered(a_buffers))]
    args = [a]
    for off in w_col_offsets:
        assert off % tn == 0
        if transposed:
            assert w.w3t.shape[2] == k and w.row0 % SUBLANES == 0
            assert w.skip_at % tn == 0 and w.skip_rows % SUBLANES == 0

            def row_block(i, j, o):
                c = o + j * tn
                r = w.row0 + c + jnp.where(c >= w.skip_at, w.skip_rows, 0)
                return (w.layer, pl.multiple_of(r, SUBLANES), 0)

            spec = pl.BlockSpec((pl.Element(1), pl.Element(tn), pl.Element(k)), functools.partial(row_block, o=off))
        elif isinstance(w, RawWeight):
            assert w.w3.shape[1] == k
            spec = pl.BlockSpec((None, k, tn), functools.partial(lambda i, j, o: (w.layer, 0, j + o), o=off // tn))
        else:
            assert w.shape[1] == k
            spec = pl.BlockSpec((None, k, tn), functools.partial(lambda i, j, o: (j + o, 0, 0), o=off // tn))
        in_specs.append(spec)
        args.append(w if isinstance(w, jax.Array) else w[0])
    for arr, off in extras:
        assert off % tn == 0
        rows = arr.shape[0]
        if rows == 1:
            in_specs.append(pl.BlockSpec((1, tn), functools.partial(lambda i, j, o: (0, j + o), o=off // tn)))
        else:
            in_specs.append(pl.BlockSpec((tm, tn), functools.partial(lambda i, j, o: (i, j + o), o=off // tn)))
        args.append(arr)
    out_specs = pl.BlockSpec((tm, tn), lambda i, j: (i, j))
    out_shape = jax.ShapeDtypeStruct((m, n_out), out_dtype)
    if side_cast is not None:
        _, kc, nc = side_cast.w3.shape
        slab = kc // ((m // tm) * n_col)
        assert slab * (m // tm) * n_col == kc and slab % (2 * SUBLANES) == 0 and nc % side_cast.tn == 0
        in_specs.append(pl.BlockSpec((None, slab, nc), lambda i, j: (side_cast.layer, i * n_col + j, 0)))
        args.append(side_cast.w3)
        out_specs = [out_specs, pl.BlockSpec((nc // side_cast.tn, slab, side_cast.tn),
                                             lambda i, j: (0, i * n_col + j, 0))]
        out_shape = [out_shape, jax.ShapeDtypeStruct((nc // side_cast.tn, kc, side_cast.tn), BF16)]
    if side_proj is not None:
        assert side_cast is None and side_proj.w3t.shape[2] == k and side_proj.row0 % SUBLANES == 0
        in_specs.append(pl.BlockSpec((pl.Element(1), pl.Element(side_proj.tn), pl.Element(k)),
                                     lambda i, j: (side_proj.layer, side_proj.row0, 0)))
        args.append(side_proj.w3t)
        out_specs = [out_specs, pl.BlockSpec((tm, side_proj.tn), lambda i, j: (i, 0))]
        out_shape = [out_shape, jax.ShapeDtypeStruct((m, side_proj.tn), F32)]
    return pl.pallas_call(
        functools.partial(_mm_kernel, n_w=len(w_col_offsets), epilogue=epilogue, transposed=transposed,
                          row_chunk=row_chunk, side_cast=side_cast is not None, side_proj=side_proj is not None),
        grid=(m // tm, n_col),
        in_specs=in_specs,
        out_specs=out_specs,
        out_shape=out_shape,
        compiler_params=_cparams(("parallel", "arbitrary")),
        name=name,
    )(*args)


def _epi_swiglu(accs, ex):
    return _silu(accs[0]) * accs[1]


def _epi_sigmoid_bias(accs, ex):
    return _sigmoid(accs[0] + ex[0])


def _epi_merge(accs, ex):
    gm, gh, ym = ex
    return gm.astype(F32) * ym + gh.astype(F32) * accs[0]


def _combine_kernel(y_ref, h_ref, post_ref, pre_ref, hn_ref, u_ref, *, scale):
    hn = h_ref[...] + scale * (_rms(y_ref[...].astype(F32)) * post_ref[...])
    hn_ref[...] = hn
    u_ref[...] = (_rms(hn) * pre_ref[...]).astype(u_ref.dtype)


def _combine_last_kernel(y_ref, h_ref, post_ref, hn_ref, *, scale):
    hn_ref[...] = h_ref[...] + scale * (_rms(y_ref[...].astype(F32)) * post_ref[...])


def combine(y, h, post, scale, next_pre=None):
    m, d = y.shape
    tm = min(ROW_TILE, m)
    row = pl.BlockSpec((tm, d), lambda i: (i, 0))
    vec = pl.BlockSpec((1, d), lambda i: (0, 0))
    if next_pre is None:
        return pl.pallas_call(
            functools.partial(_combine_last_kernel, scale=scale),
            grid=(m // tm,),
            in_specs=[row, row, vec],
            out_specs=row,
            out_shape=jax.ShapeDtypeStruct((m, d), F32),
            compiler_params=_cparams(("parallel",)),
            name="combine_last",
        )(y, h, post.reshape(1, d)), None
    return pl.pallas_call(
        functools.partial(_combine_kernel, scale=scale),
        grid=(m // tm,),
        in_specs=[row, row, vec, vec],
        out_specs=[row, row],
        out_shape=[jax.ShapeDtypeStruct((m, d), F32), jax.ShapeDtypeStruct((m, d), BF16)],
        compiler_params=_cparams(("parallel",)),
        name="combine",
    )(y, h, post.reshape(1, d), next_pre.reshape(1, d))


def _xattn_kernel(q_ref, k_ref, v_ref, o_ref, *, heads, dh):
    scale = dh ** -0.5
    for hd in range(heads):
        sl = slice(hd * dh, (hd + 1) * dh)
        s = _dot_nt(q_ref[:, sl], k_ref[:, sl]) * scale
        s = s - jnp.max(s, axis=-1, keepdims=True)
        p = jnp.exp(s)
        p = p / jnp.sum(p, axis=-1, keepdims=True)
        o_ref[:, sl] = _dot(p.astype(BF16), v_ref[:, sl]).astype(o_ref.dtype)


def _xattn_out_kernel(q_ref, k_ref, v_ref, wo_ref, h_ref, post_ref, pre_ref, hn_ref, u_ref, o_scr, *, heads, dh):
    _xattn_kernel(q_ref, k_ref, v_ref, o_scr, heads=heads, dh=dh)
    tm = o_scr.shape[0]
    for r0 in range(0, tm, tm // 2):
        r = slice(r0, r0 + tm // 2)
        y = _dot(o_scr[r, :], wo_ref[...])
        hn = h_ref[r, :] + _rms(y) * post_ref[...]
        hn_ref[r, :] = hn
        u_ref[r, :] = (_rms(hn) * pre_ref[...]).astype(u_ref.dtype)


def xattn_out(q, kv, w_o, h, post, next_pre, batch, heads, dh):
    rows, width = q.shape
    d = h.shape[1]
    s_len = rows // batch
    n_mem = kv.shape[0] // batch
    tm = min(ROW_TILE, s_len)
    nt = s_len // tm
    row = pl.BlockSpec((tm, d), lambda b, i: (b * nt + i, 0))
    vec = pl.BlockSpec((1, d), lambda b, i: (0, 0))
    return pl.pallas_call(
        functools.partial(_xattn_out_kernel, heads=heads, dh=dh),
        grid=(batch, nt),
        in_specs=[
            pl.BlockSpec((tm, width), lambda b, i: (b * nt + i, 0)),
            pl.BlockSpec((n_mem, width), lambda b, i: (b, 0)),
            pl.BlockSpec((n_mem, width), lambda b, i: (b, 1)),
            pl.BlockSpec((None, width, d), lambda b, i: (0, 0, 0), pipeline_mode=pl.Buffered(1)),
            row, vec, vec,
        ],
        out_specs=[row, row],
        out_shape=[jax.ShapeDtypeStruct((rows, d), F32), jax.ShapeDtypeStruct((rows, d), BF16)],
        scratch_shapes=[pltpu.VMEM((tm, width), BF16)],
        compiler_params=_cparams(("parallel", "arbitrary")),
        name="xattn_out",
    )(q, kv, kv, w_o, h, post.reshape(1, d), next_pre.reshape(1, d))


def _causal_conv_silu(x, tail, w):
    rows8 = lax.broadcasted_iota(jnp.int32, (SUBLANES, 1), 0)
    y = x * w[CONV_K - 1:CONV_K]
    for d in range(1, CONV_K):
        xs = pltpu.roll(x, d, 0)
        top = jnp.where(rows8 < d, pltpu.roll(tail, d, 0), xs[0:SUBLANES])
        xs = jnp.concatenate([top, xs[SUBLANES:]], axis=0)
        y = y + xs * w[CONV_K - 1 - d:CONV_K - d]
    return _silu(y)


def _mlstm_kernel(q_ref, k_ref, v_ref, og_ref, gc_ref, gr_ref, bc_ref, br_ref, conv_ref, norm_ref, out_ref,
                  c_scr, n_scr, m_scr, tq_scr, tk_scr, *, heads, dqk, dv):
    L = q_ref.shape[0]
    qk_w = heads * dqk

    @pl.when(pl.program_id(1) == 0)
    def _():
        c_scr[...] = jnp.zeros_like(c_scr)
        n_scr[...] = jnp.zeros_like(n_scr)
        m_scr[...] = jnp.zeros_like(m_scr)
        tq_scr[...] = jnp.zeros_like(tq_scr)
        tk_scr[...] = jnp.zeros_like(tk_scr)

    q_raw = q_ref[...]
    k_raw = k_ref[...]
    q_all = _causal_conv_silu(q_raw, tq_scr[...], conv_ref[:, 0:qk_w]) * (dqk ** -0.5)
    k_all = _causal_conv_silu(k_raw, tk_scr[...], conv_ref[:, qk_w:2 * qk_w])
    tq_scr[...] = q_raw[L - SUBLANES:L]
    tk_scr[...] = k_raw[L - SUBLANES:L]

    gc = gc_ref[...] + bc_ref[...]
    gr = gr_ref[0] + br_ref[...]
    t_idx = lax.broadcasted_iota(jnp.int32, (L, L), 0)
    s_idx = lax.broadcasted_iota(jnp.int32, (L, L), 1)
    causal = s_idx <= t_idx
    tri = causal.astype(F32)
    bc_all = jnp.dot(tri, _log_sigmoid(gc), precision=lax.Precision.HIGHEST, preferred_element_type=F32)
    br_all = lax.dot_general(_log_sigmoid(gr), tri, (((1,), (1,)), ((), ())),
                             precision=lax.Precision.HIGHEST, preferred_element_type=F32)

    for hd in range(heads):
        qs = slice(hd * dqk, (hd + 1) * dqk)
        vs = slice(hd * dv, (hd + 1) * dv)
        q = q_all[:, qs]
        k = k_all[:, qs]
        qb = q.astype(BF16)
        kb = k.astype(BF16)
        vb = v_ref[:, vs].astype(BF16)
        b_c = bc_all[:, heads + hd:heads + hd + 1]
        li_c = gc[:, hd:hd + 1]
        b_r = br_all[heads + hd:heads + hd + 1, :]
        li_r = gr[hd:hd + 1, :]
        m_prev = m_scr[hd:hd + 1, 0:1]
        n_prev = n_scr[hd:hd + 1, :]

        d = jnp.where(causal, b_c - b_r + li_r, -jnp.inf)
        m_inter = b_c + m_prev
        m_t = jnp.maximum(m_inter, jnp.max(d, axis=1, keepdims=True))
        w_intra = jnp.exp(d - m_t)
        w_inter = jnp.exp(m_inter - m_t)
        s = _dot_nt(qb, kb) * w_intra
        num = _dot(s.astype(BF16), vb) + w_inter * _dot(qb, c_scr[hd].astype(BF16))
        den = jnp.sum(s, axis=1, keepdims=True) + w_inter * jnp.sum(q * n_prev, axis=1, keepdims=True)
        hh = num * (1.0 / jnp.maximum(jnp.abs(den), jnp.exp(-m_t)))
        out = _rms(hh) * norm_ref[:, vs] * _sigmoid(og_ref[:, vs])
        out_ref[:, vs] = out.astype(out_ref.dtype)

        g = b_c[L - 1:L, :]
        a_c = g - b_c + li_c
        m_new = jnp.maximum(g + m_prev, jnp.max(a_c, axis=0, keepdims=True))
        decay = jnp.exp(g + m_prev - m_new)
        kw = k * jnp.exp(a_c - m_new)
        c_scr[hd] = decay * c_scr[hd] + _dot_tn(kw.astype(BF16), vb)
        n_scr[hd:hd + 1, :] = decay * n_prev + jnp.sum(kw, axis=0, keepdims=True)
        m_scr[hd:hd + 1, :] = jnp.broadcast_to(m_new, (1, LANES))


def mlstm_scan(proj, gates_col, gates_row, gate_bias, conv_w, head_norm, batch, *, chunk, heads=M_HEADS,
               dqk=M_DQK, dv=M_DV):
    rows = proj.shape[0]
    s_len = rows // batch
    L = min(chunk, s_len)
    nc = s_len // L
    qk_w = heads * dqk
    v_w = heads * dv
    assert qk_w % LANES == 0 and v_w % qk_w == 0
    bias_col = gate_bias.reshape(1, 2 * heads)
    bias_row = gate_bias.reshape(2 * heads, 1)
    kern = functools.partial(_mlstm_kernel, heads=heads, dqk=dqk, dv=dv)
    return pl.pallas_call(
        kern,
        grid=(batch, nc),
        in_specs=[
            pl.BlockSpec((L, qk_w), lambda b, c: (b * nc + c, 0)),
            pl.BlockSpec((L, qk_w), lambda b, c: (b * nc + c, 1)),
            pl.BlockSpec((L, v_w), lambda b, c: (b * nc + c, 2 * qk_w // v_w)),
            pl.BlockSpec((L, v_w), lambda b, c: (b * nc + c, 2 * qk_w // v_w + 1)),
            pl.BlockSpec((L, 2 * heads), lambda b, c: (b * nc + c, 0)),
            pl.BlockSpec((1, 2 * heads, L), lambda b, c: (b, 0, c)),
            pl.BlockSpec((1, 2 * heads), lambda b, c: (0, 0)),
            pl.BlockSpec((2 * heads, 1), lambda b, c: (0, 0)),
            pl.BlockSpec((CONV_K, 2 * qk_w), lambda b, c: (0, 0)),
            pl.BlockSpec((1, v_w), lambda b, c: (0, 0)),
        ],
        out_specs=pl.BlockSpec((L, v_w), lambda b, c: (b * nc + c, 0)),
        out_shape=jax.ShapeDtypeStruct((rows, v_w), BF16),
        scratch_shapes=[
            pltpu.VMEM((heads, dqk, dv), F32),
            pltpu.VMEM((heads, dqk), F32),
            pltpu.VMEM((heads, LANES), F32),
            pltpu.VMEM((SUBLANES, qk_w), F32),
            pltpu.VMEM((SUBLANES, qk_w), F32),
        ],
        compiler_params=_cparams(("parallel", "arbitrary")),
        name="mlstm_scan",
    )(proj, proj, proj, proj, gates_col, gates_row, bias_col, bias_row, conv_w, head_norm.reshape(1, v_w))


def _hgrn_kernel(q_ref, f_ref, i_ref, g_ref, lb_ref, norm_ref, out_ref, st_scr, *, heads, dk, dv, layer):
    L = q_ref.shape[0]

    @pl.when(pl.program_id(1) == 0)
    def _():
        st_scr[...] = jnp.zeros_like(st_scr)

    lbp = lb_ref[...]
    e = jnp.exp(lbp - jnp.max(lbp, axis=0, keepdims=True))
    p = e / jnp.sum(e, axis=0, keepdims=True)
    lb = jnp.zeros((1, lbp.shape[1]), F32)
    for j in range(1, layer + 1):
        lb = lb + p[j:j + 1]

    f = lb + (1.0 - lb) * _sigmoid(f_ref[...])
    lg = jnp.log(f)
    k_all = 1.0 - f
    t_idx = lax.broadcasted_iota(jnp.int32, (L, L), 0)
    s_idx = lax.broadcasted_iota(jnp.int32, (L, L), 1)
    tri = (s_idx <= t_idx).astype(F32)
    b_all = jnp.dot(tri, lg, precision=lax.Precision.HIGHEST, preferred_element_type=F32)
    level = jnp.where(t_idx > s_idx, 31 - lax.clz(t_idx ^ s_idx), jnp.where(t_idx == s_idx, -1, -2))
    row = lax.broadcasted_iota(jnp.int32, (L, 1), 0)
    n_levels = int(math.log2(L))
    k_slices = [slice(hd * dk, (hd + 1) * dk) for hd in range(heads)]
    q_all = q_ref[...]
    qb_all = q_all.astype(BF16)

    def scores(qx, kx, lw, acc):
        keep = level == lw
        return [jnp.where(keep, _dot_nt(qx[:, sl], kx[:, sl]), acc[i]) for i, sl in enumerate(k_slices)]

    a = scores(qb_all, k_all.astype(BF16), -1, [0.0] * heads)
    a = scores(qb_all, (k_all * pltpu.roll(f, L - 1, 0)).astype(BF16), 0, a)
    fw = jnp.where((row & 1) == 1, pltpu.roll(b_all, 1, 0), b_all)
    for lw in range(1, n_levels):
        w = 1 << lw
        odd = ((row >> lw) & 1) == 1
        gw = pltpu.roll(fw, L - w, 0)
        e = jnp.exp(jnp.where(odd, b_all, gw) - jnp.where(odd, fw, b_all))
        a = scores((q_all * e).astype(BF16), (k_all * e).astype(BF16), lw, a)
        fw = jnp.where(odd, pltpu.roll(fw, w, 0), fw)

    g = b_all[L - 1:L, :]
    qe_all = (q_all * jnp.exp(b_all)).astype(BF16)
    ke_all = (k_all * jnp.exp(g - b_all)).astype(BF16)
    eg = jnp.exp(g)
    ib_all = i_ref[...].astype(BF16)
    for hd in range(heads):
        ks = k_slices[hd]
        vs = slice(hd * dv, (hd + 1) * dv)
        st = st_scr[hd]
        o = _dot(a[hd].astype(BF16), ib_all[:, vs]) + _dot_nt(qe_all[:, ks], st.astype(BF16))
        st_scr[hd] = st * eg[:, ks] + _dot_tn(ib_all[:, vs], ke_all[:, ks])
        out = _rms(o) * norm_ref[:, vs] * _silu(g_ref[:, vs])
        out_ref[:, vs] = out.astype(out_ref.dtype)


def hgrn_scan(proj, col_block0, lower_bounds, head_norm, batch, layer, *, chunk, heads=H_HEADS, dk=H_DK, dv=H_DV):
    rows = proj.shape[0]
    s_len = rows // batch
    L = min(chunk, s_len)
    nc = s_len // L
    assert L & (L - 1) == 0 and dk == dv
    w = heads * dk
    depth = lower_bounds.shape[0]
    kern = functools.partial(_hgrn_kernel, heads=heads, dk=dk, dv=dv, layer=layer)
    blk = lambda o: pl.BlockSpec((L, w), functools.partial(lambda b, c, o: (b * nc + c, o), o=col_block0 + o))
    return pl.pallas_call(
        kern,
        grid=(batch, nc),
        in_specs=[blk(0), blk(1), blk(2), blk(3),
                  pl.BlockSpec((depth, w), lambda b, c: (0, 0)),
                  pl.BlockSpec((1, w), lambda b, c: (0, 0))],
        out_specs=pl.BlockSpec((L, w), lambda b, c: (b * nc + c, 0)),
        out_shape=jax.ShapeDtypeStruct((rows, w), BF16),
        scratch_shapes=[pltpu.VMEM((heads, dv, dk), F32)],
        compiler_params=_cparams(("parallel", "arbitrary")),
        name="hgrn_scan",
    )(proj, proj, proj, proj, lower_bounds, head_norm.reshape(1, w))


def _ffn(u, w_gate_up, w_down, layer):
    d_ff, d = w_down.shape[1:]
    act, w_dn = matmul(u, RawWeight(w_gate_up, layer, FFN_UP_TN), d_ff, w_col_offsets=(0, d_ff),
                       epilogue=_epi_swiglu, out_dtype=BF16, side_cast=RawWeight(w_down, layer, MM_TN),
                       name="ffn_gate_up")
    return matmul(act, w_dn, d, out_dtype=BF16, name="ffn_down")


def _token_mixers(u, layer, batch, w_in, gate_bias, conv_w, m_norm, lower_bounds, h_norm,
                  w_bm, w_bh, w_gate, b_gate, w_out):
    rows, d = u.shape
    s_len = rows // batch
    gate_lo = 2 * M_HEADS * M_DQK + 2 * M_HEADS * M_DV
    n_gates = 2 * M_HEADS
    w_in_t = jnp.swapaxes(w_in, 1, 2)
    proj, gates_col = matmul(u, RawWeightT(w_in_t, layer, MM_TN, 0, gate_lo, n_gates), w_in.shape[2] - n_gates,
                             side_proj=RawWeightT(w_in_t, layer, n_gates, gate_lo), name="mix_in_proj")
    gates_row = gates_col.reshape(batch, s_len, n_gates).transpose(0, 2, 1)
    ym = mlstm_scan(proj, gates_col, gates_row, gate_bias, conv_w, m_norm, batch, chunk=MLSTM_CHUNK)
    yh = hgrn_scan(proj, gate_lo // (H_HEADS * H_DK), lower_bounds, h_norm, batch, layer, chunk=HGRN_CHUNK)
    gate = matmul(u, RawWeight(w_gate, layer, MM_TN), 2 * d, epilogue=_epi_sigmoid_bias,
                  extras=((b_gate.reshape(1, 2 * d), 0),), out_dtype=BF16, name="merge_gate")
    y_m = matmul(ym, cast_tiled_bf16(w_bm, layer, MM_WIDE_TN), d, name="branch_mlstm")
    z = matmul(yh, cast_tiled_bf16(w_bh, layer, MM_WIDE_TN), d, epilogue=_epi_merge,
               extras=((gate, 0), (gate, d), (y_m, 0)), out_dtype=BF16, name="branch_hgrn_merge")
    return matmul(z, RawWeight(w_out, layer, MM_TN), d, out_dtype=BF16, name="mix_out")


def _xattn(u, h, mem2, mem_norm, batch, layer, w_q, w_kv, w_o, post, next_pre):
    memn = rmsnorm_bf16(mem2, mem_norm)
    kv = matmul(memn, RawWeight(w_kv, layer, MM_TN), w_kv.shape[2], out_dtype=BF16, name="xattn_kv")
    q = matmul(u, cast_tiled_bf16(w_q, layer, MM_WIDE_TN), w_q.shape[2], out_dtype=BF16, name="xattn_q")
    return xattn_out(q, kv, cast_tiled_bf16(w_o, layer, w_o.shape[2]), h, post, next_pre, batch, X_HEADS, X_DH)


def kernel(x, mem, ffn1_pre_norm, ffn1_w_gate_up, ffn1_w_down, ffn1_post_norm, mix_pre_norm, mix_w_in, mlstm_gate_bias, mlstm_conv, mlstm_head_norm, hgrn_lower_bounds, hgrn_head_norm, branch_w_mlstm, branch_w_hgrn, merge_w_gate, merge_b_gate, mix_w_out, mix_post_norm, xattn_pre_norm, xattn_mem_norm, xattn_w_q, xattn_w_kv, xattn_w_o, xattn_post_norm, ffn2_pre_norm, ffn2_w_gate_up, ffn2_w_down, ffn2_post_norm):
    batch, s_len, d = x.shape
    n_mem = mem.shape[1]
    depth = ffn1_pre_norm.shape[0]

    h = x.reshape(batch * s_len, d)
    mem2 = mem.reshape(batch * n_mem, d)
    u = rmsnorm_bf16(h, ffn1_pre_norm[0])
    for l in range(depth):
        y = _ffn(u, ffn1_w_gate_up, ffn1_w_down, l)
        h, u = combine(y, h, ffn1_post_norm[l], 0.5, mix_pre_norm[l])

        y = _token_mixers(u, l, batch, mix_w_in, mlstm_gate_bias[l], mlstm_conv[l], mlstm_head_norm[l],
                          hgrn_lower_bounds, hgrn_head_norm[l], branch_w_mlstm, branch_w_hgrn, merge_w_gate,
                          merge_b_gate[l], mix_w_out)
        h, u = combine(y, h, mix_post_norm[l], 1.0, xattn_pre_norm[l])

        h, u = _xattn(u, h, mem2, xattn_mem_norm[l], batch, l, xattn_w_q, xattn_w_kv, xattn_w_o,
                      xattn_post_norm[l], ffn2_pre_norm[l])

        y = _ffn(u, ffn2_w_gate_up, ffn2_w_down, l)
        nxt = ffn1_pre_norm[l + 1] if l + 1 < depth else None
        h, u = combine(y, h, ffn2_post_norm[l], 0.5, nxt)
    return h.reshape(batch, s_len, d)
```
